```python
import jax, jax.numpy as jnp
from jax import lax
import numpy as np

D_MODEL = 1024
BATCH = 8
SEQ = 8192
DEPTH = 2

GRID_W = 64
CTX_LEN = 256
N_MIXERS = 2
NORM_EPS = 1e-6
N_HEADS = 16
N_KV_HEADS = 4
HEAD_DIM = D_MODEL // N_HEADS
KV_GROUP = N_HEADS // N_KV_HEADS
Q_BLOCK = 128
ROPE_THETA = 10000.0
ROPE_PAIRS = HEAD_DIM // 4
RWKV_HEAD = 64
RWKV_HEADS = D_MODEL // RWKV_HEAD
DECAY_LORA = 64
ICLR_LORA = 64
GATE_LORA = 128
GN_EPS = 64e-5
D_FF = -(-8 * D_MODEL // (3 * 256)) * 256

kernel_name = "hybrid_gqa_rwkv7_prefix_dit"


def _rmsnorm(x, g):
    xf = x.astype(jnp.float32)
    y = xf * lax.rsqrt(jnp.mean(xf * xf, axis=-1, keepdims=True) + NORM_EPS)
    return (y * g.astype(jnp.float32)).astype(x.dtype)


def _axial_rope_tables(rows, dtype):
    L = rows * GRID_W
    row = jnp.repeat(jnp.arange(rows, dtype=jnp.float32), GRID_W, total_repeat_length=L)
    col = jnp.tile(jnp.arange(GRID_W, dtype=jnp.float32), rows)
    inv_freq = ROPE_THETA ** (-jnp.arange(ROPE_PAIRS, dtype=jnp.float32) / ROPE_PAIRS)
    ang = jnp.stack([row[:, None] * inv_freq, col[:, None] * inv_freq], axis=1)
    return jnp.cos(ang).astype(dtype), jnp.sin(ang).astype(dtype)


def _apply_axial_rope(x, cos, sin):
    xs = x.reshape(x.shape[:-1] + (2, 2, ROPE_PAIRS))
    x1, x2 = xs[..., 0, :], xs[..., 1, :]
    cb, sb = cos[None, :, None], sin[None, :, None]
    out = jnp.stack([x1 * cb - x2 * sb, x1 * sb + x2 * cb], axis=-2)
    return out.reshape(x.shape)


def _gqa_softmax(q, k, v):
    s = jnp.einsum('bqkgd,bskd->bkgqs', q, k).astype(jnp.float32) * (HEAD_DIM ** -0.5)
    p = jax.nn.softmax(s, axis=-1).astype(v.dtype)
    return jnp.einsum('bkgqs,bskd->bqkgd', p, v)


def _attention_mixer(h, hc, wqkv, q_gain, k_gain, wo, cos, sin, need_ctx):
    B, L, _ = h.shape
    C = hc.shape[1]
    nq = N_HEADS * HEAD_DIM
    nk = N_KV_HEADS * HEAD_DIM
    qkv = h @ wqkv
    q = qkv[..., :nq].reshape(B, L, N_HEADS, HEAD_DIM)
    k = qkv[..., nq:nq + nk].reshape(B, L, N_KV_HEADS, HEAD_DIM)
    v = qkv[..., nq + nk:].reshape(B, L, N_KV_HEADS, HEAD_DIM)
    q = _apply_axial_rope(_rmsnorm(q, q_gain), cos, sin)
    k = _apply_axial_rope(_rmsnorm(k, k_gain), cos, sin)
    qkv_c = hc @ (wqkv if need_ctx else wqkv[:, nq:])
    kv_c = qkv_c[..., -2 * nk:]
    k_c = _rmsnorm(kv_c[..., :nk].reshape(B, C, N_KV_HEADS, HEAD_DIM), k_gain)
    v_c = kv_c[..., nk:].reshape(B, C, N_KV_HEADS, HEAD_DIM)
    k_all = jnp.concatenate([k_c, k], axis=1)
    v_all = jnp.concatenate([v_c, v], axis=1)
    nb = L // Q_BLOCK
    q_blocks = q.reshape(B, nb, Q_BLOCK, N_KV_HEADS, KV_GROUP, HEAD_DIM).transpose(1, 0, 2, 3, 4, 5)
    o = lax.map(lambda qb: _gqa_softmax(qb, k_all, v_all), q_blocks)
    o = o.transpose(1, 0, 2, 3, 4, 5).reshape(B, L, nq)
    y = o @ wo
    yc = None
    if need_ctx:
        q_c = _rmsnorm(qkv_c[..., :nq].reshape(B, C, N_KV_HEADS, KV_GROUP, HEAD_DIM), q_gain)
        yc = _gqa_softmax(q_c, k_c, v_c).reshape(B, C, nq) @ wo
    return y, yc


def _centred_shift_delta(u):
    pad = jnp.pad(u, ((0, 0), (1, 1), (0, 0)))
    return 0.5 * (pad[:, :-2] + pad[:, 2:]) - u


def _rwkv_branches(u, mix, w_rkv, w0, w1, w2, a0, a1, a2, g1, g2, k_k, k_a, readout):
    B, T, _ = u.shape
    xx = _centred_shift_delta(u)

    def lerp(j):
        return u + xx * mix[j]

    sel = (0, 2, 3) if readout else (2, 3)
    proj = jnp.einsum('jbtd,jde->jbte', jnp.stack([lerp(s) for s in sel]), w_rkv[-len(sel):])
    k = proj[-2].astype(jnp.float32)

    def heads(z):
        return z.reshape(B, T, RWKV_HEADS, RWKV_HEAD).astype(jnp.float32)

    kk = heads(k * k_k)
    kk = kk * lax.rsqrt(jnp.maximum(jnp.sum(kk * kk, axis=-1, keepdims=True), 1e-24))
    xw, xa = lerp(1), lerp(4)
    dirs = []
    for d in range(2):
        w_log = -jax.nn.softplus(-(w0[d] + jnp.tanh(xw @ w1[d]) @ w2[d]).astype(jnp.float32)) - 0.5
        decay = jnp.exp(-jnp.exp(w_log))
        a = jax.nn.sigmoid((a0[d] + (xa @ a1[d]) @ a2[d]).astype(jnp.float32))
        k_d = k * (1.0 + (a - 1.0) * k_a)
        dirs.append((heads(decay), heads(k_d), heads(a)))
    r = heads(proj[0]) if readout else None
    g = (jax.nn.sigmoid(lerp(5) @ g1) @ g2) if readout else None
    return r, heads(proj[-1]), kk, dirs, g


def _wkv_scan(S0, decay, k, v, kk, a, r, reverse):
    xs = (decay, k, v, kk, a) + (() if r is None else (r,))

    def step(S, inp):
        w_t, k_t, v_t, kk_t, a_t = inp[:5]
        sa = jnp.einsum('bhvk,bhk->bhv', S, -kk_t)
        S = (S * w_t[:, :, None, :] + sa[..., None] * (kk_t * a_t)[:, :, None, :]
             + v_t[..., None] * k_t[:, :, None, :])
        y = jnp.einsum('bhvk,bhk->bhv', S, inp[5]) if len(inp) == 6 else None
        return S, y

    S, ys = lax.scan(step, S0, tuple(jnp.moveaxis(z, 1, 0) for z in xs), reverse=reverse)
    return S, (None if ys is None else jnp.moveaxis(ys, 0, 1))


def _rwkv_readout(wkv, r, v, k_dirs, g, r_k, ln_g, ln_b, wo, dtype):
    B, T, H, N = wkv.shape
    mu = jnp.mean(wkv, axis=-1, keepdims=True)
    var = jnp.mean(jnp.square(wkv - mu), axis=-1, keepdims=True)
    gn = ((wkv - mu) * lax.rsqrt(var + GN_EPS)).reshape(B, T, H * N) * ln_g + ln_b
    bonus = sum(jnp.sum(r * k_d * r_k, axis=-1, keepdims=True) for k_d in k_dirs) * v
    out = (gn + bonus.reshape(B, T, H * N)) * g
    return out.astype(dtype) @ wo


def _rwkv7_mixer(h, hc, mix, w_rkv, w0, w1, w2, a0, a1, a2, g1, g2, k_k, k_a, r_k,
                 ln_g, ln_b, wo, need_ctx):
    params = (mix, w_rkv, w0, w1, w2, a0, a1, a2, g1, g2, k_k, k_a)
    r, v, kk, dirs, g = _rwkv_branches(h, *params, readout=True)
    rc, vc, kkc, dirs_c, gc = _rwkv_branches(hc, *params, readout=need_ctx)
    B = h.shape[0]
    S0 = jnp.zeros((B, RWKV_HEADS, RWKV_HEAD, RWKV_HEAD), jnp.float32)
    wkv = 0.0
    wkv_c = 0.0
    for d, rev in enumerate((False, True)):
        decay_c, k_c, a_c = dirs_c[d]
        S_ctx, yc = _wkv_scan(S0, decay_c, k_c, vc, kkc, a_c, rc, rev)
        decay, k_d, a_d = dirs[d]
        _, y = _wkv_scan(S_ctx, decay, k_d, v, kk, a_d, r, rev)
        wkv = wkv + y
        if need_ctx:
            wkv_c = wkv_c + yc
    y = _rwkv_readout(wkv, r, v, [dd[1] for dd in dirs], g, r_k, ln_g, ln_b, wo, h.dtype)
    yc = None
    if need_ctx:
        yc = _rwkv_readout(wkv_c, rc, vc, [dd[1] for dd in dirs_c], gc, r_k, ln_g, ln_b, wo, hc.dtype)
    return y, yc


def _swiglu(h, wg, wu, wd):
    return (jax.nn.silu(h @ wg) * (h @ wu)) @ wd


def setup_inputs(seed: int = 0) -> dict:
    key = jax.random.key(seed)
    ks = iter(jax.random.split(key, 40))
    f32 = jnp.float32
    n_attn = len(range(0, DEPTH, N_MIXERS))
    n_rwkv = len(range(1, DEPTH, N_MIXERS))
    D = D_MODEL

    def nrm(shape, std):
        return jax.random.normal(next(ks), shape, f32) * std

    def unif(shape, lo, hi):
        return jax.random.uniform(next(ks), shape, f32, lo, hi)

    nqkv = (N_HEADS + 2 * N_KV_HEADS) * HEAD_DIM
    return {
        "x": nrm((BATCH, SEQ, D), 1.0),
        "c": nrm((BATCH, D), 1.0),
        "ctx": nrm((BATCH, CTX_LEN, D), 1.0),
        "c_ctx": nrm((D,), 1.0),
        "mod_w": nrm((DEPTH, D, 6 * D), 0.3 * D ** -0.5),
        "mod_b": nrm((DEPTH, 6 * D), 0.02),
        "norm1_g": 1.0 + nrm((DEPTH, D), 0.02),
        "norm2_g": 1.0 + nrm((DEPTH, D), 0.02),
        "ffn_wg": nrm((DEPTH, D, D_FF), D ** -0.5),
        "ffn_wu": nrm((DEPTH, D, D_FF), D ** -0.5),
        "ffn_wd": nrm((DEPTH, D_FF, D), D_FF ** -0.5),
        "attn_wqkv": nrm((n_attn, D, nqkv), D ** -0.5),
        "attn_q_gain": 1.0 + nrm((n_attn, HEAD_DIM), 0.02),
        "attn_k_gain": 1.0 + nrm((n_attn, HEAD_DIM), 0.02),
        "attn_wo": nrm((n_attn, N_HEADS * HEAD_DIM, D), (N_HEADS * HEAD_DIM) ** -0.5),
        "rwkv_mix": unif((n_rwkv, 6, D), 0.0, 1.0),
        "rwkv_wrkv": nrm((n_rwkv, 3, D, D), D ** -0.5),
        "rwkv_w0": unif((n_rwkv, 2, D), -6.0, -1.0),
        "rwkv_w1": nrm((n_rwkv, 2, D, DECAY_LORA), D ** -0.5),
        "rwkv_w2": nrm((n_rwkv, 2, DECAY_LORA, D), 0.5 * DECAY_LORA ** -0.5),
        "rwkv_a0": nrm((n_rwkv, 2, D), 0.1),
        "rwkv_a1": nrm((n_rwkv, 2, D, ICLR_LORA), D ** -0.5),
        "rwkv_a2": nrm((n_rwkv, 2, ICLR_LORA, D), 0.5 * ICLR_LORA ** -0.5),
        "rwkv_g1": nrm((n_rwkv, D, GATE_LORA), D ** -0.5),
        "rwkv_g2": nrm((n_rwkv, GATE_LORA, D), GATE_LORA ** -0.5),
        "rwkv_k_k": 0.85 + nrm((n_rwkv, D), 0.05),
        "rwkv_k_a": 1.0 + nrm((n_rwkv, D), 0.05),
        "rwkv_r_k": nrm((n_rwkv, RWKV_HEADS, RWKV_HEAD), 0.1),
        "rwkv_ln_g": 1.0 + nrm((n_rwkv, D), 0.02),
        "rwkv_ln_b": nrm((n_rwkv, D), 0.02),
        "rwkv_wo": nrm((n_rwkv, D, D), D ** -0.5),
        "final_g": 1.0 + nrm((D,), 0.02),
    }


def reference(x, c, ctx, c_ctx, mod_w, mod_b, norm1_g, norm2_g, ffn_wg, ffn_wu, ffn_wd,
              attn_wqkv, attn_q_gain, attn_k_gain, attn_wo,
              rwkv_mix, rwkv_wrkv, rwkv_w0, rwkv_w1, rwkv_w2, rwkv_a0, rwkv_a1, rwkv_a2,
              rwkv_g1, rwkv_g2, rwkv_k_k, rwkv_k_a, rwkv_r_k, rwkv_ln_g, rwkv_ln_b, rwkv_wo,
              final_g):
    B, L, D = x.shape
    rows = L // GRID_W
    cos, sin = _axial_rope_tables(rows, x.dtype)
    sc = jax.nn.silu(c)
    scc = jax.nn.silu(c_ctx)
    xc = ctx
    for i in range(DEPTH):
        last = i == DEPTH - 1
        j = i // N_MIXERS
        m = (sc @ mod_w[i] + mod_b[i]).reshape(B, 6, 1, D)
        mc = (scc @ mod_w[i] + mod_b[i]).reshape(6, D)
        h = _rmsnorm(x, norm1_g[i]) * (1.0 + m[:, 1]) + m[:, 0]
        hc = _rmsnorm(xc, norm1_g[i]) * (1.0 + mc[1]) + mc[0]
        if i % N_MIXERS == 0:
            y, yc = _attention_mixer(h, hc, attn_wqkv[j], attn_q_gain[j], attn_k_gain[j],
                                     attn_wo[j], cos, sin, not last)
        else:
            y, yc = _rwkv7_mixer(h, hc, rwkv_mix[j], rwkv_wrkv[j], rwkv_w0[j], rwkv_w1[j],
                                 rwkv_w2[j], rwkv_a0[j], rwkv_a1[j], rwkv_a2[j], rwkv_g1[j],
                                 rwkv_g2[j], rwkv_k_k[j], rwkv_k_a[j], rwkv_r_k[j],
                                 rwkv_ln_g[j], rwkv_ln_b[j], rwkv_wo[j], not last)
        x = x + m[:, 2] * y
        h2 = _rmsnorm(x, norm2_g[i]) * (1.0 + m[:, 4]) + m[:, 3]
        x = x + m[:, 5] * _swiglu(h2, ffn_wg[i], ffn_wu[i], ffn_wd[i])
        if not last:
            xc = xc + mc[2] * yc
            hc2 = _rmsnorm(xc, norm2_g[i]) * (1.0 + mc[4]) + mc[3]
            xc = xc + mc[5] * _swiglu(hc2, ffn_wg[i], ffn_wu[i], ffn_wd[i])
    return _rmsnorm(x, final_g)
```

```python
import functools

import jax
import jax.numpy as jnp
import numpy as np
from jax import lax
from jax.experimental import pallas as pl
from jax.experimental.pallas import tpu as pltpu

F32 = jnp.float32
BF16 = jnp.bfloat16

NORM_EPS = 1e-6
GN_EPS = 64e-5
GRID_W = 64
N_HEADS = 16
N_KV_HEADS = 4
KV_GROUP = N_HEADS // N_KV_HEADS
HEAD_DIM = 64
ROPE_THETA = 10000.0
ROPE_PAIRS = HEAD_DIM // 4
RWKV_HEAD = 64
DECAY_SCALE = float(np.exp(-0.5))

V7X_LANES = 128
V7X_MXU_DIM = 256
V7X_VMEM_LIMIT_BYTES = 60000 * 1024

ROW_TILE = 256
GROUP_LANES = V7X_MXU_DIM
HEADS_PER_GROUP = GROUP_LANES // HEAD_DIM
WKV_CHUNK = 64
N_MOD_ROWS = 8


def _cparams(n_axes):
    return pltpu.CompilerParams(
        dimension_semantics=("arbitrary",) * n_axes,
        vmem_limit_bytes=V7X_VMEM_LIMIT_BYTES,
    )


def _const_spec(shape):
    nd = len(shape)
    return pl.BlockSpec(shape, lambda *_: (0,) * nd, pipeline_mode=pl.Buffered(1))


def _dot(a, b):
    return jnp.dot(a, b, preferred_element_type=F32)


def _dot_nt(a, b):
    return lax.dot_general(a, b, (((1,), (1,)), ((), ())), preferred_element_type=F32)


def _dot_tn(a, b):
    return lax.dot_general(a, b, (((0,), (0,)), ((), ())), preferred_element_type=F32)


def _norm_mod(x, g, shift, scale):
    ms = jnp.mean(x * x, axis=-1, keepdims=True)
    return (x * lax.rsqrt(ms + NORM_EPS) * g) * (1.0 + scale) + shift


def _split3(x):
    hi = x.astype(BF16)
    r1 = x - hi.astype(F32)
    mid = r1.astype(BF16)
    lo = (r1 - mid.astype(F32)).astype(BF16)
    return hi, mid, lo


def _group_sum(x, seg_ref, exact=False):
    seg = seg_ref[...]
    outs = []
    for j in range(x.shape[1] // GROUP_LANES):
        xs = x[:, j * GROUP_LANES:(j + 1) * GROUP_LANES]
        if exact:
            hi, mid, lo = _split3(xs)
            outs.append(_dot(hi, seg) + _dot(mid, seg) + _dot(lo, seg))
        else:
            outs.append(_dot(xs.astype(BF16), seg))
    return outs[0] if len(outs) == 1 else jnp.concatenate(outs, axis=1)


def _mod_kernel(c_ref, w_ref, b_ref, o_ref):
    c = c_ref[...]
    s = c * jax.nn.sigmoid(c)
    o_ref[0] = _dot(s.astype(BF16), w_ref[0]) + b_ref[0]


def _modulation(c_rows, mod_w, mod_b):
    depth, d, n = mod_w.shape
    rows = c_rows.shape[0]
    tn = n // 4
    return pl.pallas_call(
        _mod_kernel,
        out_shape=jax.ShapeDtypeStruct((depth, rows, n), F32),
        grid=(depth, n // tn),
        in_specs=[
            pl.BlockSpec((rows, d), lambda i, j: (0, 0)),
            pl.BlockSpec((1, d, tn), lambda i, j: (i, 0, j)),
            pl.BlockSpec((1, 1, tn), lambda i, j: (i, 0, j)),
        ],
        out_specs=pl.BlockSpec((1, rows, tn), lambda i, j: (i, 0, j)),
        compiler_params=_cparams(2),
        name="modulation",
    )(c_rows, mod_w.astype(BF16), mod_b.reshape(depth, 1, n))


def _rope(x, cos, sin_signed):
    w = x.shape[1]
    lane = lax.broadcasted_iota(jnp.int32, x.shape, 1)
    first_half = (lane % (2 * ROPE_PAIRS)) < ROPE_PAIRS
    partner = jnp.where(first_half, pltpu.roll(x, w - ROPE_PAIRS, 1), pltpu.roll(x, ROPE_PAIRS, 1))
    return x * cos + partner * sin_signed


def _qkv_kernel(x_ref, mod_ref, g_ref, w_ref, qg_ref, kg_ref, cos_ref, sin_ref, seg_ref,
                ek_ref, ev_ref, q_ref, k_ref, v_ref, *, nq, nk):
    mod = mod_ref[0, 0]
    h = _norm_mod(x_ref[0], g_ref[...], mod[0:1], mod[1:2]).astype(BF16)
    qkv = _dot(h, w_ref[...])
    q, k, v = qkv[:, :nq], qkv[:, nq:nq + nk], qkv[:, nq + nk:]
    cos2, sin2 = cos_ref[...], sin_ref[...]

    def head_norm_rope(z, gain):
        reps = z.shape[1] // V7X_LANES
        cos = jnp.concatenate([cos2] * reps, axis=1)
        sin = jnp.concatenate([sin2] * reps, axis=1)
        ss = _group_sum(z * z, seg_ref)
        zn = z * lax.rsqrt(ss * (1.0 / HEAD_DIM) + NORM_EPS) * gain
        return _rope(zn, cos, sin)

    qn = head_norm_rope(q, qg_ref[...]) * (HEAD_DIM ** -0.5)
    q_ref[0] = qn.astype(BF16)
    kn = head_norm_rope(k, kg_ref[...]).astype(BF16)
    k_t = _dot(kn, ek_ref[...]).astype(BF16)
    v_t = _dot(v.astype(BF16), ev_ref[...]).astype(BF16)
    for j in range(N_KV_HEADS):
        k_ref[0, j] = k_t[:, j * GROUP_LANES:(j + 1) * GROUP_LANES]
        v_ref[0, j] = v_t[:, j * V7X_LANES:(j + 1) * V7X_LANES]


def _qkv_project(xs, mods, g1, wqkv, q_gain, k_gain, cos2, sin2, seg, ctx_tiles):
    b, nt, d = xs.shape
    nq, nk = N_HEADS * HEAD_DIM, N_KV_HEADS * HEAD_DIM
    tm = ROW_TILE
    ek = np.zeros((nk, N_KV_HEADS * GROUP_LANES), np.float32)
    ev = np.zeros((nk, N_KV_HEADS * V7X_LANES), np.float32)
    for j in range(N_KV_HEADS):
        for dd in range(HEAD_DIM):
            for r in range(GROUP_LANES // HEAD_DIM):
                ek[j * HEAD_DIM + dd, j * GROUP_LANES + r * HEAD_DIM + dd] = 1.0
            for r in range(V7X_LANES // HEAD_DIM):
                ev[j * HEAD_DIM + dd, j * V7X_LANES + r * HEAD_DIM + dd] = 1.0
    kern = functools.partial(_qkv_kernel, nq=nq, nk=nk)
    return pl.pallas_call(
        kern,
        out_shape=(
            jax.ShapeDtypeStruct((b, nt, nq), BF16),
            jax.ShapeDtypeStruct((b, N_KV_HEADS, nt, GROUP_LANES), BF16),
            jax.ShapeDtypeStruct((b, N_KV_HEADS, nt, V7X_LANES), BF16),
        ),
        grid=(b, nt // tm),
        in_specs=[
            pl.BlockSpec((1, tm, d), lambda i, t: (i, t, 0)),
            pl.BlockSpec((1, 1, N_MOD_ROWS, d), lambda i, t: (i, jnp.where(t < ctx_tiles, 0, 1), 0, 0)),
            _const_spec((1, d)),
            _const_spec((d, nq + 2 * nk)),
            _const_spec((1, nq)),
            _const_spec((1, nk)),
            pl.BlockSpec((tm, V7X_LANES), lambda i, t: (t, 0)),
            pl.BlockSpec((tm, V7X_LANES), lambda i, t: (t, 0)),
            _const_spec((GROUP_LANES, GROUP_LANES)),
            _const_spec(ek.shape),
            _const_spec(ev.shape),
        ],
        out_specs=(
            pl.BlockSpec((1, tm, nq), lambda i, t: (i, t, 0)),
            pl.BlockSpec((1, N_KV_HEADS, tm, GROUP_LANES), lambda i, t: (i, 0, t, 0)),
            pl.BlockSpec((1, N_KV_HEADS, tm, V7X_LANES), lambda i, t: (i, 0, t, 0)),
        ),
        compiler_params=_cparams(2),
        name="qkv_project",
    )(xs, mods, g1.reshape(1, d), wqkv.astype(BF16),
      jnp.tile(q_gain, N_HEADS).reshape(1, nq), jnp.tile(k_gain, N_KV_HEADS).reshape(1, nk),
      cos2, sin2, seg, jnp.asarray(ek, BF16), jnp.asarray(ev, BF16))


def _attn_kernel(q_ref, k_ref, v_ref, o_ref, qs_ref, m_ref, l_ref, acc_ref, *,
                 tq, tk, ctx_tiles, ctx_chunks, all_chunks):
    qi = pl.program_id(2)
    q = q_ref[0]
    lane_head = lax.broadcasted_iota(jnp.int32, q.shape, 1) // HEAD_DIM
    for g in range(KV_GROUP):
        qs_ref[g * tq:(g + 1) * tq, :] = jnp.where(lane_head == g, q, jnp.zeros_like(q))
    m_ref[...] = jnp.full(m_ref.shape, -jnp.inf, F32)
    l_ref[...] = jnp.zeros(l_ref.shape, F32)
    acc_ref[...] = jnp.zeros(acc_ref.shape, F32)
    n_chunks = jnp.where(qi < ctx_tiles, ctx_chunks, all_chunks)

    def body(c, carry):
        start = pl.multiple_of(c * tk, tk)
        kc = k_ref[0, 0, pl.ds(start, tk), :]
        vc = v_ref[0, 0, pl.ds(start, tk), :]
        s = _dot_nt(qs_ref[...], kc)
        m_prev = m_ref[...]
        m_new = jnp.maximum(m_prev, jnp.max(s, axis=1, keepdims=True))
        alpha = jnp.exp(m_prev - m_new)
        p = jnp.exp(s - jnp.concatenate([m_new] * (tk // V7X_LANES), axis=1))
        l_ref[...] = alpha * l_ref[...] + jnp.sum(p, axis=1, keepdims=True)
        acc_ref[...] = alpha * acc_ref[...] + _dot(p.astype(BF16), vc)
        m_ref[...] = m_new
        return carry

    lax.fori_loop(0, n_chunks, body, 0)
    o = acc_ref[...] / l_ref[...]
    lane_odd = (lax.broadcasted_iota(jnp.int32, (tq, V7X_LANES), 1) // HEAD_DIM) == 1
    halves = []
    for pair in range(KV_GROUP // 2):
        even = o[(2 * pair) * tq:(2 * pair + 1) * tq]
        odd = o[(2 * pair + 1) * tq:(2 * pair + 2) * tq]
        halves.append(jnp.where(lane_odd, odd, even))
    o_ref[0] = jnp.concatenate(halves, axis=1).astype(o_ref.dtype)


def _attention(q, k4, v2, ctx_len):
    b, nt, nq = q.shape
    tq = ROW_TILE
    tk = ROW_TILE
    kern = functools.partial(
        _attn_kernel, tq=tq, tk=tk, ctx_tiles=ctx_len // tq,
        ctx_chunks=ctx_len // tk, all_chunks=nt // tk)
    return pl.pallas_call(
        kern,
        out_shape=jax.ShapeDtypeStruct((b, nt, nq), BF16),
        grid=(b, N_KV_HEADS, nt // tq),
        in_specs=[
            pl.BlockSpec((1, tq, GROUP_LANES), lambda i, j, t: (i, t, j)),
            pl.BlockSpec((1, 1, nt, GROUP_LANES), lambda i, j, t: (i, j, 0, 0)),
            pl.BlockSpec((1, 1, nt, V7X_LANES), lambda i, j, t: (i, j, 0, 0)),
        ],
        out_specs=pl.BlockSpec((1, tq, GROUP_LANES), lambda i, j, t: (i, t, j)),
        scratch_shapes=[
            pltpu.VMEM((KV_GROUP * tq, GROUP_LANES), BF16),
            pltpu.VMEM((KV_GROUP * tq, V7X_LANES), F32),
            pltpu.VMEM((KV_GROUP * tq, V7X_LANES), F32),
            pltpu.VMEM((KV_GROUP * tq, V7X_LANES), F32),
        ],
        compiler_params=_cparams(3),
        name="flash_attention",
    )(q, k4, v2)


def _out_ffn_kernel(y_ref, x_ref, mod_ref, g_ref, wo_ref, wg_ref, wu_ref, wd_ref, fg_ref, o_ref, *,
                    final_norm):
    mod = mod_ref[0, 0]
    x1 = x_ref[0] + mod[2:3] * _dot(y_ref[0], wo_ref[...])
    h2 = _norm_mod(x1, g_ref[...], mod[3:4], mod[4:5]).astype(BF16)
    a = _dot(h2, wg_ref[...])
    u = _dot(h2, wu_ref[...])
    hid = (a * jax.nn.sigmoid(a) * u).astype(BF16)
    x2 = x1 + mod[5:6] * _dot(hid, wd_ref[...])
    if final_norm:
        ms = jnp.mean(x2 * x2, axis=-1, keepdims=True)
        x2 = x2 * lax.rsqrt(ms + NORM_EPS) * fg_ref[...]
    o_ref[0] = x2


def _out_ffn(y, xs, mods, g2, wo, wg, wu, wd, final_g, *, x_tile_offset, ctx_tiles, final_norm):
    b, n, d = y.shape
    f = wg.shape[1]
    tm = ROW_TILE
    kern = functools.partial(_out_ffn_kernel, final_norm=final_norm)

    def mod_sel(i, t):
        return (i, jnp.where(t + x_tile_offset < ctx_tiles, 0, 1), 0, 0)

    return pl.pallas_call(
        kern,
        out_shape=jax.ShapeDtypeStruct((b, n, d), F32),
        grid=(b, n // tm),
        in_specs=[
            pl.BlockSpec((1, tm, d), lambda i, t: (i, t, 0)),
            pl.BlockSpec((1, tm, d), lambda i, t: (i, t + x_tile_offset, 0)),
            pl.BlockSpec((1, 1, N_MOD_ROWS, d), mod_sel),
            _const_spec((1, d)),
            _const_spec((d, d)),
            _const_spec((d, f)),
            _const_spec((d, f)),
            _const_spec((f, d)),
            _const_spec((1, d)),
        ],
        out_specs=pl.BlockSpec((1, tm, d), lambda i, t: (i, t, 0)),
        compiler_params=_cparams(2),
        name="out_ffn_final" if final_norm else "out_ffn",
    )(y, xs, mods, g2.reshape(1, d), wo.astype(BF16), wg.astype(BF16), wu.astype(BF16),
      wd.astype(BF16), final_g.reshape(1, d))


def _rwkv_proj_kernel(x_ref, xp_ref, xn_ref, mod_ref, g_ref, mix_ref, wr_ref, wk_ref, wv_ref,
                      w1_ref, w2_ref, a1_ref, a2_ref, g1_ref, g2_ref, w0_ref, a0_ref, kk_ref, ka_ref,
                      seg_ref,
                      r_ref, v_ref, kkn_ref, gate_ref, lw0_ref, lw1_ref, kd0_ref, kd1_ref,
                      b0_ref, b1_ref, *, tm, seq_starts, seq_ends):
    t = pl.program_id(1)
    mod = mod_ref[0, 0]
    g = g_ref[...]
    h = _norm_mod(x_ref[0], g, mod[0:1], mod[1:2])
    h_prev = _norm_mod(xp_ref[0], g, mod[0:1], mod[1:2])[7:8]
    h_next = _norm_mod(xn_ref[0], g, mod[0:1], mod[1:2])[0:1]
    row = lax.broadcasted_iota(jnp.int32, h.shape, 0)
    pos = row + t * tm
    at_start = functools.reduce(jnp.logical_or, [pos == s for s in seq_starts])
    at_end = functools.reduce(jnp.logical_or, [pos == e for e in seq_ends])
    before = jnp.where(row == 0, h_prev, pltpu.roll(h, 1, 0))
    after = jnp.where(row == tm - 1, h_next, pltpu.roll(h, tm - 1, 0))
    before = jnp.where(at_start, 0.0, before)
    after = jnp.where(at_end, 0.0, after)
    xx = 0.5 * (before + after) - h
    mix = mix_ref[...]

    def lerp(j):
        return (h + xx * mix[j:j + 1]).astype(BF16)

    r_ref[0] = _dot(lerp(0), wr_ref[...])
    k = _dot(lerp(2), wk_ref[...])
    v_ref[0] = _dot(lerp(3), wv_ref[...])
    kkr = k * kk_ref[...]
    ss = _group_sum(kkr * kkr, seg_ref)
    kkn = kkr * lax.rsqrt(jnp.maximum(ss, 1e-24))
    kkn_ref[0] = kkn
    xw, xa = lerp(1), lerp(4)
    ka = ka_ref[...]
    for d, (lw_ref, kd_ref, b_ref) in enumerate(((lw0_ref, kd0_ref, b0_ref), (lw1_ref, kd1_ref, b1_ref))):
        z = w0_ref[d:d + 1] + _dot(jnp.tanh(_dot(xw, w1_ref[d])).astype(BF16), w2_ref[d])
        lw_ref[0] = -DECAY_SCALE * jax.nn.sigmoid(z)
        a = jax.nn.sigmoid(a0_ref[d:d + 1] + _dot(_dot(xa, a1_ref[d]).astype(BF16), a2_ref[d]))
        kd_ref[0] = k * (1.0 + (a - 1.0) * ka)
        b_ref[0] = kkn * a
    gate_ref[0] = _dot(jax.nn.sigmoid(_dot(lerp(5), g1_ref[...])).astype(BF16), g2_ref[...])


def _rwkv_project(xs, mods, g1n, mix, w_rkv, w0, w1, w2, a0, a1, a2, gw1, gw2, k_k, k_a, seg,
                  ctx_len):
    b, nt, d = xs.shape
    tm = ROW_TILE
    halo = 8
    per = tm // halo
    n_halo = nt // halo
    kern = functools.partial(_rwkv_proj_kernel, tm=tm, seq_starts=(0, ctx_len),
                             seq_ends=(ctx_len - 1, nt - 1))
    tok = pl.BlockSpec((1, tm, d), lambda i, t: (i, t, 0))
    bf = lambda w: w.astype(BF16)
    outs = pl.pallas_call(
        kern,
        out_shape=tuple(jax.ShapeDtypeStruct((b, nt, d), F32) for _ in range(10)),
        grid=(b, nt // tm),
        in_specs=[
            tok,
            pl.BlockSpec((1, halo, d), lambda i, t: (i, jnp.maximum(t * per - 1, 0), 0)),
            pl.BlockSpec((1, halo, d), lambda i, t: (i, jnp.minimum((t + 1) * per, n_halo - 1), 0)),
            pl.BlockSpec((1, 1, N_MOD_ROWS, d), lambda i, t: (i, jnp.where(t * tm < ctx_len, 0, 1), 0, 0)),
            _const_spec((1, d)),
            _const_spec((N_MOD_ROWS, d)),
            _const_spec((d, d)), _const_spec((d, d)), _const_spec((d, d)),
            _const_spec(w1.shape), _const_spec(w2.shape), _const_spec(a1.shape), _const_spec(a2.shape),
            _const_spec(gw1.shape), _const_spec(gw2.shape),
            _const_spec((2, d)), _const_spec((2, d)), _const_spec((1, d)), _const_spec((1, d)),
            _const_spec((GROUP_LANES, GROUP_LANES)),
        ],
        out_specs=tuple(tok for _ in range(10)),
        compiler_params=_cparams(2),
        name="rwkv_project",
    )(xs, xs, xs, mods, g1n.reshape(1, d),
      jnp.concatenate([mix, jnp.zeros((N_MOD_ROWS - mix.shape[0], d), F32)], axis=0),
      bf(w_rkv[0]), bf(w_rkv[1]), bf(w_rkv[2]), bf(w1), bf(w2), bf(a1), bf(a2), bf(gw1), bf(gw2),
      w0, a0, k_k.reshape(1, d), k_a.reshape(1, d), seg)
    return outs


def _block_diag(x_bf, mask):
    return jnp.where(mask, jnp.concatenate([x_bf] * HEADS_PER_GROUP, axis=0), jnp.zeros((), BF16))


def _diag_blocks(full):
    lane_head = lax.broadcasted_iota(jnp.int32, (RWKV_HEAD, GROUP_LANES), 1) // RWKV_HEAD
    out = full[0:RWKV_HEAD]
    for j in range(1, HEADS_PER_GROUP):
        out = jnp.where(lane_head == j, full[j * RWKV_HEAD:(j + 1) * RWKV_HEAD], out)
    return out


def _wkv_group(lw, kd, bb, kk, v, r, s0, reverse):
    c = WKV_CHUNK
    rows = lax.broadcasted_iota(jnp.int32, (c, GROUP_LANES), 0)
    cols = lax.broadcasted_iota(jnp.int32, (c, GROUP_LANES), 1) % RWKV_HEAD
    if reverse:
        strict, incl = cols > rows, cols >= rows
    else:
        strict, incl = cols < rows, cols <= rows
    eye = (cols == rows).astype(F32)
    bd_r = lax.broadcasted_iota(jnp.int32, (GROUP_LANES, GROUP_LANES), 0) // RWKV_HEAD
    bd_c = lax.broadcasted_iota(jnp.int32, (GROUP_LANES, GROUP_LANES), 1) // RWKV_HEAD
    bmask = bd_r == bd_c
    tri_r = lax.broadcasted_iota(jnp.int32, (c, c), 0)
    tri_c = lax.broadcasted_iota(jnp.int32, (c, c), 1)
    tri = ((tri_c >= tri_r) if reverse else (tri_c <= tri_r)).astype(BF16)

    hi, mid, lo = _split3(lw)
    cs = _dot(tri, hi) + _dot(tri, mid) + _dot(tri, lo)
    total = cs[0:1] if reverse else cs[c - 1:c]
    g_prev = jnp.exp(cs - lw)
    g_inv = jnp.exp(-cs)
    g_all = jnp.exp(cs)
    g_rest = jnp.exp(total - cs)
    g_end = jnp.exp(total)
    a_s = -kk * g_prev
    b_s = bb * g_inv
    k_s = kd * g_inv
    r_s = r * g_all
    b_e = (bb * g_rest).astype(BF16)
    k_e = (kd * g_rest).astype(BF16)

    lhs = jnp.concatenate([a_s, r_s], axis=0).astype(BF16)
    pb = _dot_nt(lhs, _block_diag(b_s.astype(BF16), bmask))
    pk = _dot_nt(lhs, _block_diag(k_s.astype(BF16), bmask))
    l_ab = jnp.where(strict, pb[:c], 0.0)
    m_rb = jnp.where(incl, pb[c:], 0.0)
    l_ak = jnp.where(strict, pk[:c], 0.0)
    m_rk = jnp.where(incl, pk[c:], 0.0)

    p = l_ab
    tmat = eye + p
    p = _dot(p.astype(BF16), _block_diag(p.astype(BF16), bmask))
    for _ in range(4):
        z = _dot(jnp.concatenate([tmat, p], axis=0).astype(BF16), _block_diag(p.astype(BF16), bmask))
        tmat = tmat + z[:c]
        p = z[c:]
    tmat = tmat + _dot(tmat.astype(BF16), _block_diag(p.astype(BF16), bmask))
    t_bf = tmat.astype(BF16)

    bd_v = _block_diag(v.astype(BF16), bmask)
    zv = _dot(jnp.concatenate([l_ak, m_rk], axis=0).astype(BF16), bd_v)
    a_hat = _dot(t_bf, _block_diag(a_s.astype(BF16), bmask))
    u_til = _dot(t_bf, _block_diag(zv[:c].astype(BF16), bmask))
    m_rb_bf = m_rb.astype(BF16)
    r_hat = r_s + _dot(m_rb_bf, _block_diag(a_hat.astype(BF16), bmask))
    y_til = zv[c:] + _dot(m_rb_bf, _block_diag(u_til.astype(BF16), bmask))

    au = jnp.concatenate([a_hat, u_til], axis=1).astype(BF16)
    full = _dot_tn(au, b_e)
    g_mat = _diag_blocks(full[:GROUP_LANES])
    h_mat = _diag_blocks(full[GROUP_LANES:] + _dot_tn(v.astype(BF16), k_e))

    s0_bf = s0.astype(BF16)
    y = y_til + _dot_nt(r_hat.astype(BF16), _block_diag(s0_bf, bmask))
    s1 = s0 * g_end + _dot(s0_bf, _block_diag(g_mat.astype(BF16), bmask)) + h_mat
    return y, s1


def _wkv_kernel(lwf, kdf, bf_, kkf, vf, rf, lwr, kdr, br, kkr, vr, rr, yf_ref, yr_ref, s_ref):
    @pl.when(pl.program_id(1) == 0)
    def _():
        s_ref[...] = jnp.zeros(s_ref.shape, F32)

    n_groups = lwf.shape[2] // GROUP_LANES
    for d, (refs, y_ref) in enumerate((((lwf, kdf, bf_, kkf, vf, rf), yf_ref),
                                       ((lwr, kdr, br, kkr, vr, rr), yr_ref))):
        for j in range(n_groups):
            sl = slice(j * GROUP_LANES, (j + 1) * GROUP_LANES)
            args = [ref[0, :, sl] for ref in refs]
            y, s1 = _wkv_group(*args, s_ref[d, :, sl], reverse=(d == 1))
            y_ref[0, :, sl] = y
            s_ref[d, :, sl] = s1


def _wkv_scan(lw0, lw1, kd0, kd1, b0, b1, kk, v, r, ctx_len):
    b, nt, d = v.shape
    c = WKV_CHUNK
    n_steps = nt // c
    ctx_chunks = ctx_len // c

    def fwd(i, s):
        return (i, s, 0)

    def rev(i, s):
        return (i, jnp.where(s < ctx_chunks, ctx_chunks - 1 - s, n_steps - 1 + ctx_chunks - s), 0)

    blk_f = pl.BlockSpec((1, c, d), fwd)
    blk_r = pl.BlockSpec((1, c, d), rev)
    return pl.pallas_call(
        _wkv_kernel,
        out_shape=(jax.ShapeDtypeStruct((b, nt, d), F32), jax.ShapeDtypeStruct((b, nt, d), F32)),
        grid=(b, n_steps),
        in_specs=[blk_f] * 6 + [blk_r] * 6,
        out_specs=(blk_f, blk_r),
        scratch_shapes=[pltpu.VMEM((2, RWKV_HEAD, d), F32)],
        compiler_params=_cparams(2),
        name="wkv_scan",
    )(lw0, kd0, b0, kk, v, r, lw1, kd1, b1, kk, v, r)


def _readout_kernel(yf_ref, yr_ref, r_ref, v_ref, kd0_ref, kd1_ref, gate_ref, rk_ref, lg_ref, lb_ref,
                    seg_ref, o_ref):
    wkv = yf_ref[0] + yr_ref[0]
    inv_n = 1.0 / RWKV_HEAD
    mu = _group_sum(wkv, seg_ref, exact=True) * inv_n
    dev = wkv - mu
    var = _group_sum(dev * dev, seg_ref, exact=True) * inv_n
    gn = dev * lax.rsqrt(var + GN_EPS) * lg_ref[...] + lb_ref[...]
    rk = r_ref[0] * rk_ref[...]
    bonus = _group_sum(rk * kd0_ref[0] + rk * kd1_ref[0], seg_ref, exact=True) * v_ref[0]
    o_ref[0] = ((gn + bonus) * gate_ref[0]).astype(o_ref.dtype)


def _rwkv_readout(yf, yr, r, v, kd0, kd1, gate, r_k, ln_g, ln_b, seg, ctx_len):
    b, nt, d = v.shape
    tm = ROW_TILE
    off = ctx_len // tm
    n = nt - ctx_len
    tok = pl.BlockSpec((1, tm, d), lambda i, t: (i, t + off, 0))
    return pl.pallas_call(
        _readout_kernel,
        out_shape=jax.ShapeDtypeStruct((b, n, d), BF16),
        grid=(b, n // tm),
        in_specs=[tok] * 7 + [_const_spec((1, d))] * 3 + [_const_spec((GROUP_LANES, GROUP_LANES))],
        out_specs=pl.BlockSpec((1, tm, d), lambda i, t: (i, t, 0)),
        compiler_params=_cparams(2),
        name="rwkv_readout",
    )(yf, yr, r, v, kd0, kd1, gate, r_k.reshape(1, d), ln_g.reshape(1, d), ln_b.reshape(1, d), seg)


def _rope_tables(ctx_len, seq_len):
    rows = seq_len // GRID_W
    row = jnp.repeat(jnp.arange(rows, dtype=F32), GRID_W, total_repeat_length=seq_len)
    col = jnp.tile(jnp.arange(GRID_W, dtype=F32), rows)
    inv_freq = ROPE_THETA ** (-jnp.arange(ROPE_PAIRS, dtype=F32) / ROPE_PAIRS)
    row_ang, col_ang = row[:, None] * inv_freq, col[:, None] * inv_freq
    ang = jnp.concatenate([row_ang, row_ang, col_ang, col_ang], axis=1)
    cos = jnp.concatenate([jnp.ones((ctx_len, HEAD_DIM), F32), jnp.cos(ang)], axis=0)
    sin = jnp.concatenate([jnp.zeros((ctx_len, HEAD_DIM), F32), jnp.sin(ang)], axis=0)
    sign = jnp.asarray(np.concatenate([-np.ones(ROPE_PAIRS), np.ones(ROPE_PAIRS)] * 2), F32)
    reps = V7X_LANES // HEAD_DIM
    return jnp.tile(cos, (1, reps)), jnp.tile(sin * sign, (1, reps))


def kernel(x, c, ctx, c_ctx, mod_w, mod_b, norm1_g, norm2_g, ffn_wg, ffn_wu, ffn_wd, attn_wqkv,
           attn_q_gain, attn_k_gain, attn_wo, rwkv_mix, rwkv_wrkv, rwkv_w0, rwkv_w1, rwkv_w2, rwkv_a0,
           rwkv_a1, rwkv_a2, rwkv_g1, rwkv_g2, rwkv_k_k, rwkv_k_a, rwkv_r_k, rwkv_ln_g, rwkv_ln_b,
           rwkv_wo, final_g):
    b, seq_len, d = x.shape
    ctx_len = ctx.shape[1]
    depth = mod_w.shape[0]
    assert depth == 2 and d == N_HEADS * HEAD_DIM
    assert ctx_len % ROW_TILE == 0 and seq_len % ROW_TILE == 0 and seq_len % GRID_W == 0
    ctx_tiles = ctx_len // ROW_TILE

    n_rows = -(-(b + 1) // 8) * 8
    c_rows = jnp.concatenate([c, c_ctx[None], jnp.zeros((n_rows - b - 1, d), F32)], axis=0)
    m_all = _modulation(c_rows, mod_w, mod_b).reshape(depth, n_rows, 6, d)
    pad = jnp.zeros((depth, b, N_MOD_ROWS - 6, d), F32)
    lat = jnp.concatenate([m_all[:, :b], pad], axis=2)
    con = jnp.concatenate([jnp.broadcast_to(m_all[:, b:b + 1], (depth, b, 6, d)), pad], axis=2)
    mods = jnp.stack([con, lat], axis=2)

    cos2, sin2 = _rope_tables(ctx_len, seq_len)
    seg = jnp.asarray(np.kron(np.eye(HEADS_PER_GROUP), np.ones((HEAD_DIM, HEAD_DIM))), BF16)

    xs = jnp.concatenate([ctx, x], axis=1)

    q, k4, v2 = _qkv_project(xs, mods[0], norm1_g[0], attn_wqkv[0], attn_q_gain[0], attn_k_gain[0],
                             cos2, sin2, seg, ctx_tiles)
    o = _attention(q, k4, v2, ctx_len)
    xs = _out_ffn(o, xs, mods[0], norm2_g[0], attn_wo[0], ffn_wg[0], ffn_wu[0], ffn_wd[0], final_g,
                  x_tile_offset=0, ctx_tiles=ctx_tiles, final_norm=False)

    r, v, kk, gate, lw0, lw1, kd0, kd1, b0, b1 = _rwkv_project(
        xs, mods[1], norm1_g[1], rwkv_mix[0], rwkv_wrkv[0], rwkv_w0[0], rwkv_w1[0], rwkv_w2[0],
        rwkv_a0[0], rwkv_a1[0], rwkv_a2[0], rwkv_g1[0], rwkv_g2[0], rwkv_k_k[0], rwkv_k_a[0], seg,
        ctx_len)
    yf, yr = _wkv_scan(lw0, lw1, kd0, kd1, b0, b1, kk, v, r, ctx_len)
    mix_out = _rwkv_readout(yf, yr, r, v, kd0, kd1, gate, rwkv_r_k[0].reshape(-1), rwkv_ln_g[0],
                            rwkv_ln_b[0], seg, ctx_len)
    return _out_ffn(mix_out, xs, mods[1], norm2_g[1], rwkv_wo[0], ffn_wg[1], ffn_wu[1], ffn_wd[1],
                    final_g, x_tile_offset=ctx_tiles, ctx_tiles=ctx_tiles, final_norm=True)
```

```python
import functools

import jax
import jax.numpy as jnp
import numpy as np
from jax import lax
from jax.experimental import pallas as pl
from jax.experimental.pallas import tpu as pltpu

F32 = jnp.float32
BF16 = jnp.bfloat16

NORM_EPS = 1e-6
GN_EPS = 64e-5
GRID_W = 64
N_HEADS = 16
N_KV_HEADS = 4
KV_GROUP = N_HEADS // N_KV_HEADS
HEAD_DIM = 64
ROPE_THETA = 10000.0
ROPE_PAIRS = HEAD_DIM // 4
RWKV_HEAD = 64
DECAY_SCALE = float(np.exp(-0.5))
LOG2_E = float(np.log2(np.e))

V7X_LANES = 128
V7X_MXU_DIM = 256
V7X_VMEM_LIMIT_BYTES = 60000 * 1024

ROW_TILE = 256
GROUP_LANES = V7X_MXU_DIM
HEADS_PER_GROUP = GROUP_LANES // HEAD_DIM
WKV_CHUNK = 64
N_MOD_ROWS = 8


def _cparams(n_axes):
    return pltpu.CompilerParams(
        dimension_semantics=("arbitrary",) * n_axes,
        vmem_limit_bytes=V7X_VMEM_LIMIT_BYTES,
    )


def _const_spec(shape):
    nd = len(shape)
    return pl.BlockSpec(shape, lambda *_: (0,) * nd, pipeline_mode=pl.Buffered(1))


def _dot(a, b):
    return jnp.dot(a, b, preferred_element_type=F32)


def _dot_nt(a, b):
    return lax.dot_general(a, b, (((1,), (1,)), ((), ())), preferred_element_type=F32)


def _dot_tn(a, b):
    return lax.dot_general(a, b, (((0,), (0,)), ((), ())), preferred_element_type=F32)


def _norm_mod(x, g, shift, scale):
    ms = jnp.mean(x * x, axis=-1, keepdims=True)
    return (x * lax.rsqrt(ms + NORM_EPS) * g) * (1.0 + scale) + shift


def _split3(x):
    hi = x.astype(BF16)
    r1 = x - hi.astype(F32)
    mid = r1.astype(BF16)
    lo = (r1 - mid.astype(F32)).astype(BF16)
    return hi, mid, lo


def _group_sum(x, seg_ref, exact=False):
    seg = seg_ref[...]
    outs = []
    for j in range(x.shape[1] // GROUP_LANES):
        xs = x[:, j * GROUP_LANES:(j + 1) * GROUP_LANES]
        if exact:
            hi, mid, lo = _split3(xs)
            outs.append(_dot(hi, seg) + _dot(mid, seg) + _dot(lo, seg))
        else:
            outs.append(_dot(xs.astype(BF16), seg))
    return outs[0] if len(outs) == 1 else jnp.concatenate(outs, axis=1)


def _mod_kernel(c_ref, w_ref, b_ref, o_ref):
    c = c_ref[...]
    s = c * jax.nn.sigmoid(c)
    o_ref[0] = _dot(s.astype(BF16), w_ref[0]) + b_ref[0]


def _modulation(c_rows, mod_w, mod_b):
    depth, d, n = mod_w.shape
    rows = c_rows.shape[0]
    tn = n // 4
    return pl.pallas_call(
        _mod_kernel,
        out_shape=jax.ShapeDtypeStruct((depth, rows, n), F32),
        grid=(depth, n // tn),
        in_specs=[
            pl.BlockSpec((rows, d), lambda i, j: (0, 0)),
            pl.BlockSpec((1, d, tn), lambda i, j: (i, 0, j)),
            pl.BlockSpec((1, 1, tn), lambda i, j: (i, 0, j)),
        ],
        out_specs=pl.BlockSpec((1, rows, tn), lambda i, j: (i, 0, j)),
        compiler_params=_cparams(2),
        name="modulation",
    )(c_rows, mod_w.astype(BF16), mod_b.reshape(depth, 1, n))


def _rope(x, cos, sin_signed):
    w = x.shape[1]
    lane = lax.broadcasted_iota(jnp.int32, x.shape, 1)
    first_half = (lane % (2 * ROPE_PAIRS)) < ROPE_PAIRS
    partner = jnp.where(first_half, pltpu.roll(x, w - ROPE_PAIRS, 1), pltpu.roll(x, ROPE_PAIRS, 1))
    return x * cos + partner * sin_signed


def _qkv_kernel(x_ref, mod_ref, g_ref, w_ref, qg_ref, kg_ref, cos_ref, sin_ref, seg_ref,
                ek_ref, ev_ref, q_ref, k_ref, v_ref, *, nq, nk):
    mod = mod_ref[0, 0]
    h = _norm_mod(x_ref[0], g_ref[...], mod[0:1], mod[1:2]).astype(BF16)
    qkv = _dot(h, w_ref[...])
    q, k, v = qkv[:, :nq], qkv[:, nq:nq + nk], qkv[:, nq + nk:]
    cos2, sin2 = cos_ref[...], sin_ref[...]

    def head_norm_rope(z, gain):
        reps = z.shape[1] // V7X_LANES
        cos = jnp.concatenate([cos2] * reps, axis=1)
        sin = jnp.concatenate([sin2] * reps, axis=1)
        ss = _group_sum(z * z, seg_ref)
        zn = z * lax.rsqrt(ss * (1.0 / HEAD_DIM) + NORM_EPS) * gain
        return _rope(zn, cos, sin)

    qn = head_norm_rope(q, qg_ref[...]) * (HEAD_DIM ** -0.5 * LOG2_E)
    q_ref[0] = qn.astype(BF16)
    kn = head_norm_rope(k, kg_ref[...]).astype(BF16)
    k_t = _dot(kn, ek_ref[...]).astype(BF16)
    v_bf = v.astype(BF16)
    ones = jnp.ones((V7X_LANES - HEAD_DIM, v_bf.shape[0]), BF16)
    for j in range(N_KV_HEADS):
        k_ref[0, j] = k_t[:, j * V7X_LANES:(j + 1) * V7X_LANES]
        v_ref[0, j, 0:HEAD_DIM, :] = _dot_nt(ev_ref[j], v_bf).astype(BF16)
        v_ref[0, j, HEAD_DIM:, :] = ones


def _head_select(n_heads, rows):
    sel = np.zeros((n_heads, rows, n_heads * HEAD_DIM), np.float32)
    for j in range(n_heads):
        sel[j, np.arange(HEAD_DIM), j * HEAD_DIM + np.arange(HEAD_DIM)] = 1.0
    return sel


def _qkv_project(xs, mods, g1, wqkv, q_gain, k_gain, cos2, sin2, seg, ctx_tiles):
    b, nt, d = xs.shape
    nq, nk = N_HEADS * HEAD_DIM, N_KV_HEADS * HEAD_DIM
    tm = ROW_TILE
    ek = np.zeros((nk, N_KV_HEADS * V7X_LANES), np.float32)
    for j in range(N_KV_HEADS):
        ek[j * HEAD_DIM + np.arange(HEAD_DIM), j * V7X_LANES + np.arange(HEAD_DIM)] = 1.0
    ev = _head_select(N_KV_HEADS, HEAD_DIM)
    kern = functools.partial(_qkv_kernel, nq=nq, nk=nk)
    return pl.pallas_call(
        kern,
        out_shape=(
            jax.ShapeDtypeStruct((b, nt, nq), BF16),
            jax.ShapeDtypeStruct((b, N_KV_HEADS, nt, V7X_LANES), BF16),
            jax.ShapeDtypeStruct((b, N_KV_HEADS, V7X_LANES, nt), BF16),
        ),
        grid=(b, nt // tm),
        in_specs=[
            pl.BlockSpec((1, tm, d), lambda i, t: (i, t, 0)),
            pl.BlockSpec((1, 1, N_MOD_ROWS, d), lambda i, t: (i, jnp.where(t < ctx_tiles, 0, 1), 0, 0)),
            _const_spec((1, d)),
            _const_spec((d, nq + 2 * nk)),
            _const_spec((1, nq)),
            _const_spec((1, nk)),
            pl.BlockSpec((tm, V7X_LANES), lambda i, t: (t, 0)),
            pl.BlockSpec((tm, V7X_LANES), lambda i, t: (t, 0)),
            _const_spec((GROUP_LANES, GROUP_LANES)),
            _const_spec(ek.shape),
            _const_spec(ev.shape),
        ],
        out_specs=(
            pl.BlockSpec((1, tm, nq), lambda i, t: (i, t, 0)),
            pl.BlockSpec((1, N_KV_HEADS, tm, V7X_LANES), lambda i, t: (i, 0, t, 0)),
            pl.BlockSpec((1, N_KV_HEADS, V7X_LANES, tm), lambda i, t: (i, 0, 0, t)),
        ),
        compiler_params=_cparams(2),
        name="qkv_project",
    )(xs, mods, g1.reshape(1, d), wqkv.astype(BF16),
      jnp.tile(q_gain, N_HEADS).reshape(1, nq), jnp.tile(k_gain, N_KV_HEADS).reshape(1, nk),
      cos2, sin2, seg, jnp.asarray(ek, BF16), jnp.asarray(ev, BF16))


def _attn_kernel(q_ref, k_ref, vt_ref, selq_ref, place_ref, o_ref, qt_ref, m_ref, acc_ref,
                 sa_ref, sb_ref, *, tq, tk, ctx_tiles, ctx_chunks, all_chunks, unroll):
    qi = pl.program_id(2)
    q = q_ref[0]
    for g in range(KV_GROUP):
        qt_ref[:, g * tq:(g + 1) * tq] = _dot_nt(selq_ref[g], q).astype(BF16)
    m_ref[...] = jnp.full(m_ref.shape, -jnp.inf, F32)
    acc_ref[...] = jnp.zeros(acc_ref.shape, F32)
    n_chunks = jnp.where(qi < ctx_tiles, ctx_chunks, all_chunks)

    def scores(c):
        start = pl.multiple_of(c * tk, tk)
        return _dot(k_ref[0, 0, pl.ds(start, tk), :], qt_ref[...])

    def update(c, s_ref):
        start = pl.multiple_of(c * tk, tk)
        vtc = vt_ref[0, 0, :, pl.ds(start, tk)]
        st = s_ref[...]
        m_prev = m_ref[...]
        m_new = jnp.maximum(m_prev, jnp.max(st, axis=0, keepdims=True))
        alpha = jnp.exp2(m_prev - m_new)
        pt = jnp.exp2(st - m_new).astype(BF16)
        acc_ref[...] = alpha * acc_ref[...] + _dot(vtc, pt)
        m_ref[...] = m_new

    slots = (sa_ref, sb_ref)
    sa_ref[...] = scores(0)

    def body(j, carry):
        c0 = unroll * j
        for u in range(unroll):
            slots[(u + 1) % 2][...] = scores(c0 + u + 1)
            update(c0 + u, slots[u % 2])
        return carry

    lax.fori_loop(0, (n_chunks - 1) // unroll, body, 0)
    update(n_chunks - 1, sa_ref)
    acc = acc_ref[...]
    ot = (acc[0:HEAD_DIM] / acc[HEAD_DIM:HEAD_DIM + 1]).astype(BF16)
    out = _dot_tn(ot[:, 0:tq], place_ref[0])
    for g in range(1, KV_GROUP):
        out = out + _dot_tn(ot[:, g * tq:(g + 1) * tq], place_ref[g])
    o_ref[0] = out.astype(o_ref.dtype)


def _attention(q, k, vt, ctx_len):
    b, nt, nq = q.shape
    tq = ROW_TILE
    tk = ROW_TILE
    selq = _head_select(KV_GROUP, V7X_LANES)
    place = _head_select(KV_GROUP, HEAD_DIM)
    ctx_chunks, all_chunks = ctx_len // tk, nt // tk
    unroll = max(u for u in (2, 4) if (ctx_chunks - 1) % u == 0 and (all_chunks - 1) % u == 0)
    kern = functools.partial(
        _attn_kernel, tq=tq, tk=tk, ctx_tiles=ctx_len // tq,
        ctx_chunks=ctx_chunks, all_chunks=all_chunks, unroll=unroll)
    return pl.pallas_call(
        kern,
        out_shape=jax.ShapeDtypeStruct((b, nt, nq), BF16),
        grid=(b, N_KV_HEADS, nt // tq),
        in_specs=[
            pl.BlockSpec((1, tq, GROUP_LANES), lambda i, j, t: (i, t, j)),
            pl.BlockSpec((1, 1, nt, V7X_LANES), lambda i, j, t: (i, j, 0, 0)),
            pl.BlockSpec((1, 1, V7X_LANES, nt), lambda i, j, t: (i, j, 0, 0)),
            _const_spec(selq.shape),
            _const_spec(place.shape),
        ],
        out_specs=pl.BlockSpec((1, tq, GROUP_LANES), lambda i, j, t: (i, t, j)),
        scratch_shapes=[
            pltpu.VMEM((V7X_LANES, KV_GROUP * tq), BF16),
            pltpu.VMEM((1, KV_GROUP * tq), F32),
            pltpu.VMEM((V7X_LANES, KV_GROUP * tq), F32),
            pltpu.VMEM((tk, KV_GROUP * tq), F32),
            pltpu.VMEM((tk, KV_GROUP * tq), F32),
        ],
        compiler_params=_cparams(3),
        name="flash_attention",
    )(q, k, vt, jnp.asarray(selq, BF16), jnp.asarray(place, BF16))


def _out_ffn_kernel(y_ref, x_ref, mod_ref, g_ref, wo_ref, wg_ref, wu_ref, wd_ref, fg_ref, o_ref, *,
                    final_norm):
    mod = mod_ref[0, 0]
    x1 = x_ref[0] + mod[2:3] * _dot(y_ref[0], wo_ref[...])
    h2 = _norm_mod(x1, g_ref[...], mod[3:4], mod[4:5]).astype(BF16)
    a = _dot(h2, wg_ref[...])
    u = _dot(h2, wu_ref[...])
    hid = (a * jax.nn.sigmoid(a) * u).astype(BF16)
    x2 = x1 + mod[5:6] * _dot(hid, wd_ref[...])
    if final_norm:
        ms = jnp.mean(x2 * x2, axis=-1, keepdims=True)
        x2 = x2 * lax.rsqrt(ms + NORM_EPS) * fg_ref[...]
    o_ref[0] = x2


def _out_ffn(y, xs, mods, g2, wo, wg, wu, wd, final_g, *, x_tile_offset, ctx_tiles, final_norm):
    b, n, d = y.shape
    f = wg.shape[1]
    tm = ROW_TILE
    kern = functools.partial(_out_ffn_kernel, final_norm=final_norm)

    def mod_sel(i, t):
        return (i, jnp.where(t + x_tile_offset < ctx_tiles, 0, 1), 0, 0)

    return pl.pallas_call(
        kern,
        out_shape=jax.ShapeDtypeStruct((b, n, d), F32),
        grid=(b, n // tm),
        in_specs=[
            pl.BlockSpec((1, tm, d), lambda i, t: (i, t, 0)),
            pl.BlockSpec((1, tm, d), lambda i, t: (i, t + x_tile_offset, 0)),
            pl.BlockSpec((1, 1, N_MOD_ROWS, d), mod_sel),
            _const_spec((1, d)),
            _const_spec((d, d)),
            _const_spec((d, f)),
            _const_spec((d, f)),
            _const_spec((f, d)),
            _const_spec((1, d)),
        ],
        out_specs=pl.BlockSpec((1, tm, d), lambda i, t: (i, t, 0)),
        compiler_params=_cparams(2),
        name="out_ffn_final" if final_norm else "out_ffn",
    )(y, xs, mods, g2.reshape(1, d), wo.astype(BF16), wg.astype(BF16), wu.astype(BF16),
      wd.astype(BF16), final_g.reshape(1, d))


def _rwkv_proj_kernel(x_ref, xp_ref, xn_ref, mod_ref, g_ref, mix_ref, wr_ref, wk_ref, wv_ref,
                      w1_ref, w2_ref, a1_ref, a2_ref, g1_ref, g2_ref, w0_ref, a0_ref, kk_ref, ka_ref,
                      seg_ref,
                      r_ref, v_ref, kkn_ref, gate_ref, lw0_ref, lw1_ref, kd0_ref, kd1_ref,
                      b0_ref, b1_ref, *, tm, seq_starts, seq_ends):
    t = pl.program_id(1)
    mod = mod_ref[0, 0]
    g = g_ref[...]
    h = _norm_mod(x_ref[0], g, mod[0:1], mod[1:2])
    h_prev = _norm_mod(xp_ref[0], g, mod[0:1], mod[1:2])[7:8]
    h_next = _norm_mod(xn_ref[0], g, mod[0:1], mod[1:2])[0:1]
    row = lax.broadcasted_iota(jnp.int32, h.shape, 0)
    pos = row + t * tm
    at_start = functools.reduce(jnp.logical_or, [pos == s for s in seq_starts])
    at_end = functools.reduce(jnp.logical_or, [pos == e for e in seq_ends])
    before = jnp.where(row == 0, h_prev, pltpu.roll(h, 1, 0))
    after = jnp.where(row == tm - 1, h_next, pltpu.roll(h, tm - 1, 0))
    before = jnp.where(at_start, 0.0, before)
    after = jnp.where(at_end, 0.0, after)
    xx = 0.5 * (before + after) - h
    mix = mix_ref[...]

    def lerp(j):
        return (h + xx * mix[j:j + 1]).astype(BF16)

    r_ref[0] = _dot(lerp(0), wr_ref[...])
    k = _dot(lerp(2), wk_ref[...])
    v_ref[0] = _dot(lerp(3), wv_ref[...])
    kkr = k * kk_ref[...]
    ss = _group_sum(kkr * kkr, seg_ref)
    kkn = kkr * lax.rsqrt(jnp.maximum(ss, 1e-24))
    kkn_ref[0] = kkn
    xw, xa = lerp(1), lerp(4)
    ka = ka_ref[...]
    for d, (lw_ref, kd_ref, b_ref) in enumerate(((lw0_ref, kd0_ref, b0_ref), (lw1_ref, kd1_ref, b1_ref))):
        z = w0_ref[d:d + 1] + _dot(jnp.tanh(_dot(xw, w1_ref[d])).astype(BF16), w2_ref[d])
        lw_ref[0] = -DECAY_SCALE * jax.nn.sigmoid(z)
        a = jax.nn.sigmoid(a0_ref[d:d + 1] + _dot(_dot(xa, a1_ref[d]).astype(BF16), a2_ref[d]))
        kd_ref[0] = k * (1.0 + (a - 1.0) * ka)
        b_ref[0] = kkn * a
    gate_ref[0] = _dot(jax.nn.sigmoid(_dot(lerp(5), g1_ref[...])).astype(BF16), g2_ref[...])


def _rwkv_project(xs, mods, g1n, mix, w_rkv, w0, w1, w2, a0, a1, a2, gw1, gw2, k_k, k_a, seg,
                  ctx_len):
    b, nt, d = xs.shape
    tm = ROW_TILE
    halo = 8
    per = tm // halo
    n_halo = nt // halo
    kern = functools.partial(_rwkv_proj_kernel, tm=tm, seq_starts=(0, ctx_len),
                             seq_ends=(ctx_len - 1, nt - 1))
    tok = pl.BlockSpec((1, tm, d), lambda i, t: (i, t, 0))
    bf = lambda w: w.astype(BF16)
    outs = pl.pallas_call(
        kern,
        out_shape=tuple(jax.ShapeDtypeStruct((b, nt, d), F32) for _ in range(10)),
        grid=(b, nt // tm),
        in_specs=[
            tok,
            pl.BlockSpec((1, halo, d), lambda i, t: (i, jnp.maximum(t * per - 1, 0), 0)),
            pl.BlockSpec((1, halo, d), lambda i, t: (i, jnp.minimum((t + 1) * per, n_halo - 1), 0)),
            pl.BlockSpec((1, 1, N_MOD_ROWS, d), lambda i, t: (i, jnp.where(t * tm < ctx_len, 0, 1), 0, 0)),
            _const_spec((1, d)),
            _const_spec((N_MOD_ROWS, d)),
            _const_spec((d, d)), _const_spec((d, d)), _const_spec((d, d)),
            _const_spec(w1.shape), _const_spec(w2.shape), _const_spec(a1.shape), _const_spec(a2.shape),
            _const_spec(gw1.shape), _const_spec(gw2.shape),
            _const_spec((2, d)), _const_spec((2, d)), _const_spec((1, d)), _const_spec((1, d)),
            _const_spec((GROUP_LANES, GROUP_LANES)),
        ],
        out_specs=tuple(tok for _ in range(10)),
        compiler_params=_cparams(2),
        name="rwkv_project",
    )(xs, xs, xs, mods, g1n.reshape(1, d),
      jnp.concatenate([mix, jnp.zeros((N_MOD_ROWS - mix.shape[0], d), F32)], axis=0),
      bf(w_rkv[0]), bf(w_rkv[1]), bf(w_rkv[2]), bf(w1), bf(w2), bf(a1), bf(a2), bf(gw1), bf(gw2),
      w0, a0, k_k.reshape(1, d), k_a.reshape(1, d), seg)
    return outs


def _block_diag(x_bf, mask):
    return jnp.where(mask, jnp.concatenate([x_bf] * HEADS_PER_GROUP, axis=0), jnp.zeros((), BF16))


def _diag_blocks(full):
    lane_head = lax.broadcasted_iota(jnp.int32, (RWKV_HEAD, GROUP_LANES), 1) // RWKV_HEAD
    out = full[0:RWKV_HEAD]
    for j in range(1, HEADS_PER_GROUP):
        out = jnp.where(lane_head == j, full[j * RWKV_HEAD:(j + 1) * RWKV_HEAD], out)
    return out


def _wkv_masks(reverse):
    c = WKV_CHUNK
    rows = lax.broadcasted_iota(jnp.int32, (c, GROUP_LANES), 0)
    cols = lax.broadcasted_iota(jnp.int32, (c, GROUP_LANES), 1) % RWKV_HEAD
    tri_r = lax.broadcasted_iota(jnp.int32, (c, c), 0)
    tri_c = lax.broadcasted_iota(jnp.int32, (c, c), 1)
    if reverse:
        return cols > rows, cols >= rows, (tri_c >= tri_r).astype(BF16)
    return cols < rows, cols <= rows, (tri_c <= tri_r).astype(BF16)


def _wkv_chains(chains):
    c = WKV_CHUNK
    n = len(chains)
    rng = range(n)
    rows = lax.broadcasted_iota(jnp.int32, (c, GROUP_LANES), 0)
    cols = lax.broadcasted_iota(jnp.int32, (c, GROUP_LANES), 1) % RWKV_HEAD
    eye = (cols == rows).astype(F32)
    bd_r = lax.broadcasted_iota(jnp.int32, (GROUP_LANES, GROUP_LANES), 0) // RWKV_HEAD
    bd_c = lax.broadcasted_iota(jnp.int32, (GROUP_LANES, GROUP_LANES), 1) // RWKV_HEAD
    bmask = bd_r == bd_c
    masks = {rev: _wkv_masks(rev) for rev in sorted({ch[7] for ch in chains})}
    strict = [masks[ch[7]][0] for ch in chains]
    incl = [masks[ch[7]][1] for ch in chains]
    tri = [masks[ch[7]][2] for ch in chains]
    lw, kd, bb, kk, v, r, s0 = ([ch[i] for ch in chains] for i in range(7))
    bd = lambda x: _block_diag(x.astype(BF16), bmask)

    split = [_split3(lw[i]) for i in rng]
    cs = [_dot(tri[i], split[i][0]) + _dot(tri[i], split[i][1]) + _dot(tri[i], split[i][2]) for i in rng]
    total = [cs[i][0:1] if chains[i][7] else cs[i][c - 1:c] for i in rng]
    a_s = [-kk[i] * jnp.exp(cs[i] - lw[i]) for i in rng]
    g_inv = [jnp.exp(-cs[i]) for i in rng]
    b_s = [bb[i] * g_inv[i] for i in rng]
    k_s = [kd[i] * g_inv[i] for i in rng]
    r_s = [r[i] * jnp.exp(cs[i]) for i in rng]
    g_rest = [jnp.exp(total[i] - cs[i]) for i in rng]
    b_e = [(bb[i] * g_rest[i]).astype(BF16) for i in rng]
    k_e = [(kd[i] * g_rest[i]).astype(BF16) for i in rng]
    g_end = [jnp.exp(total[i]) for i in rng]

    lhs = [jnp.concatenate([a_s[i], r_s[i]], axis=0).astype(BF16) for i in rng]
    pb = [_dot_nt(lhs[i], bd(b_s[i])) for i in rng]
    pk = [_dot_nt(lhs[i], bd(k_s[i])) for i in rng]
    l_ab = [jnp.where(strict[i], pb[i][:c], 0.0) for i in rng]
    m_rb = [jnp.where(incl[i], pb[i][c:], 0.0).astype(BF16) for i in rng]
    l_ak = [jnp.where(strict[i], pk[i][:c], 0.0) for i in rng]
    m_rk = [jnp.where(incl[i], pk[i][c:], 0.0) for i in rng]

    p = l_ab
    tmat = [eye + p[i] for i in rng]
    p = [_dot(p[i].astype(BF16), bd(p[i])) for i in rng]
    for _ in range(4):
        z = [_dot(jnp.concatenate([tmat[i], p[i]], axis=0).astype(BF16), bd(p[i])) for i in rng]
        tmat = [tmat[i] + z[i][:c] for i in rng]
        p = [z[i][c:] for i in rng]
    tmat = [tmat[i] + _dot(tmat[i].astype(BF16), bd(p[i])) for i in rng]
    t_bf = [tmat[i].astype(BF16) for i in rng]

    zv = [_dot(jnp.concatenate([l_ak[i], m_rk[i]], axis=0).astype(BF16), bd(v[i])) for i in rng]
    a_hat = [_dot(t_bf[i], bd(a_s[i])) for i in rng]
    u_til = [_dot(t_bf[i], bd(zv[i][:c])) for i in rng]
    r_hat = [r_s[i] + _dot(m_rb[i], bd(a_hat[i])) for i in rng]
    y_til = [zv[i][c:] + _dot(m_rb[i], bd(u_til[i])) for i in rng]

    full = [_dot_tn(jnp.concatenate([a_hat[i], u_til[i]], axis=1).astype(BF16), b_e[i]) for i in rng]
    g_mat = [_diag_blocks(full[i][:GROUP_LANES]) for i in rng]
    h_mat = [_diag_blocks(full[i][GROUP_LANES:] + _dot_tn(v[i].astype(BF16), k_e[i])) for i in rng]

    y = [y_til[i] + _dot_nt(r_hat[i].astype(BF16), bd(s0[i])) for i in rng]
    s1 = [s0[i] * g_end[i] + _dot(s0[i].astype(BF16), bd(g_mat[i])) + h_mat[i] for i in rng]
    return y, s1


def _wkv_kernel(lwf, kdf, bf_, kkf, vf, rf, lwr, kdr, br, kkr, vr, rr, yf_ref, yr_ref, s_ref):
    @pl.when(pl.program_id(1) == 0)
    def _():
        s_ref[...] = jnp.zeros(s_ref.shape, F32)

    n_groups = lwf.shape[2] // GROUP_LANES
    chains, dests = [], []
    for j in range(n_groups):
        sl = slice(j * GROUP_LANES, (j + 1) * GROUP_LANES)
        for d, (refs, y_ref) in enumerate((((lwf, kdf, bf_, kkf, vf, rf), yf_ref),
                                           ((lwr, kdr, br, kkr, vr, rr), yr_ref))):
            chains.append(tuple(ref[0, :, sl] for ref in refs) + (s_ref[d, :, sl], d == 1))
            dests.append((y_ref, d, sl))
    ys, s1s = _wkv_chains(chains)
    for (y_ref, d, sl), y, s1 in zip(dests, ys, s1s):
        y_ref[0, :, sl] = y
        s_ref[d, :, sl] = s1


def _wkv_scan(lw0, lw1, kd0, kd1, b0, b1, kk, v, r, ctx_len):
    b, nt, d = v.shape
    c = WKV_CHUNK
    n_steps = nt // c
    ctx_chunks = ctx_len // c

    def fwd(i, s):
        return (i, s, 0)

    def rev(i, s):
        return (i, jnp.where(s < ctx_chunks, ctx_chunks - 1 - s, n_steps - 1 + ctx_chunks - s), 0)

    blk_f = pl.BlockSpec((1, c, d), fwd)
    blk_r = pl.BlockSpec((1, c, d), rev)
    return pl.pallas_call(
        _wkv_kernel,
        out_shape=(jax.ShapeDtypeStruct((b, nt, d), F32), jax.ShapeDtypeStruct((b, nt, d), F32)),
        grid=(b, n_steps),
        in_specs=[blk_f] * 6 + [blk_r] * 6,
        out_specs=(blk_f, blk_r),
        scratch_shapes=[pltpu.VMEM((2, RWKV_HEAD, d), F32)],
        compiler_params=_cparams(2),
        name="wkv_scan",
    )(lw0, kd0, b0, kk, v, r, lw1, kd1, b1, kk, v, r)


def _readout_kernel(yf_ref, yr_ref, r_ref, v_ref, kd0_ref, kd1_ref, gate_ref, rk_ref, lg_ref, lb_ref,
                    seg_ref, o_ref):
    wkv = yf_ref[0] + yr_ref[0]
    inv_n = 1.0 / RWKV_HEAD
    mu = _group_sum(wkv, seg_ref, exact=True) * inv_n
    dev = wkv - mu
    var = _group_sum(dev * dev, seg_ref, exact=True) * inv_n
    gn = dev * lax.rsqrt(var + GN_EPS) * lg_ref[...] + lb_ref[...]
    rk = r_ref[0] * rk_ref[...]
    bonus = _group_sum(rk * kd0_ref[0] + rk * kd1_ref[0], seg_ref, exact=True) * v_ref[0]
    o_ref[0] = ((gn + bonus) * gate_ref[0]).astype(o_ref.dtype)


def _rwkv_readout(yf, yr, r, v, kd0, kd1, gate, r_k, ln_g, ln_b, seg, ctx_len):
    b, nt, d = v.shape
    tm = ROW_TILE
    off = ctx_len // tm
    n = nt - ctx_len
    tok = pl.BlockSpec((1, tm, d), lambda i, t: (i, t + off, 0))
    return pl.pallas_call(
        _readout_kernel,
        out_shape=jax.ShapeDtypeStruct((b, n, d), BF16),
        grid=(b, n // tm),
        in_specs=[tok] * 7 + [_const_spec((1, d))] * 3 + [_const_spec((GROUP_LANES, GROUP_LANES))],
        out_specs=pl.BlockSpec((1, tm, d), lambda i, t: (i, t, 0)),
        compiler_params=_cparams(2),
        name="rwkv_readout",
    )(yf, yr, r, v, kd0, kd1, gate, r_k.reshape(1, d), ln_g.reshape(1, d), ln_b.reshape(1, d), seg)


def _rope_tables(ctx_len, seq_len):
    rows = seq_len // GRID_W
    row = jnp.repeat(jnp.arange(rows, dtype=F32), GRID_W, total_repeat_length=seq_len)
    col = jnp.tile(jnp.arange(GRID_W, dtype=F32), rows)
    inv_freq = ROPE_THETA ** (-jnp.arange(ROPE_PAIRS, dtype=F32) / ROPE_PAIRS)
    row_ang, col_ang = row[:, None] * inv_freq, col[:, None] * inv_freq
    ang = jnp.concatenate([row_ang, row_ang, col_ang, col_ang], axis=1)
    cos = jnp.concatenate([jnp.ones((ctx_len, HEAD_DIM), F32), jnp.cos(ang)], axis=0)
    sin = jnp.concatenate([jnp.zeros((ctx_len, HEAD_DIM), F32), jnp.sin(ang)], axis=0)
    sign = jnp.asarray(np.concatenate([-np.ones(ROPE_PAIRS), np.ones(ROPE_PAIRS)] * 2), F32)
    reps = V7X_LANES // HEAD_DIM
    return jnp.tile(cos, (1, reps)), jnp.tile(sin * sign, (1, reps))


def kernel(x, c, ctx, c_ctx, mod_w, mod_b, norm1_g, norm2_g, ffn_wg, ffn_wu, ffn_wd, attn_wqkv,
           attn_q_gain, attn_k_gain, attn_wo, rwkv_mix, rwkv_wrkv, rwkv_w0, rwkv_w1, rwkv_w2, rwkv_a0,
           rwkv_a1, rwkv_a2, rwkv_g1, rwkv_g2, rwkv_k_k, rwkv_k_a, rwkv_r_k, rwkv_ln_g, rwkv_ln_b,
           rwkv_wo, final_g):
    b, seq_len, d = x.shape
    ctx_len = ctx.shape[1]
    depth = mod_w.shape[0]
    assert depth == 2 and d == N_HEADS * HEAD_DIM
    assert ctx_len % ROW_TILE == 0 and seq_len % ROW_TILE == 0 and seq_len % GRID_W == 0
    ctx_tiles = ctx_len // ROW_TILE

    n_rows = -(-(b + 1) // 8) * 8
    c_rows = jnp.concatenate([c, c_ctx[None], jnp.zeros((n_rows - b - 1, d), F32)], axis=0)
    m_all = _modulation(c_rows, mod_w, mod_b).reshape(depth, n_rows, 6, d)
    pad = jnp.zeros((depth, b, N_MOD_ROWS - 6, d), F32)
    lat = jnp.concatenate([m_all[:, :b], pad], axis=2)
    con = jnp.concatenate([jnp.broadcast_to(m_all[:, b:b + 1], (depth, b, 6, d)), pad], axis=2)
    mods = jnp.stack([con, lat], axis=2)

    cos2, sin2 = _rope_tables(ctx_len, seq_len)
    seg = jnp.asarray(np.kron(np.eye(HEADS_PER_GROUP), np.ones((HEAD_DIM, HEAD_DIM))), BF16)

    xs = jnp.concatenate([ctx, x], axis=1)

    q, k, vt = _qkv_project(xs, mods[0], norm1_g[0], attn_wqkv[0], attn_q_gain[0], attn_k_gain[0],
                             cos2, sin2, seg, ctx_tiles)
    o = _attention(q, k, vt, ctx_len)
    xs = _out_ffn(o, xs, mods[0], norm2_g[0], attn_wo[0], ffn_wg[0], ffn_wu[0], ffn_wd[0], final_g,
                  x_tile_offset=0, ctx_tiles=ctx_tiles, final_norm=False)

    r, v, kk, gate, lw0, lw1, kd0, kd1, b0, b1 = _rwkv_project(
        xs, mods[1], norm1_g[1], rwkv_mix[0], rwkv_wrkv[0], rwkv_w0[0], rwkv_w1[0], rwkv_w2[0],
        rwkv_a0[0], rwkv_a1[0], rwkv_a2[0], rwkv_g1[0], rwkv_g2[0], rwkv_k_k[0], rwkv_k_a[0], seg,
        ctx_len)
    yf, yr = _wkv_scan(lw0, lw1, kd0, kd1, b0, b1, kk, v, r, ctx_len)
    mix_out = _rwkv_readout(yf, yr, r, v, kd0, kd1, gate, rwkv_r_k[0].reshape(-1), rwkv_ln_g[0],
                            rwkv_ln_b[0], seg, ctx_len)
    return _out_ffn(mix_out, xs, mods[1], norm2_g[1], rwkv_wo[0], ffn_wg[1], ffn_wu[1], ffn_wd[1],
                    final_g, x_tile_offset=ctx_tiles, ctx_tiles=ctx_tiles, final_norm=True)
```

```python
import functools

import jax
import jax.numpy as jnp
import numpy as np
from jax import lax
from jax.experimental import pallas as pl
from jax.experimental.pallas import tpu as pltpu

F32 = jnp.float32
BF16 = jnp.bfloat16

NORM_EPS = 1e-6
GN_EPS = 64e-5
GRID_W = 64
N_HEADS = 16
N_KV_HEADS = 4
KV_GROUP = N_HEADS // N_KV_HEADS
HEAD_DIM = 64
ROPE_THETA = 10000.0
ROPE_PAIRS = HEAD_DIM // 4
RWKV_HEAD = 64
DECAY_SCALE = float(np.exp(-0.5))
LOG2_E = float(np.log2(np.e))
EXP2_SAFE_RANGE = 120.0

V7X_LANES = 128
V7X_MXU_DIM = 256
V7X_VMEM_LIMIT_BYTES = 60000 * 1024

ROW_TILE = 256
GROUP_LANES = V7X_MXU_DIM
HEADS_PER_GROUP = GROUP_LANES // HEAD_DIM
BF16_SUBLANES = 16
ATTN_V_ROWS = HEAD_DIM + BF16_SUBLANES
ATTN_CHUNKS_PER_PV = 8
WKV_CHUNK = 64
N_MOD_ROWS = 8


def _cparams(n_axes):
    return pltpu.CompilerParams(
        dimension_semantics=("arbitrary",) * n_axes,
        vmem_limit_bytes=V7X_VMEM_LIMIT_BYTES,
    )


def _const_spec(shape):
    nd = len(shape)
    return pl.BlockSpec(shape, lambda *_: (0,) * nd, pipeline_mode=pl.Buffered(1))


def _dot(a, b):
    return jnp.dot(a, b, preferred_element_type=F32)


def _dot_nt(a, b):
    return lax.dot_general(a, b, (((1,), (1,)), ((), ())), preferred_element_type=F32)


def _dot_tn(a, b):
    return lax.dot_general(a, b, (((0,), (0,)), ((), ())), preferred_element_type=F32)


def _norm_mod(x, g, shift, scale):
    ms = jnp.mean(x * x, axis=-1, keepdims=True)
    return (x * lax.rsqrt(ms + NORM_EPS) * g) * (1.0 + scale) + shift


def _split3(x):
    hi = x.astype(BF16)
    r1 = x - hi.astype(F32)
    mid = r1.astype(BF16)
    lo = (r1 - mid.astype(F32)).astype(BF16)
    return hi, mid, lo


def _group_sum(x, seg_ref, exact=False):
    seg = seg_ref[...]
    outs = []
    for j in range(x.shape[1] // GROUP_LANES):
        xs = x[:, j * GROUP_LANES:(j + 1) * GROUP_LANES]
        if exact:
            hi, mid, lo = _split3(xs)
            outs.append(_dot(hi, seg) + _dot(mid, seg) + _dot(lo, seg))
        else:
            outs.append(_dot(xs.astype(BF16), seg))
    return outs[0] if len(outs) == 1 else jnp.concatenate(outs, axis=1)


def _mod_kernel(c_ref, w_ref, b_ref, o_ref):
    c = c_ref[...]
    s = c * jax.nn.sigmoid(c)
    o_ref[0] = _dot(s.astype(BF16), w_ref[0]) + b_ref[0]


def _modulation(c_rows, mod_w, mod_b):
    depth, d, n = mod_w.shape
    rows = c_rows.shape[0]
    tn = n // 4
    return pl.pallas_call(
        _mod_kernel,
        out_shape=jax.ShapeDtypeStruct((depth, rows, n), F32),
        grid=(depth, n // tn),
        in_specs=[
            pl.BlockSpec((rows, d), lambda i, j: (0, 0)),
            pl.BlockSpec((1, d, tn), lambda i, j: (i, 0, j)),
            pl.BlockSpec((1, 1, tn), lambda i, j: (i, 0, j)),
        ],
        out_specs=pl.BlockSpec((1, rows, tn), lambda i, j: (i, 0, j)),
        compiler_params=_cparams(2),
        name="modulation",
    )(c_rows, mod_w.astype(BF16), mod_b.reshape(depth, 1, n))


def _rope(x, cos, sin_signed):
    w = x.shape[1]
    lane = lax.broadcasted_iota(jnp.int32, x.shape, 1)
    first_half = (lane % (2 * ROPE_PAIRS)) < ROPE_PAIRS
    partner = jnp.where(first_half, pltpu.roll(x, w - ROPE_PAIRS, 1), pltpu.roll(x, ROPE_PAIRS, 1))
    return x * cos + partner * sin_signed


def _qkv_kernel(x_ref, mod_ref, g_ref, w_ref, qg_ref, kg_ref, cos_ref, sin_ref, seg_ref,
                ek_ref, ev_ref, q_ref, k_ref, v_ref, *, nq, nk):
    mod = mod_ref[0, 0]
    h = _norm_mod(x_ref[0], g_ref[...], mod[0:1], mod[1:2]).astype(BF16)
    qkv = _dot(h, w_ref[...])
    q, k, v = qkv[:, :nq], qkv[:, nq:nq + nk], qkv[:, nq + nk:]
    cos2, sin2 = cos_ref[...], sin_ref[...]

    def head_norm_rope(z, gain):
        reps = z.shape[1] // V7X_LANES
        cos = jnp.concatenate([cos2] * reps, axis=1)
        sin = jnp.concatenate([sin2] * reps, axis=1)
        ss = _group_sum(z * z, seg_ref)
        zn = z * lax.rsqrt(ss * (1.0 / HEAD_DIM) + NORM_EPS) * gain
        return _rope(zn, cos, sin)

    qn = head_norm_rope(q, qg_ref[...]) * (HEAD_DIM ** -0.5 * LOG2_E)
    q_ref[0] = qn.astype(BF16)
    kn = head_norm_rope(k, kg_ref[...]).astype(BF16)
    k_t = _dot(kn, ek_ref[...]).astype(BF16)
    v_bf = v.astype(BF16)
    ones = jnp.ones((ATTN_V_ROWS - HEAD_DIM, v_bf.shape[0]), BF16)
    shift_lane = lax.broadcasted_iota(jnp.int32, (v_bf.shape[0], V7X_LANES), 1) == HEAD_DIM
    for j in range(N_KV_HEADS):
        k_ref[0, j] = jnp.where(shift_lane, jnp.ones((), BF16), k_t[:, j * V7X_LANES:(j + 1) * V7X_LANES])
        v_ref[0, j, 0:HEAD_DIM, :] = _dot_nt(ev_ref[j], v_bf).astype(BF16)
        v_ref[0, j, HEAD_DIM:, :] = ones


def _head_select(n_heads, rows):
    sel = np.zeros((n_heads, rows, n_heads * HEAD_DIM), np.float32)
    for j in range(n_heads):
        sel[j, np.arange(HEAD_DIM), j * HEAD_DIM + np.arange(HEAD_DIM)] = 1.0
    return sel


def _qkv_project(xs, mods, g1, wqkv, q_gain, k_gain, cos2, sin2, seg, ctx_tiles):
    b, nt, d = xs.shape
    nq, nk = N_HEADS * HEAD_DIM, N_KV_HEADS * HEAD_DIM
    tm = ROW_TILE
    ek = np.zeros((nk, N_KV_HEADS * V7X_LANES), np.float32)
    for j in range(N_KV_HEADS):
        ek[j * HEAD_DIM + np.arange(HEAD_DIM), j * V7X_LANES + np.arange(HEAD_DIM)] = 1.0
    ev = _head_select(N_KV_HEADS, HEAD_DIM)
    kern = functools.partial(_qkv_kernel, nq=nq, nk=nk)
    return pl.pallas_call(
        kern,
        out_shape=(
            jax.ShapeDtypeStruct((b, nt, nq), BF16),
            jax.ShapeDtypeStruct((b, N_KV_HEADS, nt, V7X_LANES), BF16),
            jax.ShapeDtypeStruct((b, N_KV_HEADS, ATTN_V_ROWS, nt), BF16),
        ),
        grid=(b, nt // tm),
        in_specs=[
            pl.BlockSpec((1, tm, d), lambda i, t: (i, t, 0)),
            pl.BlockSpec((1, 1, N_MOD_ROWS, d), lambda i, t: (i, jnp.where(t < ctx_tiles, 0, 1), 0, 0)),
            _const_spec((1, d)),
            _const_spec((d, nq + 2 * nk)),
            _const_spec((1, nq)),
            _const_spec((1, nk)),
            pl.BlockSpec((tm, V7X_LANES), lambda i, t: (t, 0)),
            pl.BlockSpec((tm, V7X_LANES), lambda i, t: (t, 0)),
            _const_spec((GROUP_LANES, GROUP_LANES)),
            _const_spec(ek.shape),
            _const_spec(ev.shape),
        ],
        out_specs=(
            pl.BlockSpec((1, tm, nq), lambda i, t: (i, t, 0)),
            pl.BlockSpec((1, N_KV_HEADS, tm, V7X_LANES), lambda i, t: (i, 0, t, 0)),
            pl.BlockSpec((1, N_KV_HEADS, ATTN_V_ROWS, tm), lambda i, t: (i, 0, 0, t)),
        ),
        compiler_params=_cparams(2),
        name="qkv_project",
    )(xs, mods, g1.reshape(1, d), wqkv.astype(BF16),
      jnp.tile(q_gain, N_HEADS).reshape(1, nq), jnp.tile(k_gain, N_KV_HEADS).reshape(1, nk),
      cos2, sin2, seg, jnp.asarray(ek, BF16), jnp.asarray(ev, BF16))


def _attn_kernel(*refs, n_sub, tk, n_chunks, unroll):
    bound_ref = refs[0]
    q_refs = refs[1:1 + n_sub]
    (k_ref, vt_ref, selq_ref, place_ref, o_ref, qt_ref, m_ref, acc_ref, sa_ref, sb_ref,
     p_ref) = refs[1 + n_sub:]
    ts = q_refs[0].shape[1]
    for s, q_ref in enumerate(q_refs):
        for g in range(KV_GROUP):
            col = (s * KV_GROUP + g) * ts
            qt_ref[:, col:col + ts] = _dot_nt(selq_ref[g], q_ref[0]).astype(BF16)
    shift_rows = slice(HEAD_DIM, HEAD_DIM + BF16_SUBLANES)
    qt_ref[shift_rows, :] = jnp.full((BF16_SUBLANES, qt_ref.shape[1]), bound_ref[0], F32).astype(BF16)
    acc_ref[...] = jnp.zeros(acc_ref.shape, F32)

    def chunk(c, n=1):
        return pl.ds(c * tk if isinstance(c, int) else pl.multiple_of(c * tk, n * tk), n * tk)

    def scores(c):
        return _dot(k_ref[0, 0, chunk(c), :], qt_ref[...])

    @pl.when(bound_ref[1] > 0.5)
    def _():
        def accumulate(c0, n):
            for u in range(n):
                p_ref[u * tk:(u + 1) * tk, :] = jnp.exp2(scores(c0 + u)).astype(BF16)
            acc_ref[...] += _dot(vt_ref[0, 0, :, chunk(c0, n)], p_ref[0:n * tk, :])

        per_trip = p_ref.shape[0] // tk

        def trip(j, carry):
            accumulate(per_trip * j, per_trip)
            return carry

        n_trips, rem = n_chunks // per_trip, n_chunks % per_trip
        if n_trips:
            lax.fori_loop(0, n_trips, trip, 0)
        if rem:
            accumulate(n_trips * per_trip, rem)

    @pl.when(bound_ref[1] <= 0.5)
    def _():
        _attn_online_softmax(scores, chunk, vt_ref, m_ref, acc_ref, sa_ref, sb_ref, p_ref,
                             n_chunks=n_chunks, unroll=unroll)

    acc = acc_ref[...]
    ot = (acc[0:HEAD_DIM] / acc[HEAD_DIM:HEAD_DIM + 1]).astype(BF16)
    for s in range(n_sub):
        out = None
        for g in range(KV_GROUP):
            col = (s * KV_GROUP + g) * ts
            part = _dot_tn(ot[:, col:col + ts], place_ref[g])
            out = part if out is None else out + part
        o_ref[0, s * ts:(s + 1) * ts, :] = out.astype(o_ref.dtype)


def _attn_online_softmax(scores, chunk, vt_ref, m_ref, acc_ref, sa_ref, sb_ref, p_ref, *, n_chunks, unroll):
    m_ref[...] = jnp.full(m_ref.shape, -jnp.inf, F32)
    tk = sa_ref.shape[0]

    def update(c, s_ref):
        vtc = vt_ref[0, 0, :, chunk(c)]
        alphas = []
        for cb in range(s_ref.shape[1] // V7X_LANES):
            cols = slice(cb * V7X_LANES, (cb + 1) * V7X_LANES)
            st = s_ref[:, cols]
            m_prev = m_ref[:, cols]
            m_new = jnp.maximum(m_prev, jnp.max(st, axis=0, keepdims=True))
            alphas.append(jnp.exp2(m_prev - m_new))
            p_ref[0:tk, cols] = jnp.exp2(st - m_new).astype(BF16)
            m_ref[:, cols] = m_new
        alpha = jnp.concatenate(alphas, axis=1)
        acc_ref[...] = alpha * acc_ref[...] + _dot(vtc, p_ref[0:tk, :])

    slots = (sa_ref, sb_ref)
    sa_ref[...] = scores(0)

    def body(j, carry):
        c0 = unroll * j
        for u in range(unroll):
            slots[(u + 1) % 2][...] = scores(c0 + u + 1)
            update(c0 + u, slots[u % 2])
        return carry

    if n_chunks > 1:
        lax.fori_loop(0, (n_chunks - 1) // unroll, body, 0)
    update(n_chunks - 1, slots[0])


def _attention_call(bound, q, k, vt, *, row0, n_rows, n_keys, n_sub, name):
    b, _, nq = q.shape
    ts = ROW_TILE
    tk = ROW_TILE
    tq = n_sub * ts
    n_chunks = n_keys // tk
    assert n_rows % tq == 0 and row0 % ts == 0 and n_keys % tk == 0 and n_chunks % 2 == 1
    unroll = max(u for u in (2, 4) if (n_chunks - 1) % u == 0)
    selq = _head_select(KV_GROUP, V7X_LANES)
    place = _head_select(KV_GROUP, HEAD_DIM)
    kern = functools.partial(_attn_kernel, n_sub=n_sub, tk=tk, n_chunks=n_chunks, unroll=unroll)
    width = n_sub * KV_GROUP * ts

    def q_spec(s):
        return pl.BlockSpec((1, ts, GROUP_LANES), lambda i, j, t: (i, row0 // ts + n_sub * t + s, j))

    return pl.pallas_call(
        kern,
        out_shape=jax.ShapeDtypeStruct((b, n_rows, nq), BF16),
        grid=(b, N_KV_HEADS, n_rows // tq),
        in_specs=[pl.BlockSpec(memory_space=pltpu.SMEM)] + [q_spec(s) for s in range(n_sub)] + [
            pl.BlockSpec((1, 1, n_keys, V7X_LANES), lambda i, j, t: (i, j, 0, 0)),
            pl.BlockSpec((1, 1, ATTN_V_ROWS, n_keys), lambda i, j, t: (i, j, 0, 0)),
            _const_spec(selq.shape),
            _const_spec(place.shape),
        ],
        out_specs=pl.BlockSpec((1, tq, GROUP_LANES), lambda i, j, t: (i, t, j)),
        scratch_shapes=[
            pltpu.VMEM((V7X_LANES, width), BF16),
            pltpu.VMEM((1, width), F32),
            pltpu.VMEM((ATTN_V_ROWS, width), F32),
            pltpu.VMEM((tk, width), F32),
            pltpu.VMEM((tk, width), F32),
            pltpu.VMEM((min(ATTN_CHUNKS_PER_PV, n_chunks) * tk, width), BF16),
        ],
        compiler_params=_cparams(3),
        name=name,
    )(bound, *([q] * n_sub), k, vt, jnp.asarray(selq, BF16), jnp.asarray(place, BF16))


def _score_bound(q_gain, k_gain):
    bound = (HEAD_DIM * (HEAD_DIM ** -0.5 * LOG2_E) * 1.02) * jnp.max(jnp.abs(q_gain)) * jnp.max(jnp.abs(k_gain))
    bound = bound.astype(BF16).astype(F32)
    return jnp.stack([-bound, (2.0 * bound <= EXP2_SAFE_RANGE).astype(F32)])


def _attention(q, k, vt, bound, ctx_len):
    nt = q.shape[1]
    n_lat = nt - ctx_len
    n_sub = 2 if n_lat % (2 * ROW_TILE) == 0 else 1
    o_ctx = _attention_call(bound, q, k, vt, row0=0, n_rows=ctx_len, n_keys=ctx_len, n_sub=1,
                            name="flash_attention_ctx")
    o_lat = _attention_call(bound, q, k, vt, row0=ctx_len, n_rows=n_lat, n_keys=nt, n_sub=n_sub,
                            name="flash_attention")
    return o_ctx, o_lat


def _out_ffn_kernel(*refs, final_norm, ctx_tiles):
    if ctx_tiles:
        yc_ref, refs = refs[0], refs[1:]
    y_ref, x_ref, mod_ref, g_ref, wo_ref, wg_ref, wu_ref, wd_ref, fg_ref, o_ref = refs
    y = y_ref[0]
    if ctx_tiles:
        y = jnp.where(pl.program_id(1) < ctx_tiles, yc_ref[0], y)
    mod = mod_ref[0, 0]
    x1 = x_ref[0] + mod[2:3] * _dot(y, wo_ref[...])
    h2 = _norm_mod(x1, g_ref[...], mod[3:4], mod[4:5]).astype(BF16)
    a = _dot(h2, wg_ref[...])
    u = _dot(h2, wu_ref[...])
    hid = (a * jax.nn.sigmoid(a) * u).astype(BF16)
    x2 = x1 + mod[5:6] * _dot(hid, wd_ref[...])
    if final_norm:
        ms = jnp.mean(x2 * x2, axis=-1, keepdims=True)
        x2 = x2 * lax.rsqrt(ms + NORM_EPS) * fg_ref[...]
    o_ref[0] = x2


def _out_ffn(y_ctx, y_lat, xs, mods, g2, wo, wg, wu, wd, final_g, *, ctx_tiles, final_norm):
    b, n_lat, d = y_lat.shape
    f = wg.shape[1]
    tm = ROW_TILE
    out_ctx = ctx_tiles if y_ctx is not None else 0
    x_tile_offset = ctx_tiles - out_ctx
    n = n_lat + out_ctx * tm
    kern = functools.partial(_out_ffn_kernel, final_norm=final_norm, ctx_tiles=out_ctx)

    def mod_sel(i, t):
        return (i, jnp.where(t + x_tile_offset < ctx_tiles, 0, 1), 0, 0)

    y_specs, y_args = [], []
    if out_ctx:
        y_specs.append(pl.BlockSpec((1, tm, d), lambda i, t: (i, jnp.minimum(t, out_ctx - 1), 0)))
        y_args.append(y_ctx)
    return pl.pallas_call(
        kern,
        out_shape=jax.ShapeDtypeStruct((b, n, d), F32),
        grid=(b, n // tm),
        in_specs=y_specs + [
            pl.BlockSpec((1, tm, d), lambda i, t: (i, jnp.maximum(t - out_ctx, 0), 0)),
            pl.BlockSpec((1, tm, d), lambda i, t: (i, t + x_tile_offset, 0)),
            pl.BlockSpec((1, 1, N_MOD_ROWS, d), mod_sel),
            _const_spec((1, d)),
            _const_spec((d, d)),
            _const_spec((d, f)),
            _const_spec((d, f)),
            _const_spec((f, d)),
            _const_spec((1, d)),
        ],
        out_specs=pl.BlockSpec((1, tm, d), lambda i, t: (i, t, 0)),
        compiler_params=_cparams(2),
        name="out_ffn_final" if final_norm else "out_ffn",
    )(*y_args, y_lat, xs, mods, g2.reshape(1, d), wo.astype(BF16), wg.astype(BF16), wu.astype(BF16),
      wd.astype(BF16), final_g.reshape(1, d))


def _rwkv_proj_kernel(x_ref, xp_ref, xn_ref, mod_ref, g_ref, mix_ref, wr_ref, wk_ref, wv_ref,
                      w1_ref, w2_ref, a1_ref, a2_ref, g1_ref, g2_ref, w0_ref, a0_ref, kk_ref, ka_ref,
                      seg_ref,
                      r_ref, v_ref, kkn_ref, gate_ref, lw0_ref, lw1_ref, kd0_ref, kd1_ref,
                      b0_ref, b1_ref, *, tm, seq_starts, seq_ends):
    t = pl.program_id(1)
    mod = mod_ref[0, 0]
    g = g_ref[...]
    h = _norm_mod(x_ref[0], g, mod[0:1], mod[1:2])
    h_prev = _norm_mod(xp_ref[0], g, mod[0:1], mod[1:2])[7:8]
    h_next = _norm_mod(xn_ref[0], g, mod[0:1], mod[1:2])[0:1]
    row = lax.broadcasted_iota(jnp.int32, h.shape, 0)
    pos = row + t * tm
    at_start = functools.reduce(jnp.logical_or, [pos == s for s in seq_starts])
    at_end = functools.reduce(jnp.logical_or, [pos == e for e in seq_ends])
    before = jnp.where(row == 0, h_prev, pltpu.roll(h, 1, 0))
    after = jnp.where(row == tm - 1, h_next, pltpu.roll(h, tm - 1, 0))
    before = jnp.where(at_start, 0.0, before)
    after = jnp.where(at_end, 0.0, after)
    xx = 0.5 * (before + after) - h
    mix = mix_ref[...]

    def lerp(j):
        return (h + xx * mix[j:j + 1]).astype(BF16)

    r_ref[0] = _dot(lerp(0), wr_ref[...]).astype(r_ref.dtype)
    k = _dot(lerp(2), wk_ref[...])
    v_ref[0] = _dot(lerp(3), wv_ref[...]).astype(v_ref.dtype)
    kkr = k * kk_ref[...]
    ss = _group_sum(kkr * kkr, seg_ref)
    kkn = kkr * lax.rsqrt(jnp.maximum(ss, 1e-24))
    kkn_ref[0] = kkn.astype(kkn_ref.dtype)
    xw, xa = lerp(1), lerp(4)
    ka = ka_ref[...]
    for d, (lw_ref, kd_ref, b_ref) in enumerate(((lw0_ref, kd0_ref, b0_ref), (lw1_ref, kd1_ref, b1_ref))):
        z = w0_ref[d:d + 1] + _dot(jnp.tanh(_dot(xw, w1_ref[d])).astype(BF16), w2_ref[d])
        lw_ref[0] = -DECAY_SCALE * jax.nn.sigmoid(z)
        a = jax.nn.sigmoid(a0_ref[d:d + 1] + _dot(_dot(xa, a1_ref[d]).astype(BF16), a2_ref[d]))
        kd_ref[0] = (k * (1.0 + (a - 1.0) * ka)).astype(kd_ref.dtype)
        b_ref[0] = (kkn * a).astype(b_ref.dtype)
    gate = _dot(jax.nn.sigmoid(_dot(lerp(5), g1_ref[...])).astype(BF16), g2_ref[...])
    gate_ref[0] = gate.astype(gate_ref.dtype)


def _rwkv_project(xs, mods, g1n, mix, w_rkv, w0, w1, w2, a0, a1, a2, gw1, gw2, k_k, k_a, seg,
                  ctx_len):
    b, nt, d = xs.shape
    tm = ROW_TILE
    halo = 8
    per = tm // halo
    n_halo = nt // halo
    kern = functools.partial(_rwkv_proj_kernel, tm=tm, seq_starts=(0, ctx_len),
                             seq_ends=(ctx_len - 1, nt - 1))
    tok = pl.BlockSpec((1, tm, d), lambda i, t: (i, t, 0))
    bf = lambda w: w.astype(BF16)
    outs = pl.pallas_call(
        kern,
        out_shape=tuple(jax.ShapeDtypeStruct((b, nt, d), dt)
                        for dt in (BF16, BF16, BF16, BF16, F32, F32, BF16, BF16, BF16, BF16)),
        grid=(b, nt // tm),
        in_specs=[
            tok,
            pl.BlockSpec((1, halo, d), lambda i, t: (i, jnp.maximum(t * per - 1, 0), 0)),
            pl.BlockSpec((1, halo, d), lambda i, t: (i, jnp.minimum((t + 1) * per, n_halo - 1), 0)),
            pl.BlockSpec((1, 1, N_MOD_ROWS, d), lambda i, t: (i, jnp.where(t * tm < ctx_len, 0, 1), 0, 0)),
            _const_spec((1, d)),
            _const_spec((N_MOD_ROWS, d)),
            _const_spec((d, d)), _const_spec((d, d)), _const_spec((d, d)),
            _const_spec(w1.shape), _const_spec(w2.shape), _const_spec(a1.shape), _const_spec(a2.shape),
            _const_spec(gw1.shape), _const_spec(gw2.shape),
            _const_spec((2, d)), _const_spec((2, d)), _const_spec((1, d)), _const_spec((1, d)),
            _const_spec((GROUP_LANES, GROUP_LANES)),
        ],
        out_specs=tuple(tok for _ in range(10)),
        compiler_params=_cparams(2),
        name="rwkv_project",
    )(xs, xs, xs, mods, g1n.reshape(1, d),
      jnp.concatenate([mix, jnp.zeros((N_MOD_ROWS - mix.shape[0], d), F32)], axis=0),
      bf(w_rkv[0]), bf(w_rkv[1]), bf(w_rkv[2]), bf(w1), bf(w2), bf(a1), bf(a2), bf(gw1), bf(gw2),
      w0, a0, k_k.reshape(1, d), k_a.reshape(1, d), seg)
    return outs


def _block_diag(x_bf, mask):
    return jnp.where(mask, jnp.concatenate([x_bf] * HEADS_PER_GROUP, axis=0), jnp.zeros((), BF16))


def _diag_blocks(full):
    lane_head = lax.broadcasted_iota(jnp.int32, (RWKV_HEAD, GROUP_LANES), 1) // RWKV_HEAD
    out = full[0:RWKV_HEAD]
    for j in range(1, HEADS_PER_GROUP):
        out = jnp.where(lane_head == j, full[j * RWKV_HEAD:(j + 1) * RWKV_HEAD], out)
    return out


def _wkv_masks(reverse):
    c = WKV_CHUNK
    rows = lax.broadcasted_iota(jnp.int32, (c, GROUP_LANES), 0)
    cols = lax.broadcasted_iota(jnp.int32, (c, GROUP_LANES), 1) % RWKV_HEAD
    tri_r = lax.broadcasted_iota(jnp.int32, (c, c), 0)
    tri_c = lax.broadcasted_iota(jnp.int32, (c, c), 1)
    if reverse:
        return cols > rows, cols >= rows, (tri_c >= tri_r).astype(BF16)
    return cols < rows, cols <= rows, (tri_c <= tri_r).astype(BF16)


def _wkv_chains(chains):
    c = WKV_CHUNK
    n = len(chains)
    rng = range(n)
    rows = lax.broadcasted_iota(jnp.int32, (c, GROUP_LANES), 0)
    cols = lax.broadcasted_iota(jnp.int32, (c, GROUP_LANES), 1) % RWKV_HEAD
    eye = (cols == rows).astype(F32)
    bd_r = lax.broadcasted_iota(jnp.int32, (GROUP_LANES, GROUP_LANES), 0) // RWKV_HEAD
    bd_c = lax.broadcasted_iota(jnp.int32, (GROUP_LANES, GROUP_LANES), 1) // RWKV_HEAD
    bmask = bd_r == bd_c
    masks = {rev: _wkv_masks(rev) for rev in sorted({ch[7] for ch in chains})}
    strict = [masks[ch[7]][0] for ch in chains]
    incl = [masks[ch[7]][1] for ch in chains]
    tri = [masks[ch[7]][2] for ch in chains]
    lw, kd, bb, kk, v, r, s0 = ([ch[i] for ch in chains] for i in range(7))
    bd = lambda x: _block_diag(x.astype(BF16), bmask)

    split = [_split3(lw[i]) for i in rng]
    cs = [_dot(tri[i], split[i][0]) + _dot(tri[i], split[i][1]) + _dot(tri[i], split[i][2]) for i in rng]
    total = [cs[i][0:1] if chains[i][7] else cs[i][c - 1:c] for i in rng]
    a_s = [-kk[i] * jnp.exp(cs[i] - lw[i]) for i in rng]
    g_inv = [jnp.exp(-cs[i]) for i in rng]
    b_s = [bb[i] * g_inv[i] for i in rng]
    k_s = [kd[i] * g_inv[i] for i in rng]
    r_s = [r[i] * jnp.exp(cs[i]) for i in rng]
    g_rest = [jnp.exp(total[i] - cs[i]) for i in rng]
    b_e = [(bb[i] * g_rest[i]).astype(BF16) for i in rng]
    k_e = [(kd[i] * g_rest[i]).astype(BF16) for i in rng]
    g_end = [jnp.exp(total[i]) for i in rng]

    lhs = [jnp.concatenate([a_s[i], r_s[i]], axis=0).astype(BF16) for i in rng]
    pb = [_dot_nt(lhs[i], bd(b_s[i])) for i in rng]
    pk = [_dot_nt(lhs[i], bd(k_s[i])) for i in rng]
    l_ab = [jnp.where(strict[i], pb[i][:c], 0.0) for i in rng]
    m_rb = [jnp.where(incl[i], pb[i][c:], 0.0).astype(BF16) for i in rng]
    l_ak = [jnp.where(strict[i], pk[i][:c], 0.0) for i in rng]
    m_rk = [jnp.where(incl[i], pk[i][c:], 0.0) for i in rng]

    p = l_ab
    tmat = [eye + p[i] for i in rng]
    p = [_dot(p[i].astype(BF16), bd(p[i])) for i in rng]
    for _ in range(4):
        z = [_dot(jnp.concatenate([tmat[i], p[i]], axis=0).astype(BF16), bd(p[i])) for i in rng]
        tmat = [tmat[i] + z[i][:c] for i in rng]
        p = [z[i][c:] for i in rng]
    tmat = [tmat[i] + _dot(tmat[i].astype(BF16), bd(p[i])) for i in rng]
    t_bf = [tmat[i].astype(BF16) for i in rng]

    zv = [_dot(jnp.concatenate([l_ak[i], m_rk[i]], axis=0).astype(BF16), bd(v[i])) for i in rng]
    a_hat = [_dot(t_bf[i], bd(a_s[i])) for i in rng]
    u_til = [_dot(t_bf[i], bd(zv[i][:c])) for i in rng]
    r_hat = [r_s[i] + _dot(m_rb[i], bd(a_hat[i])) for i in rng]
    y_til = [zv[i][c:] + _dot(m_rb[i], bd(u_til[i])) for i in rng]

    full = [_dot_tn(jnp.concatenate([a_hat[i], u_til[i]], axis=1).astype(BF16), b_e[i]) for i in rng]
    g_mat = [_diag_blocks(full[i][:GROUP_LANES]) for i in rng]
    h_mat = [_diag_blocks(full[i][GROUP_LANES:] + _dot_tn(v[i].astype(BF16), k_e[i])) for i in rng]

    y = [y_til[i] + _dot_nt(r_hat[i].astype(BF16), bd(s0[i])) for i in rng]
    s1 = [s0[i] * g_end[i] + _dot(s0[i].astype(BF16), bd(g_mat[i])) + h_mat[i] for i in rng]
    return y, s1


def _wkv_kernel(lwf, kdf, bf_, kkf, vf, rf, lwr, kdr, br, kkr, vr, rr, yf_ref, yr_ref, s_ref):
    @pl.when(pl.program_id(1) == 0)
    def _():
        s_ref[...] = jnp.zeros(s_ref.shape, F32)

    n_groups = lwf.shape[2] // GROUP_LANES
    chains, dests = [], []
    for j in range(n_groups):
        sl = slice(j * GROUP_LANES, (j + 1) * GROUP_LANES)
        for d, (refs, y_ref) in enumerate((((lwf, kdf, bf_, kkf, vf, rf), yf_ref),
                                           ((lwr, kdr, br, kkr, vr, rr), yr_ref))):
            lw, kd, bb, kk, v, r = (ref[0, :, sl] for ref in refs)
            chains.append((lw, kd.astype(F32), bb.astype(F32), kk.astype(F32), v, r.astype(F32),
                           s_ref[d, :, sl], d == 1))
            dests.append((y_ref, d, sl))
    ys, s1s = _wkv_chains(chains)
    for (y_ref, d, sl), y, s1 in zip(dests, ys, s1s):
        y_ref[0, :, sl] = y.astype(y_ref.dtype)
        s_ref[d, :, sl] = s1


def _wkv_scan(lw0, lw1, kd0, kd1, b0, b1, kk, v, r, ctx_len):
    b, nt, d = v.shape
    c = WKV_CHUNK
    n_steps = nt // c
    ctx_chunks = ctx_len // c

    def fwd(i, s):
        return (i, s, 0)

    def rev(i, s):
        return (i, jnp.where(s < ctx_chunks, ctx_chunks - 1 - s, n_steps - 1 + ctx_chunks - s), 0)

    blk_f = pl.BlockSpec((1, c, d), fwd)
    blk_r = pl.BlockSpec((1, c, d), rev)
    return pl.pallas_call(
        _wkv_kernel,
        out_shape=(jax.ShapeDtypeStruct((b, nt, d), BF16), jax.ShapeDtypeStruct((b, nt, d), BF16)),
        grid=(b, n_steps),
        in_specs=[blk_f] * 6 + [blk_r] * 6,
        out_specs=(blk_f, blk_r),
        scratch_shapes=[pltpu.VMEM((2, RWKV_HEAD, d), F32)],
        compiler_params=_cparams(2),
        name="wkv_scan",
    )(lw0, kd0, b0, kk, v, r, lw1, kd1, b1, kk, v, r)


def _readout_kernel(yf_ref, yr_ref, r_ref, v_ref, kd0_ref, kd1_ref, gate_ref, rk_ref, lg_ref, lb_ref,
                    seg_ref, o_ref):
    wkv = yf_ref[0].astype(F32) + yr_ref[0].astype(F32)
    inv_n = 1.0 / RWKV_HEAD
    mu = _group_sum(wkv, seg_ref, exact=True) * inv_n
    dev = wkv - mu
    var = _group_sum(dev * dev, seg_ref, exact=True) * inv_n
    gn = dev * lax.rsqrt(var + GN_EPS) * lg_ref[...] + lb_ref[...]
    rk = r_ref[0].astype(F32) * rk_ref[...]
    kd_sum = kd0_ref[0].astype(F32) + kd1_ref[0].astype(F32)
    bonus = _group_sum(rk * kd_sum, seg_ref, exact=True) * v_ref[0].astype(F32)
    o_ref[0] = ((gn + bonus) * gate_ref[0].astype(F32)).astype(o_ref.dtype)


def _rwkv_readout(yf, yr, r, v, kd0, kd1, gate, r_k, ln_g, ln_b, seg, ctx_len):
    b, nt, d = v.shape
    tm = ROW_TILE
    off = ctx_len // tm
    n = nt - ctx_len
    tok = pl.BlockSpec((1, tm, d), lambda i, t: (i, t + off, 0))
    return pl.pallas_call(
        _readout_kernel,
        out_shape=jax.ShapeDtypeStruct((b, n, d), BF16),
        grid=(b, n // tm),
        in_specs=[tok] * 7 + [_const_spec((1, d))] * 3 + [_const_spec((GROUP_LANES, GROUP_LANES))],
        out_specs=pl.BlockSpec((1, tm, d), lambda i, t: (i, t, 0)),
        compiler_params=_cparams(2),
        name="rwkv_readout",
    )(yf, yr, r, v, kd0, kd1, gate, r_k.reshape(1, d), ln_g.reshape(1, d), ln_b.reshape(1, d), seg)


def _rope_tables(ctx_len, seq_len):
    rows = seq_len // GRID_W
    row = jnp.repeat(jnp.arange(rows, dtype=F32), GRID_W, total_repeat_length=seq_len)
    col = jnp.tile(jnp.arange(GRID_W, dtype=F32), rows)
    inv_freq = ROPE_THETA ** (-jnp.arange(ROPE_PAIRS, dtype=F32) / ROPE_PAIRS)
    row_ang, col_ang = row[:, None] * inv_freq, col[:, None] * inv_freq
    ang = jnp.concatenate([row_ang, row_ang, col_ang, col_ang], axis=1)
    cos = jnp.concatenate([jnp.ones((ctx_len, HEAD_DIM), F32), jnp.cos(ang)], axis=0)
    sin = jnp.concatenate([jnp.zeros((ctx_len, HEAD_DIM), F32), jnp.sin(ang)], axis=0)
    sign = jnp.asarray(np.concatenate([-np.ones(ROPE_PAIRS), np.ones(ROPE_PAIRS)] * 2), F32)
    reps = V7X_LANES // HEAD_DIM
    return jnp.tile(cos, (1, reps)), jnp.tile(sin * sign, (1, reps))


def kernel(x, c, ctx, c_ctx, mod_w, mod_b, norm1_g, norm2_g, ffn_wg, ffn_wu, ffn_wd, attn_wqkv,
           attn_q_gain, attn_k_gain, attn_wo, rwkv_mix, rwkv_wrkv, rwkv_w0, rwkv_w1, rwkv_w2, rwkv_a0,
           rwkv_a1, rwkv_a2, rwkv_g1, rwkv_g2, rwkv_k_k, rwkv_k_a, rwkv_r_k, rwkv_ln_g, rwkv_ln_b,
           rwkv_wo, final_g):
    b, seq_len, d = x.shape
    ctx_len = ctx.shape[1]
    depth = mod_w.shape[0]
    assert depth == 2 and d == N_HEADS * HEAD_DIM
    assert ctx_len % ROW_TILE == 0 and seq_len % ROW_TILE == 0 and seq_len % GRID_W == 0
    ctx_tiles = ctx_len // ROW_TILE

    n_rows = -(-(b + 1) // 8) * 8
    c_rows = jnp.concatenate([c, c_ctx[None], jnp.zeros((n_rows - b - 1, d), F32)], axis=0)
    m_all = _modulation(c_rows, mod_w, mod_b).reshape(depth, n_rows, 6, d)
    pad = jnp.zeros((depth, b, N_MOD_ROWS - 6, d), F32)
    lat = jnp.concatenate([m_all[:, :b], pad], axis=2)
    con = jnp.concatenate([jnp.broadcast_to(m_all[:, b:b + 1], (depth, b, 6, d)), pad], axis=2)
    mods = jnp.stack([con, lat], axis=2)

    cos2, sin2 = _rope_tables(ctx_len, seq_len)
    seg = jnp.asarray(np.kron(np.eye(HEADS_PER_GROUP), np.ones((HEAD_DIM, HEAD_DIM))), BF16)

    xs = jnp.concatenate([ctx, x], axis=1)

    q, k, vt = _qkv_project(xs, mods[0], norm1_g[0], attn_wqkv[0], attn_q_gain[0], attn_k_gain[0],
                             cos2, sin2, seg, ctx_tiles)
    o_ctx, o_lat = _attention(q, k, vt, _score_bound(attn_q_gain[0], attn_k_gain[0]), ctx_len)
    xs = _out_ffn(o_ctx, o_lat, xs, mods[0], norm2_g[0], attn_wo[0], ffn_wg[0], ffn_wu[0], ffn_wd[0],
                  final_g, ctx_tiles=ctx_tiles, final_norm=False)

    r, v, kk, gate, lw0, lw1, kd0, kd1, b0, b1 = _rwkv_project(
        xs, mods[1], norm1_g[1], rwkv_mix[0], rwkv_wrkv[0], rwkv_w0[0], rwkv_w1[0], rwkv_w2[0],
        rwkv_a0[0], rwkv_a1[0], rwkv_a2[0], rwkv_g1[0], rwkv_g2[0], rwkv_k_k[0], rwkv_k_a[0], seg,
        ctx_len)
    yf, yr = _wkv_scan(lw0, lw1, kd0, kd1, b0, b1, kk, v, r, ctx_len)
    mix_out = _rwkv_readout(yf, yr, r, v, kd0, kd1, gate, rwkv_r_k[0].reshape(-1), rwkv_ln_g[0],
                            rwkv_ln_b[0], seg, ctx_len)
    return _out_ffn(None, mix_out, xs, mods[1], norm2_g[1], rwkv_wo[0], ffn_wg[1], ffn_wu[1], ffn_wd[1],
                    final_g, ctx_tiles=ctx_tiles, final_norm=True)
```

```python
import functools
from typing import Callable, NamedTuple

import jax
import jax.numpy as jnp
import numpy as np
from jax import lax
from jax.experimental import pallas as pl
from jax.experimental.pallas import tpu as pltpu

F32 = jnp.float32
BF16 = jnp.bfloat16

NORM_EPS = 1e-6
GN_EPS = 64e-5
GRID_W = 64
N_HEADS = 16
N_KV_HEADS = 4
KV_GROUP = N_HEADS // N_KV_HEADS
HEAD_DIM = 64
ROPE_THETA = 10000.0
ROPE_PAIRS = HEAD_DIM // 4
RWKV_HEAD = 64
DECAY_SCALE = float(np.exp(-0.5))
LOG2_E = float(np.log2(np.e))
EXP2_SAFE_RANGE = 120.0

V7X_LANES = 128
V7X_MXU_DIM = 256
V7X_VMEM_LIMIT_BYTES = 60000 * 1024

ROW_TILE = 256
GROUP_LANES = V7X_MXU_DIM
HEADS_PER_GROUP = GROUP_LANES // HEAD_DIM
BF16_SUBLANES = 16
ATTN_V_ROWS = HEAD_DIM + BF16_SUBLANES
ATTN_CHUNKS_PER_PV = 8
WKV_CHUNK = 64
N_MOD_ROWS = 8


def _cparams(n_axes):
    return pltpu.CompilerParams(
        dimension_semantics=("arbitrary",) * n_axes,
        vmem_limit_bytes=V7X_VMEM_LIMIT_BYTES,
    )


def _const_spec(shape):
    nd = len(shape)
    return pl.BlockSpec(shape, lambda *_: (0,) * nd, pipeline_mode=pl.Buffered(1))


def _dot(a, b):
    return jnp.dot(a, b, preferred_element_type=F32)


def _dot_nt(a, b):
    return lax.dot_general(a, b, (((1,), (1,)), ((), ())), preferred_element_type=F32)


def _dot_tn(a, b):
    return lax.dot_general(a, b, (((0,), (0,)), ((), ())), preferred_element_type=F32)


def _norm_mod(x, g, shift, scale):
    ms = jnp.mean(x * x, axis=-1, keepdims=True)
    return (x * lax.rsqrt(ms + NORM_EPS) * g) * (1.0 + scale) + shift


def _group_sum(x, seg_ref, two_pass=False):
    seg = seg_ref[...]
    outs = []
    for j in range(x.shape[1] // GROUP_LANES):
        xs = x[:, j * GROUP_LANES:(j + 1) * GROUP_LANES]
        hi = xs.astype(BF16)
        out = _dot(hi, seg)
        if two_pass:
            out = out + _dot((xs - hi.astype(F32)).astype(BF16), seg)
        outs.append(out)
    return outs[0] if len(outs) == 1 else jnp.concatenate(outs, axis=1)


def _mod_kernel(c_ref, w_ref, b_ref, o_ref):
    c = c_ref[...]
    s = c * jax.nn.sigmoid(c)
    o_ref[0] = _dot(s.astype(BF16), w_ref[0]) + b_ref[0]


def _modulation(c_rows, mod_w, mod_b):
    depth, d, n = mod_w.shape
    rows = c_rows.shape[0]
    tn = n // 4
    return pl.pallas_call(
        _mod_kernel,
        out_shape=jax.ShapeDtypeStruct((depth, rows, n), F32),
        grid=(depth, n // tn),
        in_specs=[
            pl.BlockSpec((rows, d), lambda i, j: (0, 0)),
            pl.BlockSpec((1, d, tn), lambda i, j: (i, 0, j)),
            pl.BlockSpec((1, 1, tn), lambda i, j: (i, 0, j)),
        ],
        out_specs=pl.BlockSpec((1, rows, tn), lambda i, j: (i, 0, j)),
        compiler_params=_cparams(2),
        name="modulation",
    )(c_rows, mod_w.astype(BF16), mod_b.reshape(depth, 1, n))


def _rope(x, cos, sin_signed):
    w = x.shape[1]
    lane = lax.broadcasted_iota(jnp.int32, x.shape, 1)
    first_half = (lane % (2 * ROPE_PAIRS)) < ROPE_PAIRS
    partner = jnp.where(first_half, pltpu.roll(x, w - ROPE_PAIRS, 1), pltpu.roll(x, ROPE_PAIRS, 1))
    return x * cos + partner * sin_signed


class _Rows(NamedTuple):
    read: Callable
    specs: list
    args: list
    shape: tuple


def _stream_of(ctx, x):
    b, c, d = ctx.shape
    tm = ROW_TILE
    ctx_tiles = c // tm

    def read(refs):
        c_ref, x_ref = refs
        return jnp.where(pl.program_id(1) < ctx_tiles, c_ref[0], x_ref[0])

    specs = [pl.BlockSpec((1, tm, d), lambda i, t: (i, jnp.minimum(t, ctx_tiles - 1), 0)),
             pl.BlockSpec((1, tm, d), lambda i, t: (i, jnp.maximum(t - ctx_tiles, 0), 0))]
    return _Rows(read, specs, [ctx, x], (b, c + x.shape[1], d))


def _rows_from(xs, tile_offset):
    b, n, d = xs.shape
    spec = pl.BlockSpec((1, ROW_TILE, d), lambda i, t: (i, t + tile_offset, 0))
    return _Rows(lambda refs: refs[0][0], [spec], [xs], (b, n - tile_offset * ROW_TILE, d))


def _qkv_kernel(*refs, nq, nk, read_x, n_x):
    x = read_x(refs[:n_x])
    (mod_ref, g_ref, w_ref, qg_ref, kg_ref, cos_ref, sin_ref, seg_ref, ek_ref, ev_ref,
     q_ref, k_ref, v_ref) = refs[n_x:]
    mod = mod_ref[0, 0]
    h = _norm_mod(x, g_ref[...], mod[0:1], mod[1:2]).astype(BF16)
    qkv = _dot(h, w_ref[...])
    q, k, v = qkv[:, :nq], qkv[:, nq:nq + nk], qkv[:, nq + nk:]
    cos2, sin2 = cos_ref[...], sin_ref[...]

    def head_norm_rope(z, gain):
        reps = z.shape[1] // V7X_LANES
        cos = jnp.concatenate([cos2] * reps, axis=1)
        sin = jnp.concatenate([sin2] * reps, axis=1)
        ss = _group_sum(z * z, seg_ref)
        zn = z * lax.rsqrt(ss * (1.0 / HEAD_DIM) + NORM_EPS) * gain
        return _rope(zn, cos, sin)

    qn = head_norm_rope(q, qg_ref[...]) * (HEAD_DIM ** -0.5 * LOG2_E)
    q_ref[0] = qn.astype(BF16)
    kn = head_norm_rope(k, kg_ref[...]).astype(BF16)
    k_t = _dot(kn, ek_ref[...]).astype(BF16)
    v_bf = v.astype(BF16)
    ones = jnp.ones((ATTN_V_ROWS - HEAD_DIM, v_bf.shape[0]), BF16)
    shift_lane = lax.broadcasted_iota(jnp.int32, (v_bf.shape[0], V7X_LANES), 1) == HEAD_DIM
    for j in range(N_KV_HEADS):
        k_ref[0, j] = jnp.where(shift_lane, jnp.ones((), BF16), k_t[:, j * V7X_LANES:(j + 1) * V7X_LANES])
        v_ref[0, j, 0:HEAD_DIM, :] = _dot_nt(ev_ref[j], v_bf).astype(BF16)
        v_ref[0, j, HEAD_DIM:, :] = ones


def _head_select(n_heads, rows):
    sel = np.zeros((n_heads, rows, n_heads * HEAD_DIM), np.float32)
    for j in range(n_heads):
        sel[j, np.arange(HEAD_DIM), j * HEAD_DIM + np.arange(HEAD_DIM)] = 1.0
    return sel


def _qkv_project(rows, mods, g1, wqkv, q_gain, k_gain, cos2, sin2, seg, ctx_tiles):
    b, nt, d = rows.shape
    nq, nk = N_HEADS * HEAD_DIM, N_KV_HEADS * HEAD_DIM
    tm = ROW_TILE
    ek = np.zeros((nk, N_KV_HEADS * V7X_LANES), np.float32)
    for j in range(N_KV_HEADS):
        ek[j * HEAD_DIM + np.arange(HEAD_DIM), j * V7X_LANES + np.arange(HEAD_DIM)] = 1.0
    ev = _head_select(N_KV_HEADS, HEAD_DIM)
    kern = functools.partial(_qkv_kernel, nq=nq, nk=nk, read_x=rows.read, n_x=len(rows.specs))
    return pl.pallas_call(
        kern,
        out_shape=(
            jax.ShapeDtypeStruct((b, nt, nq), BF16),
            jax.ShapeDtypeStruct((b, N_KV_HEADS, nt, V7X_LANES), BF16),
            jax.ShapeDtypeStruct((b, N_KV_HEADS, ATTN_V_ROWS, nt), BF16),
        ),
        grid=(b, nt // tm),
        in_specs=rows.specs + [
            pl.BlockSpec((1, 1, N_MOD_ROWS, d), lambda i, t: (i, jnp.where(t < ctx_tiles, 0, 1), 0, 0)),
            _const_spec((1, d)),
            _const_spec((d, nq + 2 * nk)),
            _const_spec((1, nq)),
            _const_spec((1, nk)),
            pl.BlockSpec((tm, V7X_LANES), lambda i, t: (t, 0)),
            pl.BlockSpec((tm, V7X_LANES), lambda i, t: (t, 0)),
            _const_spec((GROUP_LANES, GROUP_LANES)),
            _const_spec(ek.shape),
            _const_spec(ev.shape),
        ],
        out_specs=(
            pl.BlockSpec((1, tm, nq), lambda i, t: (i, t, 0)),
            pl.BlockSpec((1, N_KV_HEADS, tm, V7X_LANES), lambda i, t: (i, 0, t, 0)),
            pl.BlockSpec((1, N_KV_HEADS, ATTN_V_ROWS, tm), lambda i, t: (i, 0, 0, t)),
        ),
        compiler_params=_cparams(2),
        name="qkv_project",
    )(*rows.args, mods, g1.reshape(1, d), wqkv.astype(BF16),
      jnp.tile(q_gain, N_HEADS).reshape(1, nq), jnp.tile(k_gain, N_KV_HEADS).reshape(1, nk),
      cos2, sin2, seg, jnp.asarray(ek, BF16), jnp.asarray(ev, BF16))


def _attn_kernel(*refs, n_sub, tk, n_chunks, unroll):
    bound_ref = refs[0]
    q_refs = refs[1:1 + n_sub]
    (k_ref, vt_ref, selq_ref, place_ref, o_ref, qt_ref, m_ref, acc_ref, sa_ref, sb_ref,
     p_ref) = refs[1 + n_sub:]
    ts = q_refs[0].shape[1]
    for s, q_ref in enumerate(q_refs):
        for g in range(KV_GROUP):
            col = (s * KV_GROUP + g) * ts
            qt_ref[:, col:col + ts] = _dot_nt(selq_ref[g], q_ref[0]).astype(BF16)
    shift_rows = slice(HEAD_DIM, HEAD_DIM + BF16_SUBLANES)
    qt_ref[shift_rows, :] = jnp.full((BF16_SUBLANES, qt_ref.shape[1]), bound_ref[0], F32).astype(BF16)
    acc_ref[...] = jnp.zeros(acc_ref.shape, F32)

    def chunk(c, n=1):
        return pl.ds(c * tk if isinstance(c, int) else pl.multiple_of(c * tk, n * tk), n * tk)

    def scores(c):
        return _dot(k_ref[0, 0, chunk(c), :], qt_ref[...])

    @pl.when(bound_ref[1] > 0.5)
    def _():
        def accumulate(c0, n):
            for u in range(n):
                p_ref[u * tk:(u + 1) * tk, :] = jnp.exp2(scores(c0 + u)).astype(BF16)
            acc_ref[...] += _dot(vt_ref[0, 0, :, chunk(c0, n)], p_ref[0:n * tk, :])

        per_trip = p_ref.shape[0] // tk

        def trip(j, carry):
            accumulate(per_trip * j, per_trip)
            return carry

        n_trips, rem = n_chunks // per_trip, n_chunks % per_trip
        if n_trips:
            lax.fori_loop(0, n_trips, trip, 0)
        if rem:
            accumulate(n_trips * per_trip, rem)

    @pl.when(bound_ref[1] <= 0.5)
    def _():
        _attn_online_softmax(scores, chunk, vt_ref, m_ref, acc_ref, sa_ref, sb_ref, p_ref,
                             n_chunks=n_chunks, unroll=unroll)

    acc = acc_ref[...]
    ot = (acc[0:HEAD_DIM] / acc[HEAD_DIM:HEAD_DIM + 1]).astype(BF16)
    for s in range(n_sub):
        out = None
        for g in range(KV_GROUP):
            col = (s * KV_GROUP + g) * ts
            part = _dot_tn(ot[:, col:col + ts], place_ref[g])
            out = part if out is None else out + part
        o_ref[0, s * ts:(s + 1) * ts, :] = out.astype(o_ref.dtype)


def _attn_online_softmax(scores, chunk, vt_ref, m_ref, acc_ref, sa_ref, sb_ref, p_ref, *, n_chunks, unroll):
    m_ref[...] = jnp.full(m_ref.shape, -jnp.inf, F32)
    tk = sa_ref.shape[0]

    def update(c, s_ref):
        vtc = vt_ref[0, 0, :, chunk(c)]
        alphas = []
        for cb in range(s_ref.shape[1] // V7X_LANES):
            cols = slice(cb * V7X_LANES, (cb + 1) * V7X_LANES)
            st = s_ref[:, cols]
            m_prev = m_ref[:, cols]
            m_new = jnp.maximum(m_prev, jnp.max(st, axis=0, keepdims=True))
            alphas.append(jnp.exp2(m_prev - m_new))
            p_ref[0:tk, cols] = jnp.exp2(st - m_new).astype(BF16)
            m_ref[:, cols] = m_new
        alpha = jnp.concatenate(alphas, axis=1)
        acc_ref[...] = alpha * acc_ref[...] + _dot(vtc, p_ref[0:tk, :])

    slots = (sa_ref, sb_ref)
    sa_ref[...] = scores(0)

    def body(j, carry):
        c0 = unroll * j
        for u in range(unroll):
            slots[(u + 1) % 2][...] = scores(c0 + u + 1)
            update(c0 + u, slots[u % 2])
        return carry

    if n_chunks > 1:
        lax.fori_loop(0, (n_chunks - 1) // unroll, body, 0)
    update(n_chunks - 1, slots[0])


def _attention_call(bound, q, k, vt, *, row0, n_rows, n_keys, n_sub, name):
    b, _, nq = q.shape
    ts = ROW_TILE
    tk = ROW_TILE
    tq = n_sub * ts
    n_chunks = n_keys // tk
    assert n_rows % tq == 0 and row0 % ts == 0 and n_keys % tk == 0 and n_chunks % 2 == 1
    unroll = max(u for u in (2, 4) if (n_chunks - 1) % u == 0)
    selq = _head_select(KV_GROUP, V7X_LANES)
    place = _head_select(KV_GROUP, HEAD_DIM)
    kern = functools.partial(_attn_kernel, n_sub=n_sub, tk=tk, n_chunks=n_chunks, unroll=unroll)
    width = n_sub * KV_GROUP * ts

    def q_spec(s):
        return pl.BlockSpec((1, ts, GROUP_LANES), lambda i, j, t: (i, row0 // ts + n_sub * t + s, j))

    return pl.pallas_call(
        kern,
        out_shape=jax.ShapeDtypeStruct((b, n_rows, nq), BF16),
        grid=(b, N_KV_HEADS, n_rows // tq),
        in_specs=[pl.BlockSpec(memory_space=pltpu.SMEM)] + [q_spec(s) for s in range(n_sub)] + [
            pl.BlockSpec((1, 1, n_keys, V7X_LANES), lambda i, j, t: (i, j, 0, 0)),
            pl.BlockSpec((1, 1, ATTN_V_ROWS, n_keys), lambda i, j, t: (i, j, 0, 0)),
            _const_spec(selq.shape),
            _const_spec(place.shape),
        ],
        out_specs=pl.BlockSpec((1, tq, GROUP_LANES), lambda i, j, t: (i, t, j)),
        scratch_shapes=[
            pltpu.VMEM((V7X_LANES, width), BF16),
            pltpu.VMEM((1, width), F32),
            pltpu.VMEM((ATTN_V_ROWS, width), F32),
            pltpu.VMEM((tk, width), F32),
            pltpu.VMEM((tk, width), F32),
            pltpu.VMEM((min(ATTN_CHUNKS_PER_PV, n_chunks) * tk, width), BF16),
        ],
        compiler_params=_cparams(3),
        name=name,
    )(bound, *([q] * n_sub), k, vt, jnp.asarray(selq, BF16), jnp.asarray(place, BF16))


def _score_bound(q_gain, k_gain):
    bound = (HEAD_DIM * (HEAD_DIM ** -0.5 * LOG2_E) * 1.02) * jnp.max(jnp.abs(q_gain)) * jnp.max(jnp.abs(k_gain))
    bound = bound.astype(BF16).astype(F32)
    return jnp.stack([-bound, (2.0 * bound <= EXP2_SAFE_RANGE).astype(F32)])


def _attention(q, k, vt, bound, ctx_len):
    nt = q.shape[1]
    n_lat = nt - ctx_len
    n_sub = max(s for s in (1, 2, 4) if n_lat % (s * ROW_TILE) == 0)
    o_ctx = _attention_call(bound, q, k, vt, row0=0, n_rows=ctx_len, n_keys=ctx_len, n_sub=1,
                            name="flash_attention_ctx")
    o_lat = _attention_call(bound, q, k, vt, row0=ctx_len, n_rows=n_lat, n_keys=nt, n_sub=n_sub,
                            name="flash_attention")
    return o_ctx, o_lat


def _out_ffn_kernel(*refs, final_norm, read_mix, n_mix, read_x, n_x):
    y = read_mix(refs[:n_mix])
    x = read_x(refs[n_mix:n_mix + n_x])
    mod_ref, g_ref, wo_ref, wg_ref, wu_ref, wd_ref, fg_ref, o_ref = refs[n_mix + n_x:]
    mod = mod_ref[0, 0]
    x1 = x + mod[2:3] * _dot(y, wo_ref[...])
    h2 = _norm_mod(x1, g_ref[...], mod[3:4], mod[4:5]).astype(BF16)
    a = _dot(h2, wg_ref[...])
    u = _dot(h2, wu_ref[...])
    hid = (a * jax.nn.sigmoid(a) * u).astype(BF16)
    x2 = x1 + mod[5:6] * _dot(hid, wd_ref[...])
    if final_norm:
        ms = jnp.mean(x2 * x2, axis=-1, keepdims=True)
        x2 = x2 * lax.rsqrt(ms + NORM_EPS) * fg_ref[...]
    o_ref[0] = x2


def _attention_mixer(o_ctx, o_lat):
    d = o_lat.shape[2]
    tm = ROW_TILE
    ctx_tiles = o_ctx.shape[1] // tm

    def read(refs):
        yc_ref, yl_ref = refs
        return jnp.where(pl.program_id(1) < ctx_tiles, yc_ref[0], yl_ref[0])

    specs = [pl.BlockSpec((1, tm, d), lambda i, t: (i, jnp.minimum(t, ctx_tiles - 1), 0)),
             pl.BlockSpec((1, tm, d), lambda i, t: (i, jnp.maximum(t - ctx_tiles, 0), 0))]
    return _Rows(read, specs, [o_ctx, o_lat], (o_lat.shape[0], o_ctx.shape[1] + o_lat.shape[1], d))


def _out_ffn(mix, rows, mods, g2, wo, wg, wu, wd, final_g, *, first_tile, ctx_tiles, final_norm):
    b, n, d = rows.shape
    assert mix.shape == rows.shape
    f = wg.shape[1]
    tm = ROW_TILE
    kern = functools.partial(_out_ffn_kernel, final_norm=final_norm, read_mix=mix.read,
                             n_mix=len(mix.specs), read_x=rows.read, n_x=len(rows.specs))

    def mod_sel(i, t):
        return (i, jnp.where(t + first_tile < ctx_tiles, 0, 1), 0, 0)

    return pl.pallas_call(
        kern,
        out_shape=jax.ShapeDtypeStruct((b, n, d), F32),
        grid=(b, n // tm),
        in_specs=mix.specs + rows.specs + [
            pl.BlockSpec((1, 1, N_MOD_ROWS, d), mod_sel),
            _const_spec((1, d)),
            _const_spec((d, d)),
            _const_spec((d, f)),
            _const_spec((d, f)),
            _const_spec((f, d)),
            _const_spec((1, d)),
        ],
        out_specs=pl.BlockSpec((1, tm, d), lambda i, t: (i, t, 0)),
        compiler_params=_cparams(2),
        name="out_ffn_final" if final_norm else "out_ffn",
    )(*mix.args, *rows.args, mods, g2.reshape(1, d), wo.astype(BF16), wg.astype(BF16), wu.astype(BF16),
      wd.astype(BF16), final_g.reshape(1, d))


def _rwkv_proj_kernel(x_ref, xp_ref, xn_ref, mod_ref, g_ref, mix_ref, wr_ref, wk_ref, wv_ref,
                      w1_ref, w2_ref, a1_ref, a2_ref, g1_ref, g2_ref, w0_ref, a0_ref, kk_ref, ka_ref,
                      seg_ref,
                      r_ref, v_ref, kkn_ref, gate_ref, lw0_ref, lw1_ref, kd0_ref, kd1_ref,
                      b0_ref, b1_ref, *, tm, seq_starts, seq_ends):
    t = pl.program_id(1)
    mod = mod_ref[0, 0]
    g = g_ref[...]
    h = _norm_mod(x_ref[0], g, mod[0:1], mod[1:2])
    h_prev = _norm_mod(xp_ref[0], g, mod[0:1], mod[1:2])[7:8]
    h_next = _norm_mod(xn_ref[0], g, mod[0:1], mod[1:2])[0:1]
    row = lax.broadcasted_iota(jnp.int32, h.shape, 0)
    pos = row + t * tm
    at_start = functools.reduce(jnp.logical_or, [pos == s for s in seq_starts])
    at_end = functools.reduce(jnp.logical_or, [pos == e for e in seq_ends])
    before = jnp.where(row == 0, h_prev, pltpu.roll(h, 1, 0))
    after = jnp.where(row == tm - 1, h_next, pltpu.roll(h, tm - 1, 0))
    before = jnp.where(at_start, 0.0, before)
    after = jnp.where(at_end, 0.0, after)
    xx = 0.5 * (before + after) - h
    mix = mix_ref[...]

    def lerp(j):
        return (h + xx * mix[j:j + 1]).astype(BF16)

    r_ref[0] = _dot(lerp(0), wr_ref[...]).astype(r_ref.dtype)
    k = _dot(lerp(2), wk_ref[...])
    v_ref[0] = _dot(lerp(3), wv_ref[...]).astype(v_ref.dtype)
    kkr = k * kk_ref[...]
    ss = _group_sum(kkr * kkr, seg_ref)
    kkn = kkr * lax.rsqrt(jnp.maximum(ss, 1e-24))
    kkn_ref[0] = kkn.astype(kkn_ref.dtype)
    xw, xa = lerp(1), lerp(4)
    ka = ka_ref[...]
    for d, (lw_ref, kd_ref, b_ref) in enumerate(((lw0_ref, kd0_ref, b0_ref), (lw1_ref, kd1_ref, b1_ref))):
        z = w0_ref[d:d + 1] + _dot(jnp.tanh(_dot(xw, w1_ref[d])).astype(BF16), w2_ref[d])
        lw_ref[0] = -DECAY_SCALE * jax.nn.sigmoid(z)
        a = jax.nn.sigmoid(a0_ref[d:d + 1] + _dot(_dot(xa, a1_ref[d]).astype(BF16), a2_ref[d]))
        kd_ref[0] = (k * (1.0 + (a - 1.0) * ka)).astype(kd_ref.dtype)
        b_ref[0] = (kkn * a).astype(b_ref.dtype)
    gate = _dot(jax.nn.sigmoid(_dot(lerp(5), g1_ref[...])).astype(BF16), g2_ref[...])
    gate_ref[0] = gate.astype(gate_ref.dtype)


def _rwkv_project(xs, mods, g1n, mix, w_rkv, w0, w1, w2, a0, a1, a2, gw1, gw2, k_k, k_a, seg,
                  ctx_len):
    b, nt, d = xs.shape
    tm = ROW_TILE
    halo = 8
    per = tm // halo
    n_halo = nt // halo
    kern = functools.partial(_rwkv_proj_kernel, tm=tm, seq_starts=(0, ctx_len),
                             seq_ends=(ctx_len - 1, nt - 1))
    tok = pl.BlockSpec((1, tm, d), lambda i, t: (i, t, 0))
    bf = lambda w: w.astype(BF16)
    outs = pl.pallas_call(
        kern,
        out_shape=tuple(jax.ShapeDtypeStruct((b, nt, d), dt)
                        for dt in (BF16, BF16, BF16, BF16, F32, F32, BF16, BF16, BF16, BF16)),
        grid=(b, nt // tm),
        in_specs=[
            tok,
            pl.BlockSpec((1, halo, d), lambda i, t: (i, jnp.maximum(t * per - 1, 0), 0)),
            pl.BlockSpec((1, halo, d), lambda i, t: (i, jnp.minimum((t + 1) * per, n_halo - 1), 0)),
            pl.BlockSpec((1, 1, N_MOD_ROWS, d), lambda i, t: (i, jnp.where(t * tm < ctx_len, 0, 1), 0, 0)),
            _const_spec((1, d)),
            _const_spec((N_MOD_ROWS, d)),
            _const_spec((d, d)), _const_spec((d, d)), _const_spec((d, d)),
            _const_spec(w1.shape), _const_spec(w2.shape), _const_spec(a1.shape), _const_spec(a2.shape),
            _const_spec(gw1.shape), _const_spec(gw2.shape),
            _const_spec((2, d)), _const_spec((2, d)), _const_spec((1, d)), _const_spec((1, d)),
            _const_spec((GROUP_LANES, GROUP_LANES)),
        ],
        out_specs=tuple(tok for _ in range(10)),
        compiler_params=_cparams(2),
        name="rwkv_project",
    )(xs, xs, xs, mods, g1n.reshape(1, d),
      jnp.concatenate([mix, jnp.zeros((N_MOD_ROWS - mix.shape[0], d), F32)], axis=0),
      bf(w_rkv[0]), bf(w_rkv[1]), bf(w_rkv[2]), bf(w1), bf(w2), bf(a1), bf(a2), bf(gw1), bf(gw2),
      w0, a0, k_k.reshape(1, d), k_a.reshape(1, d), seg)
    return outs


def _block_diag(x_bf, mask):
    return jnp.where(mask, jnp.concatenate([x_bf] * HEADS_PER_GROUP, axis=0), jnp.zeros((), BF16))


def _cumsum_rows(x, reverse):
    n = x.shape[0]
    row = lax.broadcasted_iota(jnp.int32, x.shape, 0)
    shift = 1
    while shift < n:
        if reverse:
            x = x + jnp.where(row < n - shift, pltpu.roll(x, n - shift, 0), 0.0)
        else:
            x = x + jnp.where(row >= shift, pltpu.roll(x, shift, 0), 0.0)
        shift *= 2
    return x


def _diag_blocks(full):
    lane_head = lax.broadcasted_iota(jnp.int32, (RWKV_HEAD, GROUP_LANES), 1) // RWKV_HEAD
    out = full[0:RWKV_HEAD]
    for j in range(1, HEADS_PER_GROUP):
        out = jnp.where(lane_head == j, full[j * RWKV_HEAD:(j + 1) * RWKV_HEAD], out)
    return out


def _wkv_masks(reverse):
    c = WKV_CHUNK
    rows = lax.broadcasted_iota(jnp.int32, (c, GROUP_LANES), 0)
    cols = lax.broadcasted_iota(jnp.int32, (c, GROUP_LANES), 1) % RWKV_HEAD
    tri_r = lax.broadcasted_iota(jnp.int32, (c, c), 0)
    tri_c = lax.broadcasted_iota(jnp.int32, (c, c), 1)
    if reverse:
        return cols > rows, cols >= rows, (tri_c >= tri_r).astype(BF16)
    return cols < rows, cols <= rows, (tri_c <= tri_r).astype(BF16)


def _wkv_chains(chains):
    c = WKV_CHUNK
    n = len(chains)
    rng = range(n)
    rows = lax.broadcasted_iota(jnp.int32, (c, GROUP_LANES), 0)
    cols = lax.broadcasted_iota(jnp.int32, (c, GROUP_LANES), 1) % RWKV_HEAD
    eye = (cols == rows).astype(F32)
    bd_r = lax.broadcasted_iota(jnp.int32, (GROUP_LANES, GROUP_LANES), 0) // RWKV_HEAD
    bd_c = lax.broadcasted_iota(jnp.int32, (GROUP_LANES, GROUP_LANES), 1) // RWKV_HEAD
    bmask = bd_r == bd_c
    masks = {rev: _wkv_masks(rev) for rev in sorted({ch[7] for ch in chains})}
    strict = [masks[ch[7]][0] for ch in chains]
    incl = [masks[ch[7]][1] for ch in chains]
    tri = [masks[ch[7]][2] for ch in chains]
    lw, kd, bb, kk, v, r, s0 = ([ch[i] for ch in chains] for i in range(7))
    bd = lambda x: _block_diag(x.astype(BF16), bmask)

    cs = [_cumsum_rows(lw[i], chains[i][7]) for i in rng]
    total = [cs[i][0:1] if chains[i][7] else cs[i][c - 1:c] for i in rng]
    a_s = [-kk[i] * jnp.exp(cs[i] - lw[i]) for i in rng]
    g_inv = [jnp.exp(-cs[i]) for i in rng]
    b_s = [bb[i] * g_inv[i] for i in rng]
    k_s = [kd[i] * g_inv[i] for i in rng]
    r_s = [r[i] * jnp.exp(cs[i]) for i in rng]
    g_rest = [jnp.exp(total[i] - cs[i]) for i in rng]
    b_e = [(bb[i] * g_rest[i]).astype(BF16) for i in rng]
    k_e = [(kd[i] * g_rest[i]).astype(BF16) for i in rng]
    g_end = [jnp.exp(total[i]) for i in rng]

    lhs = [jnp.concatenate([a_s[i], r_s[i]], axis=0).astype(BF16) for i in rng]
    pb = [_dot_nt(lhs[i], bd(b_s[i])) for i in rng]
    pk = [_dot_nt(lhs[i], bd(k_s[i])) for i in rng]
    l_ab = [jnp.where(strict[i], pb[i][:c], 0.0) for i in rng]
    m_rb = [jnp.where(incl[i], pb[i][c:], 0.0).astype(BF16) for i in rng]
    l_ak = [jnp.where(strict[i], pk[i][:c], 0.0) for i in rng]
    m_rk = [jnp.where(incl[i], pk[i][c:], 0.0) for i in rng]

    p = l_ab
    tmat = [eye + p[i] for i in rng]
    p = [_dot(p[i].astype(BF16), bd(p[i])) for i in rng]
    for _ in range(4):
        z = [_dot(jnp.concatenate([tmat[i], p[i]], axis=0).astype(BF16), bd(p[i])) for i in rng]
        tmat = [tmat[i] + z[i][:c] for i in rng]
        p = [z[i][c:] for i in rng]
    tmat = [tmat[i] + _dot(tmat[i].astype(BF16), bd(p[i])) for i in rng]
    t_bf = [tmat[i].astype(BF16) for i in rng]

    zv = [_dot(jnp.concatenate([l_ak[i], m_rk[i]], axis=0).astype(BF16), bd(v[i])) for i in rng]
    a_hat = [_dot(t_bf[i], bd(a_s[i])) for i in rng]
    u_til = [_dot(t_bf[i], bd(zv[i][:c])) for i in rng]
    r_hat = [r_s[i] + _dot(m_rb[i], bd(a_hat[i])) for i in rng]
    y_til = [zv[i][c:] + _dot(m_rb[i], bd(u_til[i])) for i in rng]

    full = [_dot_tn(jnp.concatenate([a_hat[i], u_til[i]], axis=1).astype(BF16), b_e[i]) for i in rng]
    g_mat = [_diag_blocks(full[i][:GROUP_LANES]) for i in rng]
    h_mat = [_diag_blocks(full[i][GROUP_LANES:] + _dot_tn(v[i].astype(BF16), k_e[i])) for i in rng]

    y = [y_til[i] + _dot_nt(r_hat[i].astype(BF16), bd(s0[i])) for i in rng]
    s1 = [s0[i] * g_end[i] + _dot(s0[i].astype(BF16), bd(g_mat[i])) + h_mat[i] for i in rng]
    return y, s1


def _wkv_kernel(lwf, kdf, bf_, kkf, vf, rf, lwr, kdr, br, kkr, vr, rr, yf_ref, yr_ref, s_ref):
    @pl.when(pl.program_id(1) == 0)
    def _():
        s_ref[...] = jnp.zeros(s_ref.shape, F32)

    n_groups = lwf.shape[2] // GROUP_LANES
    chains, dests = [], []
    for j in range(n_groups):
        sl = slice(j * GROUP_LANES, (j + 1) * GROUP_LANES)
        for d, (refs, y_ref) in enumerate((((lwf, kdf, bf_, kkf, vf, rf), yf_ref),
                                           ((lwr, kdr, br, kkr, vr, rr), yr_ref))):
            lw, kd, bb, kk, v, r = (ref[0, :, sl] for ref in refs)
            chains.append((lw, kd.astype(F32), bb.astype(F32), kk.astype(F32), v, r.astype(F32),
                           s_ref[d, :, sl], d == 1))
            dests.append((y_ref, d, sl))
    ys, s1s = _wkv_chains(chains)
    for (y_ref, d, sl), y, s1 in zip(dests, ys, s1s):
        y_ref[0, :, sl] = y.astype(y_ref.dtype)
        s_ref[d, :, sl] = s1


def _wkv_scan(lw0, lw1, kd0, kd1, b0, b1, kk, v, r, ctx_len):
    b, nt, d = v.shape
    c = WKV_CHUNK
    n_steps = nt // c
    ctx_chunks = ctx_len // c

    def fwd(i, s):
        return (i, s, 0)

    def rev(i, s):
        return (i, jnp.where(s < ctx_chunks, ctx_chunks - 1 - s, n_steps - 1 + ctx_chunks - s), 0)

    blk_f = pl.BlockSpec((1, c, d), fwd)
    blk_r = pl.BlockSpec((1, c, d), rev)
    return pl.pallas_call(
        _wkv_kernel,
        out_shape=(jax.ShapeDtypeStruct((b, nt, d), BF16), jax.ShapeDtypeStruct((b, nt, d), BF16)),
        grid=(b, n_steps),
        in_specs=[blk_f] * 6 + [blk_r] * 6,
        out_specs=(blk_f, blk_r),
        scratch_shapes=[pltpu.VMEM((2, RWKV_HEAD, d), F32)],
        compiler_params=_cparams(2),
        name="wkv_scan",
    )(lw0, kd0, b0, kk, v, r, lw1, kd1, b1, kk, v, r)


def _read_rwkv_mix(refs):
    yf_ref, yr_ref, r_ref, v_ref, kd0_ref, kd1_ref, gate_ref, rk_ref, lg_ref, lb_ref, seg_ref = refs
    wkv = yf_ref[0].astype(F32) + yr_ref[0].astype(F32)
    inv_n = 1.0 / RWKV_HEAD
    mu = _group_sum(wkv, seg_ref, two_pass=True) * inv_n
    dev = wkv - mu
    var = _group_sum(dev * dev, seg_ref) * inv_n
    gn = dev * lax.rsqrt(var + GN_EPS) * lg_ref[...] + lb_ref[...]
    rk = r_ref[0].astype(F32) * rk_ref[...]
    kd_sum = kd0_ref[0].astype(F32) + kd1_ref[0].astype(F32)
    bonus = _group_sum(rk * kd_sum, seg_ref) * v_ref[0].astype(F32)
    return ((gn + bonus) * gate_ref[0].astype(F32)).astype(BF16)


def _rwkv_mixer(yf, yr, r, v, kd0, kd1, gate, r_k, ln_g, ln_b, seg, ctx_len):
    b, nt, d = v.shape
    tm = ROW_TILE
    off = ctx_len // tm
    tok = pl.BlockSpec((1, tm, d), lambda i, t: (i, t + off, 0))
    specs = [tok] * 7 + [_const_spec((1, d))] * 3 + [_const_spec((GROUP_LANES, GROUP_LANES))]
    args = [yf, yr, r, v, kd0, kd1, gate, r_k.reshape(1, d), ln_g.reshape(1, d), ln_b.reshape(1, d), seg]
    return _Rows(_read_rwkv_mix, specs, args, (b, nt - ctx_len, d))


def _rope_tables(ctx_len, seq_len):
    f32 = np.float32
    rows = seq_len // GRID_W
    row = np.repeat(np.arange(rows, dtype=f32), GRID_W)
    col = np.tile(np.arange(GRID_W, dtype=f32), rows)
    inv_freq = np.power(f32(ROPE_THETA), -np.arange(ROPE_PAIRS, dtype=f32) / f32(ROPE_PAIRS)).astype(f32)
    row_ang, col_ang = row[:, None] * inv_freq, col[:, None] * inv_freq
    ang = np.concatenate([row_ang, row_ang, col_ang, col_ang], axis=1).astype(np.float64)
    cos = np.concatenate([np.ones((ctx_len, HEAD_DIM)), np.cos(ang)], axis=0)
    sin = np.concatenate([np.zeros((ctx_len, HEAD_DIM)), np.sin(ang)], axis=0)
    sign = np.concatenate([-np.ones(ROPE_PAIRS), np.ones(ROPE_PAIRS)] * 2)
    reps = V7X_LANES // HEAD_DIM
    return (jnp.asarray(np.tile(cos, (1, reps)), F32), jnp.asarray(np.tile(sin * sign, (1, reps)), F32))


def kernel(x, c, ctx, c_ctx, mod_w, mod_b, norm1_g, norm2_g, ffn_wg, ffn_wu, ffn_wd, attn_wqkv,
           attn_q_gain, attn_k_gain, attn_wo, rwkv_mix, rwkv_wrkv, rwkv_w0, rwkv_w1, rwkv_w2, rwkv_a0,
           rwkv_a1, rwkv_a2, rwkv_g1, rwkv_g2, rwkv_k_k, rwkv_k_a, rwkv_r_k, rwkv_ln_g, rwkv_ln_b,
           rwkv_wo, final_g):
    b, seq_len, d = x.shape
    ctx_len = ctx.shape[1]
    depth = mod_w.shape[0]
    assert depth == 2 and d == N_HEADS * HEAD_DIM
    assert ctx_len % ROW_TILE == 0 and seq_len % ROW_TILE == 0 and seq_len % GRID_W == 0
    ctx_tiles = ctx_len // ROW_TILE

    n_rows = -(-(b + 1) // 8) * 8
    c_rows = jnp.concatenate([c, c_ctx[None], jnp.zeros((n_rows - b - 1, d), F32)], axis=0)
    m_all = _modulation(c_rows, mod_w, mod_b).reshape(depth, n_rows, 6, d)
    pad = jnp.zeros((depth, b, N_MOD_ROWS - 6, d), F32)
    lat = jnp.concatenate([m_all[:, :b], pad], axis=2)
    con = jnp.concatenate([jnp.broadcast_to(m_all[:, b:b + 1], (depth, b, 6, d)), pad], axis=2)
    mods = jnp.stack([con, lat], axis=2)

    cos2, sin2 = _rope_tables(ctx_len, seq_len)
    seg = jnp.asarray(np.kron(np.eye(HEADS_PER_GROUP), np.ones((HEAD_DIM, HEAD_DIM))), BF16)

    stream = _stream_of(ctx, x)

    q, k, vt = _qkv_project(stream, mods[0], norm1_g[0], attn_wqkv[0], attn_q_gain[0], attn_k_gain[0],
                            cos2, sin2, seg, ctx_tiles)
    o_ctx, o_lat = _attention(q, k, vt, _score_bound(attn_q_gain[0], attn_k_gain[0]), ctx_len)
    xs = _out_ffn(_attention_mixer(o_ctx, o_lat), stream, mods[0], norm2_g[0], attn_wo[0], ffn_wg[0],
                  ffn_wu[0], ffn_wd[0], final_g, first_tile=0, ctx_tiles=ctx_tiles, final_norm=False)

    r, v, kk, gate, lw0, lw1, kd0, kd1, b0, b1 = _rwkv_project(
        xs, mods[1], norm1_g[1], rwkv_mix[0], rwkv_wrkv[0], rwkv_w0[0], rwkv_w1[0], rwkv_w2[0],
        rwkv_a0[0], rwkv_a1[0], rwkv_a2[0], rwkv_g1[0], rwkv_g2[0], rwkv_k_k[0], rwkv_k_a[0], seg,
        ctx_len)
    yf, yr = _wkv_scan(lw0, lw1, kd0, kd1, b0, b1, kk, v, r, ctx_len)
    mix = _rwkv_mixer(yf, yr, r, v, kd0, kd1, gate, rwkv_r_k[0].reshape(-1), rwkv_ln_g[0],
                      rwkv_ln_b[0], seg, ctx_len)
    return _out_ffn(mix, _rows_from(xs, ctx_tiles), mods[1], norm2_g[1], rwkv_wo[0], ffn_wg[1], ffn_wu[1],
                    ffn_wd[1], final_g, first_tile=ctx_tiles, ctx_tiles=ctx_tiles, final_norm=True)
```

```python
import functools
from typing import Callable, NamedTuple

import jax
import jax.numpy as jnp
import numpy as np
from jax import lax
from jax.experimental import pallas as pl
from jax.experimental.pallas import tpu as pltpu

F32 = jnp.float32
BF16 = jnp.bfloat16

NORM_EPS = 1e-6
GN_EPS = 64e-5
GRID_W = 64
N_HEADS = 16
N_KV_HEADS = 4
KV_GROUP = N_HEADS // N_KV_HEADS
HEAD_DIM = 64
ROPE_THETA = 10000.0
ROPE_PAIRS = HEAD_DIM // 4
RWKV_HEAD = 64
DECAY_SCALE = float(np.exp(-0.5))
LOG2_E = float(np.log2(np.e))
EXP2_SAFE_RANGE = 120.0

V7X_LANES = 128
V7X_MXU_DIM = 256
V7X_VMEM_LIMIT_BYTES = 60000 * 1024

ROW_TILE = 256
GROUP_LANES = V7X_MXU_DIM
HEADS_PER_GROUP = GROUP_LANES // HEAD_DIM
BF16_SUBLANES = 16
ATTN_V_ROWS = HEAD_DIM + BF16_SUBLANES
ATTN_CHUNKS_PER_PV = 11
WKV_CHUNK = 64
WKV_CHUNKS_PER_STEP = 1
N_MOD_ROWS = 8


def _cparams(n_axes):
    return pltpu.CompilerParams(
        dimension_semantics=("arbitrary",) * n_axes,
        vmem_limit_bytes=V7X_VMEM_LIMIT_BYTES,
    )


def _const_spec(shape):
    nd = len(shape)
    return pl.BlockSpec(shape, lambda *_: (0,) * nd, pipeline_mode=pl.Buffered(1))


def _dot(a, b):
    return jnp.dot(a, b, preferred_element_type=F32)


def _dot_nt(a, b):
    return lax.dot_general(a, b, (((1,), (1,)), ((), ())), preferred_element_type=F32)


def _norm_mod(x, g, shift, scale):
    ms = jnp.mean(x * x, axis=-1, keepdims=True)
    return (x * lax.rsqrt(ms + NORM_EPS) * g) * (1.0 + scale) + shift


def _group_sum(x, seg_ref, two_pass=False):
    seg = seg_ref[...]
    outs = []
    for j in range(x.shape[1] // GROUP_LANES):
        xs = x[:, j * GROUP_LANES:(j + 1) * GROUP_LANES]
        hi = xs.astype(BF16)
        out = _dot(hi, seg)
        if two_pass:
            out = out + _dot((xs - hi.astype(F32)).astype(BF16), seg)
        outs.append(out)
    return outs[0] if len(outs) == 1 else jnp.concatenate(outs, axis=1)


def _mod_kernel(c_ref, w_ref, b_ref, o_ref):
    c = c_ref[...]
    s = c * jax.nn.sigmoid(c)
    o_ref[0] = _dot(s.astype(BF16), w_ref[0]) + b_ref[0]


def _modulation(c_rows, mod_w, mod_b):
    depth, d, n = mod_w.shape
    rows = c_rows.shape[0]
    tn = n // 4
    return pl.pallas_call(
        _mod_kernel,
        out_shape=jax.ShapeDtypeStruct((depth, rows, n), F32),
        grid=(depth, n // tn),
        in_specs=[
            pl.BlockSpec((rows, d), lambda i, j: (0, 0)),
            pl.BlockSpec((1, d, tn), lambda i, j: (i, 0, j)),
            pl.BlockSpec((1, 1, tn), lambda i, j: (i, 0, j)),
        ],
        out_specs=pl.BlockSpec((1, rows, tn), lambda i, j: (i, 0, j)),
        compiler_params=_cparams(2),
        name="modulation",
    )(c_rows, mod_w.astype(BF16), mod_b.reshape(depth, 1, n))


def _rope(x, cos, sin_signed):
    w = x.shape[1]
    lane = lax.broadcasted_iota(jnp.int32, x.shape, 1)
    first_half = (lane % (2 * ROPE_PAIRS)) < ROPE_PAIRS
    partner = jnp.where(first_half, pltpu.roll(x, w - ROPE_PAIRS, 1), pltpu.roll(x, ROPE_PAIRS, 1))
    return x * cos + partner * sin_signed


class _Rows(NamedTuple):
    read: Callable
    specs: list
    args: list
    shape: tuple


def _stream_of(ctx, x):
    b, c, d = ctx.shape
    tm = ROW_TILE
    ctx_tiles = c // tm

    def read(refs):
        c_ref, x_ref = refs
        return jnp.where(pl.program_id(1) < ctx_tiles, c_ref[0], x_ref[0])

    specs = [pl.BlockSpec((1, tm, d), lambda i, t: (i, jnp.minimum(t, ctx_tiles - 1), 0)),
             pl.BlockSpec((1, tm, d), lambda i, t: (i, jnp.maximum(t - ctx_tiles, 0), 0))]
    return _Rows(read, specs, [ctx, x], (b, c + x.shape[1], d))


def _rows_from(xs, tile_offset):
    b, n, d = xs.shape
    spec = pl.BlockSpec((1, ROW_TILE, d), lambda i, t: (i, t + tile_offset, 0))
    return _Rows(lambda refs: refs[0][0], [spec], [xs], (b, n - tile_offset * ROW_TILE, d))


def _qkv_kernel(*refs, nq, nk, read_x, n_x):
    x = read_x(refs[:n_x])
    (mod_ref, g_ref, w_ref, qg_ref, kg_ref, cos_ref, sin_ref, seg_ref, ek_ref, ev_ref,
     q_ref, k_ref, v_ref) = refs[n_x:]
    mod = mod_ref[0, 0]
    h = _norm_mod(x, g_ref[...], mod[0:1], mod[1:2]).astype(BF16)
    qkv = _dot(h, w_ref[...])
    q, k, v = qkv[:, :nq], qkv[:, nq:nq + nk], qkv[:, nq + nk:]
    cos2, sin2 = cos_ref[...], sin_ref[...]

    def head_norm_rope(z, gain):
        reps = z.shape[1] // V7X_LANES
        cos = jnp.concatenate([cos2] * reps, axis=1)
        sin = jnp.concatenate([sin2] * reps, axis=1)
        ss = _group_sum(z * z, seg_ref)
        zn = z * lax.rsqrt(ss * (1.0 / HEAD_DIM) + NORM_EPS) * gain
        return _rope(zn, cos, sin)

    qn = head_norm_rope(q, qg_ref[...]) * (HEAD_DIM ** -0.5 * LOG2_E)
    q_ref[0] = qn.astype(BF16)
    kn = head_norm_rope(k, kg_ref[...]).astype(BF16)
    k_t = _dot(kn, ek_ref[...]).astype(BF16)
    v_bf = v.astype(BF16)
    ones = jnp.ones((ATTN_V_ROWS - HEAD_DIM, v_bf.shape[0]), BF16)
    shift_lane = lax.broadcasted_iota(jnp.int32, (v_bf.shape[0], V7X_LANES), 1) == HEAD_DIM
    for j in range(N_KV_HEADS):
        k_ref[0, j] = jnp.where(shift_lane, jnp.ones((), BF16), k_t[:, j * V7X_LANES:(j + 1) * V7X_LANES])
        v_ref[0, j, 0:HEAD_DIM, :] = _dot_nt(ev_ref[j], v_bf).astype(BF16)
        v_ref[0, j, HEAD_DIM:, :] = ones


def _head_select(n_heads, rows):
    sel = np.zeros((n_heads, rows, n_heads * HEAD_DIM), np.float32)
    for j in range(n_heads):
        sel[j, np.arange(HEAD_DIM), j * HEAD_DIM + np.arange(HEAD_DIM)] = 1.0
    return sel


def _qkv_project(rows, mods, g1, wqkv, q_gain, k_gain, cos2, sin2, seg, ctx_tiles):
    b, nt, d = rows.shape
    nq, nk = N_HEADS * HEAD_DIM, N_KV_HEADS * HEAD_DIM
    tm = ROW_TILE
    ek = np.zeros((nk, N_KV_HEADS * V7X_LANES), np.float32)
    for j in range(N_KV_HEADS):
        ek[j * HEAD_DIM + np.arange(HEAD_DIM), j * V7X_LANES + np.arange(HEAD_DIM)] = 1.0
    ev = _head_select(N_KV_HEADS, HEAD_DIM)
    kern = functools.partial(_qkv_kernel, nq=nq, nk=nk, read_x=rows.read, n_x=len(rows.specs))
    return pl.pallas_call(
        kern,
        out_shape=(
            jax.ShapeDtypeStruct((b, nt, nq), BF16),
            jax.ShapeDtypeStruct((b, N_KV_HEADS, nt, V7X_LANES), BF16),
            jax.ShapeDtypeStruct((b, N_KV_HEADS, ATTN_V_ROWS, nt), BF16),
        ),
        grid=(b, nt // tm),
        in_specs=rows.specs + [
            pl.BlockSpec((1, 1, N_MOD_ROWS, d), lambda i, t: (i, jnp.where(t < ctx_tiles, 0, 1), 0, 0)),
            _const_spec((1, d)),
            _const_spec((d, nq + 2 * nk)),
            _const_spec((1, nq)),
            _const_spec((1, nk)),
            pl.BlockSpec((tm, V7X_LANES), lambda i, t: (t, 0)),
            pl.BlockSpec((tm, V7X_LANES), lambda i, t: (t, 0)),
            _const_spec((GROUP_LANES, GROUP_LANES)),
            _const_spec(ek.shape),
            _const_spec(ev.shape),
        ],
        out_specs=(
            pl.BlockSpec((1, tm, nq), lambda i, t: (i, t, 0)),
            pl.BlockSpec((1, N_KV_HEADS, tm, V7X_LANES), lambda i, t: (i, 0, t, 0)),
            pl.BlockSpec((1, N_KV_HEADS, ATTN_V_ROWS, tm), lambda i, t: (i, 0, 0, t)),
        ),
        compiler_params=_cparams(2),
        name="qkv_project",
    )(*rows.args, mods, g1.reshape(1, d), wqkv.astype(BF16),
      jnp.tile(q_gain, N_HEADS).reshape(1, nq), jnp.tile(k_gain, N_KV_HEADS).reshape(1, nk),
      cos2, sin2, seg, jnp.asarray(ek, BF16), jnp.asarray(ev, BF16))


def _attn_kernel(*refs, n_sub, tk, n_chunks, unroll):
    bound_ref = refs[0]
    q_refs = refs[1:1 + n_sub]
    k_ref, vt_ref, o_ref, qt_ref, m_ref, acc_ref, sa_ref, sb_ref, p_ref = refs[1 + n_sub:]
    ts = q_refs[0].shape[1]
    width = qt_ref.shape[1]
    for s, q_ref in enumerate(q_refs):
        q_t = q_ref[0].astype(F32).T
        for g in range(KV_GROUP):
            col = (s * KV_GROUP + g) * ts
            qt_ref[0:HEAD_DIM, col:col + ts] = q_t[g * HEAD_DIM:(g + 1) * HEAD_DIM].astype(BF16)
    qt_ref[HEAD_DIM:, :] = jnp.full((qt_ref.shape[0] - HEAD_DIM, width), bound_ref[0], F32).astype(BF16)
    acc_ref[...] = jnp.zeros(acc_ref.shape, F32)

    def chunk(c, n=1):
        return pl.ds(c * tk if isinstance(c, int) else pl.multiple_of(c * tk, n * tk), n * tk)

    def scores(c):
        return _dot(k_ref[0, 0, chunk(c), :], qt_ref[...])

    @pl.when(bound_ref[1] > 0.5)
    def _():
        def accumulate(c0, n):
            for u in range(n):
                p_ref[u * tk:(u + 1) * tk, :] = jnp.exp2(scores(c0 + u)).astype(BF16)
            acc_ref[...] += _dot(vt_ref[0, 0, :, chunk(c0, n)], p_ref[0:n * tk, :])

        per_trip = p_ref.shape[0] // tk

        def trip(j, carry):
            accumulate(per_trip * j, per_trip)
            return carry

        n_trips, rem = n_chunks // per_trip, n_chunks % per_trip
        if n_trips:
            lax.fori_loop(0, n_trips, trip, 0)
        if rem:
            accumulate(n_trips * per_trip, rem)

    @pl.when(bound_ref[1] <= 0.5)
    def _():
        _attn_online_softmax(scores, chunk, vt_ref, m_ref, acc_ref, sa_ref, sb_ref, p_ref,
                             n_chunks=n_chunks, unroll=unroll)

    acc = acc_ref[...]
    ot = acc[0:HEAD_DIM] / acc[HEAD_DIM:HEAD_DIM + 1]
    for s in range(n_sub):
        heads = [ot[:, (s * KV_GROUP + g) * ts:(s * KV_GROUP + g + 1) * ts] for g in range(KV_GROUP)]
        o_ref[0, s * ts:(s + 1) * ts, :] = jnp.concatenate(heads, axis=0).T.astype(o_ref.dtype)


def _attn_online_softmax(scores, chunk, vt_ref, m_ref, acc_ref, sa_ref, sb_ref, p_ref, *, n_chunks, unroll):
    m_ref[...] = jnp.full(m_ref.shape, -jnp.inf, F32)
    tk = sa_ref.shape[0]

    def update(c, s_ref):
        vtc = vt_ref[0, 0, :, chunk(c)]
        alphas = []
        for cb in range(s_ref.shape[1] // V7X_LANES):
            cols = slice(cb * V7X_LANES, (cb + 1) * V7X_LANES)
            st = s_ref[:, cols]
            m_prev = m_ref[:, cols]
            m_new = jnp.maximum(m_prev, jnp.max(st, axis=0, keepdims=True))
            alphas.append(jnp.exp2(m_prev - m_new))
            p_ref[0:tk, cols] = jnp.exp2(st - m_new).astype(BF16)
            m_ref[:, cols] = m_new
        alpha = jnp.concatenate(alphas, axis=1)
        acc_ref[...] = alpha * acc_ref[...] + _dot(vtc, p_ref[0:tk, :])

    slots = (sa_ref, sb_ref)
    sa_ref[...] = scores(0)

    def body(j, carry):
        c0 = unroll * j
        for u in range(unroll):
            slots[(u + 1) % 2][...] = scores(c0 + u + 1)
            update(c0 + u, slots[u % 2])
        return carry

    if n_chunks > 1:
        lax.fori_loop(0, (n_chunks - 1) // unroll, body, 0)
    update(n_chunks - 1, slots[0])


def _attention_call(bound, q, k, vt, *, row0, n_rows, n_keys, n_sub, name):
    b, _, nq = q.shape
    ts = ROW_TILE
    tk = ROW_TILE
    tq = n_sub * ts
    n_chunks = n_keys // tk
    assert n_rows % tq == 0 and row0 % ts == 0 and n_keys % tk == 0 and n_chunks % 2 == 1
    unroll = max(u for u in (2, 4) if (n_chunks - 1) % u == 0)
    kern = functools.partial(_attn_kernel, n_sub=n_sub, tk=tk, n_chunks=n_chunks, unroll=unroll)
    width = n_sub * KV_GROUP * ts

    def q_spec(s):
        return pl.BlockSpec((1, ts, GROUP_LANES), lambda i, j, t: (i, row0 // ts + n_sub * t + s, j))

    return pl.pallas_call(
        kern,
        out_shape=jax.ShapeDtypeStruct((b, n_rows, nq), BF16),
        grid=(b, N_KV_HEADS, n_rows // tq),
        in_specs=[pl.BlockSpec(memory_space=pltpu.SMEM)] + [q_spec(s) for s in range(n_sub)] + [
            pl.BlockSpec((1, 1, n_keys, V7X_LANES), lambda i, j, t: (i, j, 0, 0)),
            pl.BlockSpec((1, 1, ATTN_V_ROWS, n_keys), lambda i, j, t: (i, j, 0, 0)),
        ],
        out_specs=pl.BlockSpec((1, tq, GROUP_LANES), lambda i, j, t: (i, t, j)),
        scratch_shapes=[
            pltpu.VMEM((V7X_LANES, width), BF16),
            pltpu.VMEM((1, width), F32),
            pltpu.VMEM((ATTN_V_ROWS, width), F32),
            pltpu.VMEM((tk, width), F32),
            pltpu.VMEM((tk, width), F32),
            pltpu.VMEM((min(ATTN_CHUNKS_PER_PV, n_chunks) * tk, width), BF16),
        ],
        compiler_params=_cparams(3),
        name=name,
    )(bound, *([q] * n_sub), k, vt)


def _score_bound(q_gain, k_gain):
    bound = (HEAD_DIM * (HEAD_DIM ** -0.5 * LOG2_E) * 1.02) * jnp.max(jnp.abs(q_gain)) * jnp.max(jnp.abs(k_gain))
    bound = bound.astype(BF16).astype(F32)
    return jnp.stack([-bound, (2.0 * bound <= EXP2_SAFE_RANGE).astype(F32)])


def _attention(q, k, vt, bound, ctx_len):
    nt = q.shape[1]
    n_lat = nt - ctx_len
    n_sub = max(s for s in (1, 2, 4) if n_lat % (s * ROW_TILE) == 0)
    o_ctx = _attention_call(bound, q, k, vt, row0=0, n_rows=ctx_len, n_keys=ctx_len, n_sub=1,
                            name="flash_attention_ctx")
    o_lat = _attention_call(bound, q, k, vt, row0=ctx_len, n_rows=n_lat, n_keys=nt, n_sub=n_sub,
                            name="flash_attention")
    return o_ctx, o_lat


def _out_ffn_kernel(*refs, final_norm, read_mix, n_mix, read_x, n_x):
    y = read_mix(refs[:n_mix])
    x = read_x(refs[n_mix:n_mix + n_x])
    mod_ref, g_ref, wo_ref, wg_ref, wu_ref, wd_ref, fg_ref, o_ref = refs[n_mix + n_x:]
    mod = mod_ref[0, 0]
    x1 = x + mod[2:3] * _dot(y, wo_ref[...])
    h2 = _norm_mod(x1, g_ref[...], mod[3:4], mod[4:5]).astype(BF16)
    a = _dot(h2, wg_ref[...])
    u = _dot(h2, wu_ref[...])
    hid = (a * jax.nn.sigmoid(a) * u).astype(BF16)
    x2 = x1 + mod[5:6] * _dot(hid, wd_ref[...])
    if final_norm:
        ms = jnp.mean(x2 * x2, axis=-1, keepdims=True)
        x2 = x2 * lax.rsqrt(ms + NORM_EPS) * fg_ref[...]
    o_ref[0] = x2


def _attention_mixer(o_ctx, o_lat):
    d = o_lat.shape[2]
    tm = ROW_TILE
    ctx_tiles = o_ctx.shape[1] // tm

    def read(refs):
        yc_ref, yl_ref = refs
        return jnp.where(pl.program_id(1) < ctx_tiles, yc_ref[0], yl_ref[0])

    specs = [pl.BlockSpec((1, tm, d), lambda i, t: (i, jnp.minimum(t, ctx_tiles - 1), 0)),
             pl.BlockSpec((1, tm, d), lambda i, t: (i, jnp.maximum(t - ctx_tiles, 0), 0))]
    return _Rows(read, specs, [o_ctx, o_lat], (o_lat.shape[0], o_ctx.shape[1] + o_lat.shape[1], d))


def _out_ffn(mix, rows, mods, g2, wo, wg, wu, wd, final_g, *, first_tile, ctx_tiles, final_norm):
    b, n, d = rows.shape
    assert mix.shape == rows.shape
    f = wg.shape[1]
    tm = ROW_TILE
    kern = functools.partial(_out_ffn_kernel, final_norm=final_norm, read_mix=mix.read,
                             n_mix=len(mix.specs), read_x=rows.read, n_x=len(rows.specs))

    def mod_sel(i, t):
        return (i, jnp.where(t + first_tile < ctx_tiles, 0, 1), 0, 0)

    return pl.pallas_call(
        kern,
        out_shape=jax.ShapeDtypeStruct((b, n, d), F32),
        grid=(b, n // tm),
        in_specs=mix.specs + rows.specs + [
            pl.BlockSpec((1, 1, N_MOD_ROWS, d), mod_sel),
            _const_spec((1, d)),
            _const_spec((d, d)),
            _const_spec((d, f)),
            _const_spec((d, f)),
            _const_spec((f, d)),
            _const_spec((1, d)),
        ],
        out_specs=pl.BlockSpec((1, tm, d), lambda i, t: (i, t, 0)),
        compiler_params=_cparams(2),
        name="out_ffn_final" if final_norm else "out_ffn",
    )(*mix.args, *rows.args, mods, g2.reshape(1, d), wo.astype(BF16), wg.astype(BF16), wu.astype(BF16),
      wd.astype(BF16), final_g.reshape(1, d))


def _rwkv_proj_kernel(x_ref, xp_ref, xn_ref, mod_ref, g_ref, mix_ref, wr_ref, wk_ref, wv_ref,
                      w1_ref, w2_ref, a1_ref, a2_ref, g1_ref, g2_ref, w0_ref, a0_ref, kk_ref, ka_ref,
                      seg_ref,
                      r_ref, v_ref, kkn_ref, gate_ref, lw0_ref, lw1_ref, kd0_ref, kd1_ref,
                      b0_ref, b1_ref, *, tm, seq_starts, seq_ends):
    t = pl.program_id(1)
    mod = mod_ref[0, 0]
    g = g_ref[...]
    h = _norm_mod(x_ref[0], g, mod[0:1], mod[1:2])
    h_prev = _norm_mod(xp_ref[0], g, mod[0:1], mod[1:2])[7:8]
    h_next = _norm_mod(xn_ref[0], g, mod[0:1], mod[1:2])[0:1]
    row = lax.broadcasted_iota(jnp.int32, h.shape, 0)
    pos = row + t * tm
    at_start = functools.reduce(jnp.logical_or, [pos == s for s in seq_starts])
    at_end = functools.reduce(jnp.logical_or, [pos == e for e in seq_ends])
    before = jnp.where(row == 0, h_prev, pltpu.roll(h, 1, 0))
    after = jnp.where(row == tm - 1, h_next, pltpu.roll(h, tm - 1, 0))
    before = jnp.where(at_start, 0.0, before)
    after = jnp.where(at_end, 0.0, after)
    xx = 0.5 * (before + after) - h
    mix = mix_ref[...]

    def lerp(j):
        return (h + xx * mix[j:j + 1]).astype(BF16)

    r_ref[0] = _dot(lerp(0), wr_ref[...]).astype(r_ref.dtype)
    k = _dot(lerp(2), wk_ref[...])
    v_ref[0] = _dot(lerp(3), wv_ref[...]).astype(v_ref.dtype)
    kkr = k * kk_ref[...]
    ss = _group_sum(kkr * kkr, seg_ref)
    kkn = kkr * lax.rsqrt(jnp.maximum(ss, 1e-24))
    kkn_ref[0] = kkn.astype(kkn_ref.dtype)
    xw, xa = lerp(1), lerp(4)
    ka = ka_ref[...]
    for d, (lw_ref, kd_ref, b_ref) in enumerate(((lw0_ref, kd0_ref, b0_ref), (lw1_ref, kd1_ref, b1_ref))):
        z = w0_ref[d:d + 1] + _dot(jnp.tanh(_dot(xw, w1_ref[d])).astype(BF16), w2_ref[d])
        lw_ref[0] = -DECAY_SCALE * jax.nn.sigmoid(z)
        a = jax.nn.sigmoid(a0_ref[d:d + 1] + _dot(_dot(xa, a1_ref[d]).astype(BF16), a2_ref[d]))
        kd_ref[0] = (k * (1.0 + (a - 1.0) * ka)).astype(kd_ref.dtype)
        b_ref[0] = (kkn * a).astype(b_ref.dtype)
    gate = _dot(jax.nn.sigmoid(_dot(lerp(5), g1_ref[...])).astype(BF16), g2_ref[...])
    gate_ref[0] = gate.astype(gate_ref.dtype)


def _rwkv_project(xs, mods, g1n, mix, w_rkv, w0, w1, w2, a0, a1, a2, gw1, gw2, k_k, k_a, seg,
                  ctx_len):
    b, nt, d = xs.shape
    tm = ROW_TILE
    halo = 8
    per = tm // halo
    n_halo = nt // halo
    kern = functools.partial(_rwkv_proj_kernel, tm=tm, seq_starts=(0, ctx_len),
                             seq_ends=(ctx_len - 1, nt - 1))
    tok = pl.BlockSpec((1, tm, d), lambda i, t: (i, t, 0))
    bf = lambda w: w.astype(BF16)
    outs = pl.pallas_call(
        kern,
        out_shape=tuple(jax.ShapeDtypeStruct((b, nt, d), dt)
                        for dt in (BF16, BF16, BF16, BF16, F32, F32, BF16, BF16, BF16, BF16)),
        grid=(b, nt // tm),
        in_specs=[
            tok,
            pl.BlockSpec((1, halo, d), lambda i, t: (i, jnp.maximum(t * per - 1, 0), 0)),
            pl.BlockSpec((1, halo, d), lambda i, t: (i, jnp.minimum((t + 1) * per, n_halo - 1), 0)),
            pl.BlockSpec((1, 1, N_MOD_ROWS, d), lambda i, t: (i, jnp.where(t * tm < ctx_len, 0, 1), 0, 0)),
            _const_spec((1, d)),
            _const_spec((N_MOD_ROWS, d)),
            _const_spec((d, d)), _const_spec((d, d)), _const_spec((d, d)),
            _const_spec(w1.shape), _const_spec(w2.shape), _const_spec(a1.shape), _const_spec(a2.shape),
            _const_spec(gw1.shape), _const_spec(gw2.shape),
            _const_spec((2, d)), _const_spec((2, d)), _const_spec((1, d)), _const_spec((1, d)),
            _const_spec((GROUP_LANES, GROUP_LANES)),
        ],
        out_specs=tuple(tok for _ in range(10)),
        compiler_params=_cparams(2),
        name="rwkv_project",
    )(xs, xs, xs, mods, g1n.reshape(1, d),
      jnp.concatenate([mix, jnp.zeros((N_MOD_ROWS - mix.shape[0], d), F32)], axis=0),
      bf(w_rkv[0]), bf(w_rkv[1]), bf(w_rkv[2]), bf(w1), bf(w2), bf(a1), bf(a2), bf(gw1), bf(gw2),
      w0, a0, k_k.reshape(1, d), k_a.reshape(1, d), seg)
    return outs


def _block_diag(x_bf, mask):
    return jnp.where(mask, jnp.concatenate([x_bf] * HEADS_PER_GROUP, axis=0), jnp.zeros((), BF16))


def _cumsum_rows(x, reverse):
    n = x.shape[0]
    row = lax.broadcasted_iota(jnp.int32, x.shape, 0)
    shift = 1
    while shift < n:
        if reverse:
            x = x + jnp.where(row < n - shift, pltpu.roll(x, n - shift, 0), 0.0)
        else:
            x = x + jnp.where(row >= shift, pltpu.roll(x, shift, 0), 0.0)
        shift *= 2
    return x


def _block_transpose(x):
    xt = x.T
    return jnp.concatenate([xt[j * RWKV_HEAD:(j + 1) * RWKV_HEAD] for j in range(HEADS_PER_GROUP)], axis=1)


def _wkv_masks(reverse):
    c = WKV_CHUNK
    rows = lax.broadcasted_iota(jnp.int32, (c, GROUP_LANES), 0)
    cols = lax.broadcasted_iota(jnp.int32, (c, GROUP_LANES), 1) % RWKV_HEAD
    if reverse:
        return cols > rows, cols >= rows
    return cols < rows, cols <= rows


def _bd_mask():
    bd_r = lax.broadcasted_iota(jnp.int32, (GROUP_LANES, GROUP_LANES), 0) // RWKV_HEAD
    bd_c = lax.broadcasted_iota(jnp.int32, (GROUP_LANES, GROUP_LANES), 1) // RWKV_HEAD
    return bd_r == bd_c


def _wkv_prepare(chains):
    c = WKV_CHUNK
    n = len(chains)
    rng = range(n)
    rows = lax.broadcasted_iota(jnp.int32, (c, GROUP_LANES), 0)
    cols = lax.broadcasted_iota(jnp.int32, (c, GROUP_LANES), 1) % RWKV_HEAD
    eye = (cols == rows).astype(F32)
    bmask = _bd_mask()
    masks = {rev: _wkv_masks(rev) for rev in sorted({ch[6] for ch in chains})}
    strict = [masks[ch[6]][0] for ch in chains]
    incl = [masks[ch[6]][1] for ch in chains]
    lw, kd, bb, kk, v, r = ([ch[i] for ch in chains] for i in range(6))
    bd = lambda x: _block_diag(x.astype(BF16), bmask)

    cs = [_cumsum_rows(lw[i], chains[i][6]) for i in rng]
    total = [cs[i][0:1] if chains[i][6] else cs[i][c - 1:c] for i in rng]
    a_s = [-kk[i] * jnp.exp(cs[i] - lw[i]) for i in rng]
    g_inv = [jnp.exp(-cs[i]) for i in rng]
    b_s = [bb[i] * g_inv[i] for i in rng]
    k_s = [kd[i] * g_inv[i] for i in rng]
    r_s = [r[i] * jnp.exp(cs[i]) for i in rng]
    g_rest = [jnp.exp(total[i] - cs[i]) for i in rng]
    b_e = [(bb[i] * g_rest[i]).astype(BF16) for i in rng]
    k_e = [(kd[i] * g_rest[i]).astype(BF16) for i in rng]
    g_end = [jnp.exp(total[i]) for i in rng]

    lhs = [jnp.concatenate([a_s[i], r_s[i]], axis=0).astype(BF16) for i in rng]
    pb = [_dot_nt(lhs[i], bd(b_s[i])) for i in rng]
    pk = [_dot_nt(lhs[i], bd(k_s[i])) for i in rng]
    l_ab = [jnp.where(strict[i], pb[i][:c], 0.0) for i in rng]
    m_rb = [jnp.where(incl[i], pb[i][c:], 0.0).astype(BF16) for i in rng]
    l_ak = [jnp.where(strict[i], pk[i][:c], 0.0) for i in rng]
    m_rk = [jnp.where(incl[i], pk[i][c:], 0.0) for i in rng]

    p = l_ab
    tmat = [eye + p[i] for i in rng]
    p = [_dot(p[i].astype(BF16), bd(p[i])) for i in rng]
    for _ in range(4):
        z = [_dot(jnp.concatenate([tmat[i], p[i]], axis=0).astype(BF16), bd(p[i])) for i in rng]
        tmat = [tmat[i] + z[i][:c] for i in rng]
        p = [z[i][c:] for i in rng]
    tmat = [tmat[i] + _dot(tmat[i].astype(BF16), bd(p[i])) for i in rng]
    t_bf = [tmat[i].astype(BF16) for i in rng]

    zv = [_dot(jnp.concatenate([l_ak[i], m_rk[i]], axis=0).astype(BF16), bd(v[i])) for i in rng]
    a_hat = [_dot(t_bf[i], bd(a_s[i])) for i in rng]
    u_til = [_dot(t_bf[i], bd(zv[i][:c])) for i in rng]
    r_hat = [r_s[i] + _dot(m_rb[i], bd(a_hat[i])) for i in rng]
    y_til = [zv[i][c:] + _dot(m_rb[i], bd(u_til[i])) for i in rng]

    au_t = [jnp.concatenate([_block_transpose(a_hat[i]), _block_transpose(u_til[i])], axis=0) for i in rng]
    gh = [_dot(au_t[i].astype(BF16), bd(b_e[i])) for i in rng]
    g_mat = [gh[i][:c] for i in rng]
    h_mat = [gh[i][c:] + _dot(_block_transpose(v[i].astype(F32)).astype(BF16), bd(k_e[i])) for i in rng]
    return [(r_hat[i].astype(BF16), y_til[i], g_mat[i].astype(BF16), h_mat[i], g_end[i]) for i in rng]


def _wkv_kernel(lwf, kdf, bf_, kkf, vf, rf, lwr, kdr, br, kkr, vr, rr, yf_ref, yr_ref, s_ref):
    @pl.when(pl.program_id(1) == 0)
    def _():
        s_ref[...] = jnp.zeros(s_ref.shape, F32)

    c = WKV_CHUNK
    n_sub = lwf.shape[1] // c
    n_groups = lwf.shape[2] // GROUP_LANES
    dirs = (((lwf, kdf, bf_, kkf, vf, rf), yf_ref), ((lwr, kdr, br, kkr, vr, rr), yr_ref))
    chains, keys = [], []
    for u in range(n_sub):
        rs = slice(u * c, (u + 1) * c)
        for j in range(n_groups):
            sl = slice(j * GROUP_LANES, (j + 1) * GROUP_LANES)
            for d, (refs, _) in enumerate(dirs):
                lw, kd, bb, kk, v, r = (ref[0, rs, sl] for ref in refs)
                chains.append((lw, kd.astype(F32), bb.astype(F32), kk.astype(F32), v, r.astype(F32),
                               d == 1))
                keys.append((u, j, d))
    prepared = dict(zip(keys, _wkv_prepare(chains)))

    bmask = _bd_mask()
    for j in range(n_groups):
        sl = slice(j * GROUP_LANES, (j + 1) * GROUP_LANES)
        for d, (_, y_ref) in enumerate(dirs):
            s = s_ref[d, :, sl]
            for u in (range(n_sub) if d == 0 else reversed(range(n_sub))):
                r_hat, y_til, g_mat, h_mat, g_end = prepared[(u, j, d)]
                s_bf = s.astype(BF16)
                y = y_til + _dot_nt(r_hat, _block_diag(s_bf, bmask))
                y_ref[0, u * c:(u + 1) * c, sl] = y.astype(y_ref.dtype)
                s = s * g_end + _dot(s_bf, _block_diag(g_mat, bmask)) + h_mat
            s_ref[d, :, sl] = s


def _wkv_scan(lw0, lw1, kd0, kd1, b0, b1, kk, v, r, ctx_len):
    b, nt, d = v.shape
    c = WKV_CHUNKS_PER_STEP * WKV_CHUNK
    assert ctx_len % c == 0 and nt % c == 0
    n_steps = nt // c
    ctx_chunks = ctx_len // c

    def fwd(i, s):
        return (i, s, 0)

    def rev(i, s):
        return (i, jnp.where(s < ctx_chunks, ctx_chunks - 1 - s, n_steps - 1 + ctx_chunks - s), 0)

    blk_f = pl.BlockSpec((1, c, d), fwd)
    blk_r = pl.BlockSpec((1, c, d), rev)
    return pl.pallas_call(
        _wkv_kernel,
        out_shape=(jax.ShapeDtypeStruct((b, nt, d), BF16), jax.ShapeDtypeStruct((b, nt, d), BF16)),
        grid=(b, n_steps),
        in_specs=[blk_f] * 6 + [blk_r] * 6,
        out_specs=(blk_f, blk_r),
        scratch_shapes=[pltpu.VMEM((2, RWKV_HEAD, d), F32)],
        compiler_params=_cparams(2),
        name="wkv_scan",
    )(lw0, kd0, b0, kk, v, r, lw1, kd1, b1, kk, v, r)


def _read_rwkv_mix(refs):
    yf_ref, yr_ref, r_ref, v_ref, kd0_ref, kd1_ref, gate_ref, rk_ref, lg_ref, lb_ref, seg_ref = refs
    wkv = yf_ref[0].astype(F32) + yr_ref[0].astype(F32)
    inv_n = 1.0 / RWKV_HEAD
    mu = _group_sum(wkv, seg_ref, two_pass=True) * inv_n
    dev = wkv - mu
    var = _group_sum(dev * dev, seg_ref) * inv_n
    gn = dev * lax.rsqrt(var + GN_EPS) * lg_ref[...] + lb_ref[...]
    rk = r_ref[0].astype(F32) * rk_ref[...]
    kd_sum = kd0_ref[0].astype(F32) + kd1_ref[0].astype(F32)
    bonus = _group_sum(rk * kd_sum, seg_ref) * v_ref[0].astype(F32)
    return ((gn + bonus) * gate_ref[0].astype(F32)).astype(BF16)


def _rwkv_mixer(yf, yr, r, v, kd0, kd1, gate, r_k, ln_g, ln_b, seg, ctx_len):
    b, nt, d = v.shape
    tm = ROW_TILE
    off = ctx_len // tm
    tok = pl.BlockSpec((1, tm, d), lambda i, t: (i, t + off, 0))
    specs = [tok] * 7 + [_const_spec((1, d))] * 3 + [_const_spec((GROUP_LANES, GROUP_LANES))]
    args = [yf, yr, r, v, kd0, kd1, gate, r_k.reshape(1, d), ln_g.reshape(1, d), ln_b.reshape(1, d), seg]
    return _Rows(_read_rwkv_mix, specs, args, (b, nt - ctx_len, d))


def _rope_tables(ctx_len, seq_len):
    f32 = np.float32
    rows = seq_len // GRID_W
    row = np.repeat(np.arange(rows, dtype=f32), GRID_W)
    col = np.tile(np.arange(GRID_W, dtype=f32), rows)
    inv_freq = np.power(f32(ROPE_THETA), -np.arange(ROPE_PAIRS, dtype=f32) / f32(ROPE_PAIRS)).astype(f32)
    row_ang, col_ang = row[:, None] * inv_freq, col[:, None] * inv_freq
    ang = np.concatenate([row_ang, row_ang, col_ang, col_ang], axis=1).astype(np.float64)
    cos = np.concatenate([np.ones((ctx_len, HEAD_DIM)), np.cos(ang)], axis=0)
    sin = np.concatenate([np.zeros((ctx_len, HEAD_DIM)), np.sin(ang)], axis=0)
    sign = np.concatenate([-np.ones(ROPE_PAIRS), np.ones(ROPE_PAIRS)] * 2)
    reps = V7X_LANES // HEAD_DIM
    return (jnp.asarray(np.tile(cos, (1, reps)), F32), jnp.asarray(np.tile(sin * sign, (1, reps)), F32))


def kernel(x, c, ctx, c_ctx, mod_w, mod_b, norm1_g, norm2_g, ffn_wg, ffn_wu, ffn_wd, attn_wqkv,
           attn_q_gain, attn_k_gain, attn_wo, rwkv_mix, rwkv_wrkv, rwkv_w0, rwkv_w1, rwkv_w2, rwkv_a0,
           rwkv_a1, rwkv_a2, rwkv_g1, rwkv_g2, rwkv_k_k, rwkv_k_a, rwkv_r_k, rwkv_ln_g, rwkv_ln_b,
           rwkv_wo, final_g):
    b, seq_len, d = x.shape
    ctx_len = ctx.shape[1]
    depth = mod_w.shape[0]
    assert depth == 2 and d == N_HEADS * HEAD_DIM
    assert ctx_len % ROW_TILE == 0 and seq_len % ROW_TILE == 0 and seq_len % GRID_W == 0
    ctx_tiles = ctx_len // ROW_TILE

    n_rows = -(-(b + 1) // 8) * 8
    c_rows = jnp.concatenate([c, c_ctx[None], jnp.zeros((n_rows - b - 1, d), F32)], axis=0)
    m_all = _modulation(c_rows, mod_w, mod_b).reshape(depth, n_rows, 6, d)
    pad = jnp.zeros((depth, b, N_MOD_ROWS - 6, d), F32)
    lat = jnp.concatenate([m_all[:, :b], pad], axis=2)
    con = jnp.concatenate([jnp.broadcast_to(m_all[:, b:b + 1], (depth, b, 6, d)), pad], axis=2)
    mods = jnp.stack([con, lat], axis=2)

    cos2, sin2 = _rope_tables(ctx_len, seq_len)
    seg = jnp.asarray(np.kron(np.eye(HEADS_PER_GROUP), np.ones((HEAD_DIM, HEAD_DIM))), BF16)

    stream = _stream_of(ctx, x)

    q, k, vt = _qkv_project(stream, mods[0], norm1_g[0], attn_wqkv[0], attn_q_gain[0], attn_k_gain[0],
                            cos2, sin2, seg, ctx_tiles)
    o_ctx, o_lat = _attention(q, k, vt, _score_bound(attn_q_gain[0], attn_k_gain[0]), ctx_len)
    xs = _out_ffn(_attention_mixer(o_ctx, o_lat), stream, mods[0], norm2_g[0], attn_wo[0], ffn_wg[0],
                  ffn_wu[0], ffn_wd[0], final_g, first_tile=0, ctx_tiles=ctx_tiles, final_norm=False)

    r, v, kk, gate, lw0, lw1, kd0, kd1, b0, b1 = _rwkv_project(
        xs, mods[1], norm1_g[1], rwkv_mix[0], rwkv_wrkv[0], rwkv_w0[0], rwkv_w1[0], rwkv_w2[0],
        rwkv_a0[0], rwkv_a1[0], rwkv_a2[0], rwkv_g1[0], rwkv_g2[0], rwkv_k_k[0], rwkv_k_a[0], seg,
        ctx_len)
    yf, yr = _wkv_scan(lw0, lw1, kd0, kd1, b0, b1, kk, v, r, ctx_len)
    mix = _rwkv_mixer(yf, yr, r, v, kd0, kd1, gate, rwkv_r_k[0].reshape(-1), rwkv_ln_g[0],
                      rwkv_ln_b[0], seg, ctx_len)
    return _out_ffn(mix, _rows_from(xs, ctx_tiles), mods[1], norm2_g[1], rwkv_wo[0], ffn_wg[1], ffn_wu[1],
                    ffn_wd[1], final_g, first_tile=ctx_tiles, ctx_tiles=ctx_tiles, final_norm=True)
```

```python
import functools
from typing import Callable, NamedTuple

import jax
import jax.numpy as jnp
import numpy as np
from jax import lax
from jax.experimental import pallas as pl
from jax.experimental.pallas import tpu as pltpu

F32 = jnp.float32
BF16 = jnp.bfloat16

NORM_EPS = 1e-6
GN_EPS = 64e-5
GRID_W = 64
N_HEADS = 16
N_KV_HEADS = 4
KV_GROUP = N_HEADS // N_KV_HEADS
HEAD_DIM = 64
ROPE_THETA = 10000.0
ROPE_PAIRS = HEAD_DIM // 4
RWKV_HEAD = 64
DECAY_SCALE = float(np.exp(-0.5))
LOG2_E = float(np.log2(np.e))
EXP2_SAFE_RANGE = 120.0

V7X_LANES = 128
V7X_MXU_DIM = 256
V7X_VMEM_LIMIT_BYTES = 60000 * 1024

ROW_TILE = 256
GROUP_LANES = V7X_MXU_DIM
HEADS_PER_GROUP = GROUP_LANES // HEAD_DIM
BF16_SUBLANES = 16
ATTN_V_ROWS = HEAD_DIM + BF16_SUBLANES
ATTN_CHUNKS_PER_PV = 11
WKV_CHUNK = 64
WKV_CHUNKS_PER_STEP = 1
N_MOD_ROWS = 8


def _cparams(n_axes):
    return pltpu.CompilerParams(
        dimension_semantics=("arbitrary",) * n_axes,
        vmem_limit_bytes=V7X_VMEM_LIMIT_BYTES,
    )


def _const_spec(shape):
    nd = len(shape)
    return pl.BlockSpec(shape, lambda *_: (0,) * nd, pipeline_mode=pl.Buffered(1))


def _dot(a, b):
    return jnp.dot(a, b, preferred_element_type=F32)


def _dot_nt(a, b):
    return lax.dot_general(a, b, (((1,), (1,)), ((), ())), preferred_element_type=F32)


def _norm_mod(x, g, shift, scale):
    ms = jnp.mean(x * x, axis=-1, keepdims=True)
    return (x * lax.rsqrt(ms + NORM_EPS) * g) * (1.0 + scale) + shift


def _group_sum(x, seg_ref, two_pass=False):
    seg = seg_ref[...]
    outs = []
    for j in range(x.shape[1] // GROUP_LANES):
        xs = x[:, j * GROUP_LANES:(j + 1) * GROUP_LANES]
        hi = xs.astype(BF16)
        out = _dot(hi, seg)
        if two_pass:
            out = out + _dot((xs - hi.astype(F32)).astype(BF16), seg)
        outs.append(out)
    return outs[0] if len(outs) == 1 else jnp.concatenate(outs, axis=1)


def _mod_kernel(c_ref, w_ref, b_ref, o_ref):
    c = c_ref[...]
    s = c * jax.nn.sigmoid(c)
    o_ref[0] = _dot(s.astype(BF16), w_ref[0]) + b_ref[0]


def _modulation(c_rows, mod_w, mod_b):
    depth, d, n = mod_w.shape
    rows = c_rows.shape[0]
    tn = n // 4
    return pl.pallas_call(
        _mod_kernel,
        out_shape=jax.ShapeDtypeStruct((depth, rows, n), F32),
        grid=(depth, n // tn),
        in_specs=[
            pl.BlockSpec((rows, d), lambda i, j: (0, 0)),
            pl.BlockSpec((1, d, tn), lambda i, j: (i, 0, j)),
            pl.BlockSpec((1, 1, tn), lambda i, j: (i, 0, j)),
        ],
        out_specs=pl.BlockSpec((1, rows, tn), lambda i, j: (i, 0, j)),
        compiler_params=_cparams(2),
        name="modulation",
    )(c_rows, mod_w.astype(BF16), mod_b.reshape(depth, 1, n))


def _rope(x, cos, sin_signed):
    w = x.shape[1]
    lane = lax.broadcasted_iota(jnp.int32, x.shape, 1)
    first_half = (lane % (2 * ROPE_PAIRS)) < ROPE_PAIRS
    partner = jnp.where(first_half, pltpu.roll(x, w - ROPE_PAIRS, 1), pltpu.roll(x, ROPE_PAIRS, 1))
    return x * cos + partner * sin_signed


class _Rows(NamedTuple):
    read: Callable
    specs: list
    args: list
    shape: tuple


def _stream_of(ctx, x):
    b, c, d = ctx.shape
    tm = ROW_TILE
    ctx_tiles = c // tm

    def read(refs):
        c_ref, x_ref = refs
        return jnp.where(pl.program_id(1) < ctx_tiles, c_ref[0], x_ref[0])

    specs = [pl.BlockSpec((1, tm, d), lambda i, t: (i, jnp.minimum(t, ctx_tiles - 1), 0)),
             pl.BlockSpec((1, tm, d), lambda i, t: (i, jnp.maximum(t - ctx_tiles, 0), 0))]
    return _Rows(read, specs, [ctx, x], (b, c + x.shape[1], d))


def _rows_from(xs, tile_offset):
    b, n, d = xs.shape
    spec = pl.BlockSpec((1, ROW_TILE, d), lambda i, t: (i, t + tile_offset, 0))
    return _Rows(lambda refs: refs[0][0], [spec], [xs], (b, n - tile_offset * ROW_TILE, d))


def _qkv_kernel(*refs, nq, nk, read_x, n_x):
    x = read_x(refs[:n_x])
    (mod_ref, g_ref, w_ref, qg_ref, kg_ref, cos_ref, sin_ref, seg_ref, ek_ref, ev_ref,
     q_ref, k_ref, v_ref) = refs[n_x:]
    mod = mod_ref[0, 0]
    h = _norm_mod(x, g_ref[...], mod[0:1], mod[1:2]).astype(BF16)
    qkv = _dot(h, w_ref[...])
    q, k, v = qkv[:, :nq], qkv[:, nq:nq + nk], qkv[:, nq + nk:]
    cos2, sin2 = cos_ref[...], sin_ref[...]

    def head_norm_rope(z, gain):
        reps = z.shape[1] // V7X_LANES
        cos = jnp.concatenate([cos2] * reps, axis=1)
        sin = jnp.concatenate([sin2] * reps, axis=1)
        ss = _group_sum(z * z, seg_ref)
        zn = z * lax.rsqrt(ss * (1.0 / HEAD_DIM) + NORM_EPS) * gain
        return _rope(zn, cos, sin)

    qn = head_norm_rope(q, qg_ref[...]) * (HEAD_DIM ** -0.5 * LOG2_E)
    q_ref[0] = qn.astype(BF16)
    kn = head_norm_rope(k, kg_ref[...]).astype(BF16)
    k_t = _dot(kn, ek_ref[...]).astype(BF16)
    v_bf = v.astype(BF16)
    ones = jnp.ones((ATTN_V_ROWS - HEAD_DIM, v_bf.shape[0]), BF16)
    shift_lane = lax.broadcasted_iota(jnp.int32, (v_bf.shape[0], V7X_LANES), 1) == HEAD_DIM
    for j in range(N_KV_HEADS):
        k_ref[0, j] = jnp.where(shift_lane, jnp.ones((), BF16), k_t[:, j * V7X_LANES:(j + 1) * V7X_LANES])
        v_ref[0, j, 0:HEAD_DIM, :] = _dot_nt(ev_ref[j], v_bf).astype(BF16)
        v_ref[0, j, HEAD_DIM:, :] = ones


def _head_select(n_heads, rows):
    sel = np.zeros((n_heads, rows, n_heads * HEAD_DIM), np.float32)
    for j in range(n_heads):
        sel[j, np.arange(HEAD_DIM), j * HEAD_DIM + np.arange(HEAD_DIM)] = 1.0
    return sel


def _qkv_project(rows, mods, g1, wqkv, q_gain, k_gain, cos2, sin2, seg, ctx_tiles):
    b, nt, d = rows.shape
    nq, nk = N_HEADS * HEAD_DIM, N_KV_HEADS * HEAD_DIM
    tm = ROW_TILE
    ek = np.zeros((nk, N_KV_HEADS * V7X_LANES), np.float32)
    for j in range(N_KV_HEADS):
        ek[j * HEAD_DIM + np.arange(HEAD_DIM), j * V7X_LANES + np.arange(HEAD_DIM)] = 1.0
    ev = _head_select(N_KV_HEADS, HEAD_DIM)
    kern = functools.partial(_qkv_kernel, nq=nq, nk=nk, read_x=rows.read, n_x=len(rows.specs))
    return pl.pallas_call(
        kern,
        out_shape=(
            jax.ShapeDtypeStruct((b, nt, nq), BF16),
            jax.ShapeDtypeStruct((b, N_KV_HEADS, nt, V7X_LANES), BF16),
            jax.ShapeDtypeStruct((b, N_KV_HEADS, ATTN_V_ROWS, nt), BF16),
        ),
        grid=(b, nt // tm),
        in_specs=rows.specs + [
            pl.BlockSpec((1, 1, N_MOD_ROWS, d), lambda i, t: (i, jnp.where(t < ctx_tiles, 0, 1), 0, 0)),
            _const_spec((1, d)),
            _const_spec((d, nq + 2 * nk)),
            _const_spec((1, nq)),
            _const_spec((1, nk)),
            pl.BlockSpec((tm, V7X_LANES), lambda i, t: (t, 0)),
            pl.BlockSpec((tm, V7X_LANES), lambda i, t: (t, 0)),
            _const_spec((GROUP_LANES, GROUP_LANES)),
            _const_spec(ek.shape),
            _const_spec(ev.shape),
        ],
        out_specs=(
            pl.BlockSpec((1, tm, nq), lambda i, t: (i, t, 0)),
            pl.BlockSpec((1, N_KV_HEADS, tm, V7X_LANES), lambda i, t: (i, 0, t, 0)),
            pl.BlockSpec((1, N_KV_HEADS, ATTN_V_ROWS, tm), lambda i, t: (i, 0, 0, t)),
        ),
        compiler_params=_cparams(2),
        name="qkv_project",
    )(*rows.args, mods, g1.reshape(1, d), wqkv.astype(BF16),
      jnp.tile(q_gain, N_HEADS).reshape(1, nq), jnp.tile(k_gain, N_KV_HEADS).reshape(1, nk),
      cos2, sin2, seg, jnp.asarray(ek, BF16), jnp.asarray(ev, BF16))


def _attn_kernel(*refs, n_sub, tk, n_chunks, unroll):
    bound_ref = refs[0]
    q_refs = refs[1:1 + n_sub]
    k_ref, vt_ref, o_ref, qt_ref, m_ref, acc_ref, sa_ref, sb_ref, p_ref = refs[1 + n_sub:]
    ts = q_refs[0].shape[1]
    width = qt_ref.shape[1]
    for s, q_ref in enumerate(q_refs):
        q_t = q_ref[0].astype(F32).T
        for g in range(KV_GROUP):
            col = (s * KV_GROUP + g) * ts
            qt_ref[0:HEAD_DIM, col:col + ts] = q_t[g * HEAD_DIM:(g + 1) * HEAD_DIM].astype(BF16)
    qt_ref[HEAD_DIM:, :] = jnp.full((qt_ref.shape[0] - HEAD_DIM, width), bound_ref[0], F32).astype(BF16)
    acc_ref[...] = jnp.zeros(acc_ref.shape, F32)

    def chunk(c, n=1):
        return pl.ds(c * tk if isinstance(c, int) else pl.multiple_of(c * tk, n * tk), n * tk)

    def scores(c):
        return _dot(k_ref[0, 0, chunk(c), :], qt_ref[...])

    @pl.when(bound_ref[1] > 0.5)
    def _():
        def accumulate(c0, n):
            for u in range(n):
                p_ref[u * tk:(u + 1) * tk, :] = jnp.exp2(scores(c0 + u)).astype(BF16)
            acc_ref[...] += _dot(vt_ref[0, 0, :, chunk(c0, n)], p_ref[0:n * tk, :])

        per_trip = p_ref.shape[0] // tk

        def trip(j, carry):
            accumulate(per_trip * j, per_trip)
            return carry

        n_trips, rem = n_chunks // per_trip, n_chunks % per_trip
        if n_trips:
            lax.fori_loop(0, n_trips, trip, 0)
        if rem:
            accumulate(n_trips * per_trip, rem)

    @pl.when(bound_ref[1] <= 0.5)
    def _():
        _attn_online_softmax(scores, chunk, vt_ref, m_ref, acc_ref, sa_ref, sb_ref, p_ref,
                             n_chunks=n_chunks, unroll=unroll)

    acc = acc_ref[...]
    ot = acc[0:HEAD_DIM] / acc[HEAD_DIM:HEAD_DIM + 1]
    for s in range(n_sub):
        heads = [ot[:, (s * KV_GROUP + g) * ts:(s * KV_GROUP + g + 1) * ts] for g in range(KV_GROUP)]
        o_ref[0, s * ts:(s + 1) * ts, :] = jnp.concatenate(heads, axis=0).T.astype(o_ref.dtype)


def _attn_online_softmax(scores, chunk, vt_ref, m_ref, acc_ref, sa_ref, sb_ref, p_ref, *, n_chunks, unroll):
    m_ref[...] = jnp.full(m_ref.shape, -jnp.inf, F32)
    tk = sa_ref.shape[0]

    def update(c, s_ref):
        vtc = vt_ref[0, 0, :, chunk(c)]
        alphas = []
        for cb in range(s_ref.shape[1] // V7X_LANES):
            cols = slice(cb * V7X_LANES, (cb + 1) * V7X_LANES)
            st = s_ref[:, cols]
            m_prev = m_ref[:, cols]
            m_new = jnp.maximum(m_prev, jnp.max(st, axis=0, keepdims=True))
            alphas.append(jnp.exp2(m_prev - m_new))
            p_ref[0:tk, cols] = jnp.exp2(st - m_new).astype(BF16)
            m_ref[:, cols] = m_new
        alpha = jnp.concatenate(alphas, axis=1)
        acc_ref[...] = alpha * acc_ref[...] + _dot(vtc, p_ref[0:tk, :])

    slots = (sa_ref, sb_ref)
    sa_ref[...] = scores(0)

    def body(j, carry):
        c0 = unroll * j
        for u in range(unroll):
            slots[(u + 1) % 2][...] = scores(c0 + u + 1)
            update(c0 + u, slots[u % 2])
        return carry

    if n_chunks > 1:
        lax.fori_loop(0, (n_chunks - 1) // unroll, body, 0)
    update(n_chunks - 1, slots[0])


def _attention_call(bound, q, k, vt, *, row0, n_rows, n_keys, n_sub, name):
    b, _, nq = q.shape
    ts = ROW_TILE
    tk = ROW_TILE
    tq = n_sub * ts
    n_chunks = n_keys // tk
    assert n_rows % tq == 0 and row0 % ts == 0 and n_keys % tk == 0 and n_chunks % 2 == 1
    unroll = max(u for u in (2, 4) if (n_chunks - 1) % u == 0)
    kern = functools.partial(_attn_kernel, n_sub=n_sub, tk=tk, n_chunks=n_chunks, unroll=unroll)
    width = n_sub * KV_GROUP * ts

    def q_spec(s):
        return pl.BlockSpec((1, ts, GROUP_LANES), lambda i, j, t: (i, row0 // ts + n_sub * t + s, j))

    return pl.pallas_call(
        kern,
        out_shape=jax.ShapeDtypeStruct((b, n_rows, nq), BF16),
        grid=(b, N_KV_HEADS, n_rows // tq),
        in_specs=[pl.BlockSpec(memory_space=pltpu.SMEM)] + [q_spec(s) for s in range(n_sub)] + [
            pl.BlockSpec((1, 1, n_keys, V7X_LANES), lambda i, j, t: (i, j, 0, 0)),
            pl.BlockSpec((1, 1, ATTN_V_ROWS, n_keys), lambda i, j, t: (i, j, 0, 0)),
        ],
        out_specs=pl.BlockSpec((1, tq, GROUP_LANES), lambda i, j, t: (i, t, j)),
        scratch_shapes=[
            pltpu.VMEM((V7X_LANES, width), BF16),
            pltpu.VMEM((1, width), F32),
            pltpu.VMEM((ATTN_V_ROWS, width), F32),
            pltpu.VMEM((tk, width), F32),
            pltpu.VMEM((tk, width), F32),
            pltpu.VMEM((min(ATTN_CHUNKS_PER_PV, n_chunks) * tk, width), BF16),
        ],
        compiler_params=_cparams(3),
        name=name,
    )(bound, *([q] * n_sub), k, vt)


def _score_bound(q_gain, k_gain):
    bound = (HEAD_DIM * (HEAD_DIM ** -0.5 * LOG2_E) * 1.02) * jnp.max(jnp.abs(q_gain)) * jnp.max(jnp.abs(k_gain))
    bound = bound.astype(BF16).astype(F32)
    return jnp.stack([-bound, (2.0 * bound <= EXP2_SAFE_RANGE).astype(F32)])


def _attention(q, k, vt, bound, ctx_len):
    nt = q.shape[1]
    n_lat = nt - ctx_len
    n_sub = max(s for s in (1, 2, 4) if n_lat % (s * ROW_TILE) == 0)
    o_ctx = _attention_call(bound, q, k, vt, row0=0, n_rows=ctx_len, n_keys=ctx_len, n_sub=1,
                            name="flash_attention_ctx")
    o_lat = _attention_call(bound, q, k, vt, row0=ctx_len, n_rows=n_lat, n_keys=nt, n_sub=n_sub,
                            name="flash_attention")
    return o_ctx, o_lat


def _out_ffn_kernel(*refs, final_norm, read_mix, n_mix, read_x, n_x):
    y = read_mix(refs[:n_mix])
    x = read_x(refs[n_mix:n_mix + n_x])
    mod_ref, g_ref, wo_ref, wg_ref, wu_ref, wd_ref, fg_ref, o_ref = refs[n_mix + n_x:]
    mod = mod_ref[0, 0]
    x1 = x + mod[2:3] * _dot(y, wo_ref[...])
    h2 = _norm_mod(x1, g_ref[...], mod[3:4], mod[4:5]).astype(BF16)
    a = _dot(h2, wg_ref[...])
    u = _dot(h2, wu_ref[...])
    hid = (a * jax.nn.sigmoid(a) * u).astype(BF16)
    x2 = x1 + mod[5:6] * _dot(hid, wd_ref[...])
    if final_norm:
        ms = jnp.mean(x2 * x2, axis=-1, keepdims=True)
        x2 = x2 * lax.rsqrt(ms + NORM_EPS) * fg_ref[...]
    o_ref[0] = x2


def _attention_mixer(o_ctx, o_lat):
    d = o_lat.shape[2]
    tm = ROW_TILE
    ctx_tiles = o_ctx.shape[1] // tm

    def read(refs):
        yc_ref, yl_ref = refs
        return jnp.where(pl.program_id(1) < ctx_tiles, yc_ref[0], yl_ref[0])

    specs = [pl.BlockSpec((1, tm, d), lambda i, t: (i, jnp.minimum(t, ctx_tiles - 1), 0)),
             pl.BlockSpec((1, tm, d), lambda i, t: (i, jnp.maximum(t - ctx_tiles, 0), 0))]
    return _Rows(read, specs, [o_ctx, o_lat], (o_lat.shape[0], o_ctx.shape[1] + o_lat.shape[1], d))


def _out_ffn(mix, rows, mods, g2, wo, wg, wu, wd, final_g, *, first_tile, ctx_tiles, final_norm):
    b, n, d = rows.shape
    assert mix.shape == rows.shape
    f = wg.shape[1]
    tm = ROW_TILE
    kern = functools.partial(_out_ffn_kernel, final_norm=final_norm, read_mix=mix.read,
                             n_mix=len(mix.specs), read_x=rows.read, n_x=len(rows.specs))

    def mod_sel(i, t):
        return (i, jnp.where(t + first_tile < ctx_tiles, 0, 1), 0, 0)

    return pl.pallas_call(
        kern,
        out_shape=jax.ShapeDtypeStruct((b, n, d), F32),
        grid=(b, n // tm),
        in_specs=mix.specs + rows.specs + [
            pl.BlockSpec((1, 1, N_MOD_ROWS, d), mod_sel),
            _const_spec((1, d)),
            _const_spec((d, d)),
            _const_spec((d, f)),
            _const_spec((d, f)),
            _const_spec((f, d)),
            _const_spec((1, d)),
        ],
        out_specs=pl.BlockSpec((1, tm, d), lambda i, t: (i, t, 0)),
        compiler_params=_cparams(2),
        name="out_ffn_final" if final_norm else "out_ffn",
    )(*mix.args, *rows.args, mods, g2.reshape(1, d), wo.astype(BF16), wg.astype(BF16), wu.astype(BF16),
      wd.astype(BF16), final_g.reshape(1, d))


def _rwkv_proj_kernel(x_ref, xp_ref, xn_ref, mod_ref, g_ref, mix_ref, wr_ref, wk_ref, wv_ref,
                      w1_ref, w2_ref, a1_ref, a2_ref, g1_ref, g2_ref, w0_ref, a0_ref, kk_ref, ka_ref,
                      seg_ref,
                      r_ref, v_ref, kkn_ref, gate_ref, lw0_ref, lw1_ref, kd0_ref, kd1_ref,
                      b0_ref, b1_ref, *, tm, seq_starts, seq_ends):
    t = pl.program_id(1)
    mod = mod_ref[0, 0]
    g = g_ref[...]
    h = _norm_mod(x_ref[0], g, mod[0:1], mod[1:2])
    h_prev = _norm_mod(xp_ref[0], g, mod[0:1], mod[1:2])[7:8]
    h_next = _norm_mod(xn_ref[0], g, mod[0:1], mod[1:2])[0:1]
    row = lax.broadcasted_iota(jnp.int32, h.shape, 0)
    pos = row + t * tm
    at_start = functools.reduce(jnp.logical_or, [pos == s for s in seq_starts])
    at_end = functools.reduce(jnp.logical_or, [pos == e for e in seq_ends])
    before = jnp.where(row == 0, h_prev, pltpu.roll(h, 1, 0))
    after = jnp.where(row == tm - 1, h_next, pltpu.roll(h, tm - 1, 0))
    before = jnp.where(at_start, 0.0, before)
    after = jnp.where(at_end, 0.0, after)
    xx = 0.5 * (before + after) - h
    mix = mix_ref[...]

    def lerp(j):
        return (h + xx * mix[j:j + 1]).astype(BF16)

    r_ref[0] = _dot(lerp(0), wr_ref[...]).astype(r_ref.dtype)
    k = _dot(lerp(2), wk_ref[...])
    v_ref[0] = _dot(lerp(3), wv_ref[...]).astype(v_ref.dtype)
    kkr = k * kk_ref[...]
    ss = _group_sum(kkr * kkr, seg_ref)
    kkn = kkr * lax.rsqrt(jnp.maximum(ss, 1e-24))
    kkn_ref[0] = kkn.astype(kkn_ref.dtype)
    xw, xa = lerp(1), lerp(4)
    ka = ka_ref[...]
    for d, (lw_ref, kd_ref, b_ref) in enumerate(((lw0_ref, kd0_ref, b0_ref), (lw1_ref, kd1_ref, b1_ref))):
        z = w0_ref[d:d + 1] + _dot(jnp.tanh(_dot(xw, w1_ref[d])).astype(BF16), w2_ref[d])
        lw_ref[0] = -DECAY_SCALE * jax.nn.sigmoid(z)
        a = jax.nn.sigmoid(a0_ref[d:d + 1] + _dot(_dot(xa, a1_ref[d]).astype(BF16), a2_ref[d]))
        kd_ref[0] = (k * (1.0 + (a - 1.0) * ka)).astype(kd_ref.dtype)
        b_ref[0] = (kkn * a).astype(b_ref.dtype)
    gate = _dot(jax.nn.sigmoid(_dot(lerp(5), g1_ref[...])).astype(BF16), g2_ref[...])
    gate_ref[0] = gate.astype(gate_ref.dtype)


def _rwkv_project(xs, mods, g1n, mix, w_rkv, w0, w1, w2, a0, a1, a2, gw1, gw2, k_k, k_a, seg,
                  ctx_len):
    b, nt, d = xs.shape
    tm = ROW_TILE
    halo = 8
    per = tm // halo
    n_halo = nt // halo
    kern = functools.partial(_rwkv_proj_kernel, tm=tm, seq_starts=(0, ctx_len),
                             seq_ends=(ctx_len - 1, nt - 1))
    tok = pl.BlockSpec((1, tm, d), lambda i, t: (i, t, 0))
    bf = lambda w: w.astype(BF16)
    outs = pl.pallas_call(
        kern,
        out_shape=tuple(jax.ShapeDtypeStruct((b, nt, d), dt)
                        for dt in (BF16, BF16, BF16, BF16, F32, F32, BF16, BF16, BF16, BF16)),
        grid=(b, nt // tm),
        in_specs=[
            tok,
            pl.BlockSpec((1, halo, d), lambda i, t: (i, jnp.maximum(t * per - 1, 0), 0)),
            pl.BlockSpec((1, halo, d), lambda i, t: (i, jnp.minimum((t + 1) * per, n_halo - 1), 0)),
            pl.BlockSpec((1, 1, N_MOD_ROWS, d), lambda i, t: (i, jnp.where(t * tm < ctx_len, 0, 1), 0, 0)),
            _const_spec((1, d)),
            _const_spec((N_MOD_ROWS, d)),
            _const_spec((d, d)), _const_spec((d, d)), _const_spec((d, d)),
            _const_spec(w1.shape), _const_spec(w2.shape), _const_spec(a1.shape), _const_spec(a2.shape),
            _const_spec(gw1.shape), _const_spec(gw2.shape),
            _const_spec((2, d)), _const_spec((2, d)), _const_spec((1, d)), _const_spec((1, d)),
            _const_spec((GROUP_LANES, GROUP_LANES)),
        ],
        out_specs=tuple(tok for _ in range(10)),
        compiler_params=_cparams(2),
        name="rwkv_project",
    )(xs, xs, xs, mods, g1n.reshape(1, d),
      jnp.concatenate([mix, jnp.zeros((N_MOD_ROWS - mix.shape[0], d), F32)], axis=0),
      bf(w_rkv[0]), bf(w_rkv[1]), bf(w_rkv[2]), bf(w1), bf(w2), bf(a1), bf(a2), bf(gw1), bf(gw2),
      w0, a0, k_k.reshape(1, d), k_a.reshape(1, d), seg)
    return outs


def _block_diag(x_bf, mask):
    return jnp.where(mask, jnp.concatenate([x_bf] * HEADS_PER_GROUP, axis=0), jnp.zeros((), BF16))


def _cumsum_rows(x, reverse):
    n = x.shape[0]
    row = lax.broadcasted_iota(jnp.int32, x.shape, 0)
    shift = 1
    while shift < n:
        if reverse:
            x = x + jnp.where(row < n - shift, pltpu.roll(x, n - shift, 0), 0.0)
        else:
            x = x + jnp.where(row >= shift, pltpu.roll(x, shift, 0), 0.0)
        shift *= 2
    return x


def _block_transpose(x):
    xt = x.T
    return jnp.concatenate([xt[j * RWKV_HEAD:(j + 1) * RWKV_HEAD] for j in range(HEADS_PER_GROUP)], axis=1)


def _wkv_masks(reverse):
    c = WKV_CHUNK
    rows = lax.broadcasted_iota(jnp.int32, (c, GROUP_LANES), 0)
    cols = lax.broadcasted_iota(jnp.int32, (c, GROUP_LANES), 1) % RWKV_HEAD
    if reverse:
        return cols > rows, cols >= rows
    return cols < rows, cols <= rows


def _bd_mask():
    bd_r = lax.broadcasted_iota(jnp.int32, (GROUP_LANES, GROUP_LANES), 0) // RWKV_HEAD
    bd_c = lax.broadcasted_iota(jnp.int32, (GROUP_LANES, GROUP_LANES), 1) // RWKV_HEAD
    return bd_r == bd_c


def _wkv_prepare(chains):
    c = WKV_CHUNK
    n = len(chains)
    rng = range(n)
    rows = lax.broadcasted_iota(jnp.int32, (c, GROUP_LANES), 0)
    cols = lax.broadcasted_iota(jnp.int32, (c, GROUP_LANES), 1) % RWKV_HEAD
    eye = (cols == rows).astype(F32)
    bmask = _bd_mask()
    masks = {rev: _wkv_masks(rev) for rev in sorted({ch[6] for ch in chains})}
    strict = [masks[ch[6]][0] for ch in chains]
    incl = [masks[ch[6]][1] for ch in chains]
    lw, kd, bb, kk, v, r = ([ch[i] for ch in chains] for i in range(6))
    bd = lambda x: _block_diag(x.astype(BF16), bmask)

    cs = [_cumsum_rows(lw[i], chains[i][6]) for i in rng]
    total = [cs[i][0:1] if chains[i][6] else cs[i][c - 1:c] for i in rng]
    a_s = [-kk[i] * jnp.exp(cs[i] - lw[i]) for i in rng]
    g_inv = [jnp.exp(-cs[i]) for i in rng]
    b_s = [bb[i] * g_inv[i] for i in rng]
    k_s = [kd[i] * g_inv[i] for i in rng]
    r_s = [r[i] * jnp.exp(cs[i]) for i in rng]
    g_rest = [jnp.exp(total[i] - cs[i]) for i in rng]
    b_e = [(bb[i] * g_rest[i]).astype(BF16) for i in rng]
    k_e = [(kd[i] * g_rest[i]).astype(BF16) for i in rng]
    g_end = [jnp.exp(total[i]) for i in rng]

    lhs = [jnp.concatenate([a_s[i], r_s[i]], axis=0).astype(BF16) for i in rng]
    pb = [_dot_nt(lhs[i], bd(b_s[i])) for i in rng]
    pk = [_dot_nt(lhs[i], bd(k_s[i])) for i in rng]
    l_ab = [jnp.where(strict[i], pb[i][:c], 0.0) for i in rng]
    m_rb = [jnp.where(incl[i], pb[i][c:], 0.0).astype(BF16) for i in rng]
    l_ak = [jnp.where(strict[i], pk[i][:c], 0.0) for i in rng]
    m_rk = [jnp.where(incl[i], pk[i][c:], 0.0) for i in rng]

    blk_r, blk_c = rows, cols

    def coupling(i, w):
        pair = ((blk_r // (2 * w)) == (blk_c // (2 * w))) & ((blk_r // w) != (blk_c // w))
        return jnp.where(pair, l_ab[i], 0.0)

    tmat = [eye + coupling(i, 1) for i in rng]
    w = 2
    while w < c:
        ed = [_dot(coupling(i, w).astype(BF16), bd(tmat[i])) for i in rng]
        tmat = [tmat[i] + _dot(tmat[i].astype(BF16), bd(ed[i])) for i in rng]
        w *= 2
    t_bf = [tmat[i].astype(BF16) for i in rng]

    zv = [_dot(jnp.concatenate([l_ak[i], m_rk[i]], axis=0).astype(BF16), bd(v[i])) for i in rng]
    a_hat = [_dot(t_bf[i], bd(a_s[i])) for i in rng]
    u_til = [_dot(t_bf[i], bd(zv[i][:c])) for i in rng]
    r_hat = [r_s[i] + _dot(m_rb[i], bd(a_hat[i])) for i in rng]
    y_til = [zv[i][c:] + _dot(m_rb[i], bd(u_til[i])) for i in rng]

    au_t = [jnp.concatenate([_block_transpose(a_hat[i]), _block_transpose(u_til[i])], axis=0) for i in rng]
    gh = [_dot(au_t[i].astype(BF16), bd(b_e[i])) for i in rng]
    g_mat = [gh[i][:c] for i in rng]
    h_mat = [gh[i][c:] + _dot(_block_transpose(v[i].astype(F32)).astype(BF16), bd(k_e[i])) for i in rng]
    return [(r_hat[i].astype(BF16), y_til[i], g_mat[i].astype(BF16), h_mat[i], g_end[i]) for i in rng]


def _wkv_kernel(lwf, kdf, bf_, kkf, vf, rf, lwr, kdr, br, kkr, vr, rr, yf_ref, yr_ref, s_ref):
    @pl.when(pl.program_id(1) == 0)
    def _():
        s_ref[...] = jnp.zeros(s_ref.shape, F32)

    c = WKV_CHUNK
    n_sub = lwf.shape[1] // c
    n_groups = lwf.shape[2] // GROUP_LANES
    dirs = (((lwf, kdf, bf_, kkf, vf, rf), yf_ref), ((lwr, kdr, br, kkr, vr, rr), yr_ref))
    chains, keys = [], []
    for u in range(n_sub):
        rs = slice(u * c, (u + 1) * c)
        for j in range(n_groups):
            sl = slice(j * GROUP_LANES, (j + 1) * GROUP_LANES)
            for d, (refs, _) in enumerate(dirs):
                lw, kd, bb, kk, v, r = (ref[0, rs, sl] for ref in refs)
                chains.append((lw, kd.astype(F32), bb.astype(F32), kk.astype(F32), v, r.astype(F32),
                               d == 1))
                keys.append((u, j, d))
    prepared = dict(zip(keys, _wkv_prepare(chains)))

    bmask = _bd_mask()
    for j in range(n_groups):
        sl = slice(j * GROUP_LANES, (j + 1) * GROUP_LANES)
        for d, (_, y_ref) in enumerate(dirs):
            s = s_ref[d, :, sl]
            for u in (range(n_sub) if d == 0 else reversed(range(n_sub))):
                r_hat, y_til, g_mat, h_mat, g_end = prepared[(u, j, d)]
                s_bf = s.astype(BF16)
                y = y_til + _dot_nt(r_hat, _block_diag(s_bf, bmask))
                y_ref[0, u * c:(u + 1) * c, sl] = y.astype(y_ref.dtype)
                s = s * g_end + _dot(s_bf, _block_diag(g_mat, bmask)) + h_mat
            s_ref[d, :, sl] = s


def _wkv_scan(lw0, lw1, kd0, kd1, b0, b1, kk, v, r, ctx_len):
    b, nt, d = v.shape
    c = WKV_CHUNKS_PER_STEP * WKV_CHUNK
    assert ctx_len % c == 0 and nt % c == 0
    n_steps = nt // c
    ctx_chunks = ctx_len // c

    def fwd(i, s):
        return (i, s, 0)

    def rev(i, s):
        return (i, jnp.where(s < ctx_chunks, ctx_chunks - 1 - s, n_steps - 1 + ctx_chunks - s), 0)

    blk_f = pl.BlockSpec((1, c, d), fwd)
    blk_r = pl.BlockSpec((1, c, d), rev)
    return pl.pallas_call(
        _wkv_kernel,
        out_shape=(jax.ShapeDtypeStruct((b, nt, d), BF16), jax.ShapeDtypeStruct((b, nt, d), BF16)),
        grid=(b, n_steps),
        in_specs=[blk_f] * 6 + [blk_r] * 6,
        out_specs=(blk_f, blk_r),
        scratch_shapes=[pltpu.VMEM((2, RWKV_HEAD, d), F32)],
        compiler_params=_cparams(2),
        name="wkv_scan",
    )(lw0, kd0, b0, kk, v, r, lw1, kd1, b1, kk, v, r)


def _read_rwkv_mix(refs):
    yf_ref, yr_ref, r_ref, v_ref, kd0_ref, kd1_ref, gate_ref, rk_ref, lg_ref, lb_ref, seg_ref = refs
    wkv = yf_ref[0].astype(F32) + yr_ref[0].astype(F32)
    inv_n = 1.0 / RWKV_HEAD
    mu = _group_sum(wkv, seg_ref, two_pass=True) * inv_n
    dev = wkv - mu
    var = _group_sum(dev * dev, seg_ref) * inv_n
    gn = dev * lax.rsqrt(var + GN_EPS) * lg_ref[...] + lb_ref[...]
    rk = r_ref[0].astype(F32) * rk_ref[...]
    kd_sum = kd0_ref[0].astype(F32) + kd1_ref[0].astype(F32)
    bonus = _group_sum(rk * kd_sum, seg_ref) * v_ref[0].astype(F32)
    return ((gn + bonus) * gate_ref[0].astype(F32)).astype(BF16)


def _rwkv_mixer(yf, yr, r, v, kd0, kd1, gate, r_k, ln_g, ln_b, seg, ctx_len):
    b, nt, d = v.shape
    tm = ROW_TILE
    off = ctx_len // tm
    tok = pl.BlockSpec((1, tm, d), lambda i, t: (i, t + off, 0))
    specs = [tok] * 7 + [_const_spec((1, d))] * 3 + [_const_spec((GROUP_LANES, GROUP_LANES))]
    args = [yf, yr, r, v, kd0, kd1, gate, r_k.reshape(1, d), ln_g.reshape(1, d), ln_b.reshape(1, d), seg]
    return _Rows(_read_rwkv_mix, specs, args, (b, nt - ctx_len, d))


def _rope_tables(ctx_len, seq_len):
    f32 = np.float32
    rows = seq_len // GRID_W
    row = np.repeat(np.arange(rows, dtype=f32), GRID_W)
    col = np.tile(np.arange(GRID_W, dtype=f32), rows)
    inv_freq = np.power(f32(ROPE_THETA), -np.arange(ROPE_PAIRS, dtype=f32) / f32(ROPE_PAIRS)).astype(f32)
    row_ang, col_ang = row[:, None] * inv_freq, col[:, None] * inv_freq
    ang = np.concatenate([row_ang, row_ang, col_ang, col_ang], axis=1).astype(np.float64)
    cos = np.concatenate([np.ones((ctx_len, HEAD_DIM)), np.cos(ang)], axis=0)
    sin = np.concatenate([np.zeros((ctx_len, HEAD_DIM)), np.sin(ang)], axis=0)
    sign = np.concatenate([-np.ones(ROPE_PAIRS), np.ones(ROPE_PAIRS)] * 2)
    reps = V7X_LANES // HEAD_DIM
    return (jnp.asarray(np.tile(cos, (1, reps)), F32), jnp.asarray(np.tile(sin * sign, (1, reps)), F32))


def kernel(x, c, ctx, c_ctx, mod_w, mod_b, norm1_g, norm2_g, ffn_wg, ffn_wu, ffn_wd, attn_wqkv,
           attn_q_gain, attn_k_gain, attn_wo, rwkv_mix, rwkv_wrkv, rwkv_w0, rwkv_w1, rwkv_w2, rwkv_a0,
           rwkv_a1, rwkv_a2, rwkv_g1, rwkv_g2, rwkv_k_k, rwkv_k_a, rwkv_r_k, rwkv_ln_g, rwkv_ln_b,
           rwkv_wo, final_g):
    b, seq_len, d = x.shape
    ctx_len = ctx.shape[1]
    depth = mod_w.shape[0]
    assert depth == 2 and d == N_HEADS * HEAD_DIM
    assert ctx_len % ROW_TILE == 0 and seq_len % ROW_TILE == 0 and seq_len % GRID_W == 0
    ctx_tiles = ctx_len // ROW_TILE

    n_rows = -(-(b + 1) // 8) * 8
    c_rows = jnp.concatenate([c, c_ctx[None], jnp.zeros((n_rows - b - 1, d), F32)], axis=0)
    m_all = _modulation(c_rows, mod_w, mod_b).reshape(depth, n_rows, 6, d)
    pad = jnp.zeros((depth, b, N_MOD_ROWS - 6, d), F32)
    lat = jnp.concatenate([m_all[:, :b], pad], axis=2)
    con = jnp.concatenate([jnp.broadcast_to(m_all[:, b:b + 1], (depth, b, 6, d)), pad], axis=2)
    mods = jnp.stack([con, lat], axis=2)

    cos2, sin2 = _rope_tables(ctx_len, seq_len)
    seg = jnp.asarray(np.kron(np.eye(HEADS_PER_GROUP), np.ones((HEAD_DIM, HEAD_DIM))), BF16)

    stream = _stream_of(ctx, x)

    q, k, vt = _qkv_project(stream, mods[0], norm1_g[0], attn_wqkv[0], attn_q_gain[0], attn_k_gain[0],
                            cos2, sin2, seg, ctx_tiles)
    o_ctx, o_lat = _attention(q, k, vt, _score_bound(attn_q_gain[0], attn_k_gain[0]), ctx_len)
    xs = _out_ffn(_attention_mixer(o_ctx, o_lat), stream, mods[0], norm2_g[0], attn_wo[0], ffn_wg[0],
                  ffn_wu[0], ffn_wd[0], final_g, first_tile=0, ctx_tiles=ctx_tiles, final_norm=False)

    r, v, kk, gate, lw0, lw1, kd0, kd1, b0, b1 = _rwkv_project(
        xs, mods[1], norm1_g[1], rwkv_mix[0], rwkv_wrkv[0], rwkv_w0[0], rwkv_w1[0], rwkv_w2[0],
        rwkv_a0[0], rwkv_a1[0], rwkv_a2[0], rwkv_g1[0], rwkv_g2[0], rwkv_k_k[0], rwkv_k_a[0], seg,
        ctx_len)
    yf, yr = _wkv_scan(lw0, lw1, kd0, kd1, b0, b1, kk, v, r, ctx_len)
    mix = _rwkv_mixer(yf, yr, r, v, kd0, kd1, gate, rwkv_r_k[0].reshape(-1), rwkv_ln_g[0],
                      rwkv_ln_b[0], seg, ctx_len)
    return _out_ffn(mix, _rows_from(xs, ctx_tiles), mods[1], norm2_g[1], rwkv_wo[0], ffn_wg[1], ffn_wu[1],
                    ffn_wd[1], final_g, first_tile=ctx_tiles, ctx_tiles=ctx_tiles, final_norm=True)
```

```python
import functools
from typing import Callable, NamedTuple

import jax
import jax.numpy as jnp
import numpy as np
from jax import lax
from jax.experimental import pallas as pl
from jax.experimental.pallas import tpu as pltpu

F32 = jnp.float32
BF16 = jnp.bfloat16

NORM_EPS = 1e-6
GN_EPS = 64e-5
GRID_W = 64
N_HEADS = 16
N_KV_HEADS = 4
KV_GROUP = N_HEADS // N_KV_HEADS
HEAD_DIM = 64
ROPE_THETA = 10000.0
ROPE_PAIRS = HEAD_DIM // 4
RWKV_HEAD = 64
DECAY_SCALE = float(np.exp(-0.5))
LOG2_E = float(np.log2(np.e))
EXP2_SAFE_RANGE = 120.0

V7X_LANES = 128
V7X_MXU_DIM = 256
V7X_VMEM_LIMIT_BYTES = 60000 * 1024

ROW_TILE = 256
GROUP_LANES = V7X_MXU_DIM
HEADS_PER_GROUP = GROUP_LANES // HEAD_DIM
BF16_SUBLANES = 16
ATTN_V_ROWS = HEAD_DIM + BF16_SUBLANES
ATTN_CHUNKS_PER_PV = 11
WKV_CHUNK = 64
WKV_CHUNKS_PER_STEP = 2
N_MOD_ROWS = 8


def _cparams(n_axes):
    return pltpu.CompilerParams(
        dimension_semantics=("arbitrary",) * n_axes,
        vmem_limit_bytes=V7X_VMEM_LIMIT_BYTES,
    )


def _const_spec(shape):
    nd = len(shape)
    return pl.BlockSpec(shape, lambda *_: (0,) * nd, pipeline_mode=pl.Buffered(1))


def _dot(a, b):
    return jnp.dot(a, b, preferred_element_type=F32)


def _dot_nt(a, b):
    return lax.dot_general(a, b, (((1,), (1,)), ((), ())), preferred_element_type=F32)


def _norm_mod(x, g, shift, scale):
    ms = jnp.mean(x * x, axis=-1, keepdims=True)
    return (x * lax.rsqrt(ms + NORM_EPS) * g) * (1.0 + scale) + shift


def _group_sum(x, seg_ref, two_pass=False):
    seg = seg_ref[...]
    outs = []
    for j in range(x.shape[1] // GROUP_LANES):
        xs = x[:, j * GROUP_LANES:(j + 1) * GROUP_LANES]
        hi = xs.astype(BF16)
        out = _dot(hi, seg)
        if two_pass:
            out = out + _dot((xs - hi.astype(F32)).astype(BF16), seg)
        outs.append(out)
    return outs[0] if len(outs) == 1 else jnp.concatenate(outs, axis=1)


def _mod_kernel(c_ref, w_ref, b_ref, o_ref):
    c = c_ref[...]
    s = c * jax.nn.sigmoid(c)
    o_ref[0] = _dot(s.astype(BF16), w_ref[0]) + b_ref[0]


def _modulation(c_rows, mod_w, mod_b):
    depth, d, n = mod_w.shape
    rows = c_rows.shape[0]
    tn = n // 4
    return pl.pallas_call(
        _mod_kernel,
        out_shape=jax.ShapeDtypeStruct((depth, rows, n), F32),
        grid=(depth, n // tn),
        in_specs=[
            pl.BlockSpec((rows, d), lambda i, j: (0, 0)),
            pl.BlockSpec((1, d, tn), lambda i, j: (i, 0, j)),
            pl.BlockSpec((1, 1, tn), lambda i, j: (i, 0, j)),
        ],
        out_specs=pl.BlockSpec((1, rows, tn), lambda i, j: (i, 0, j)),
        compiler_params=_cparams(2),
        name="modulation",
    )(c_rows, mod_w.astype(BF16), mod_b.reshape(depth, 1, n))


def _rope(x, cos, sin_signed):
    w = x.shape[1]
    lane = lax.broadcasted_iota(jnp.int32, x.shape, 1)
    first_half = (lane % (2 * ROPE_PAIRS)) < ROPE_PAIRS
    partner = jnp.where(first_half, pltpu.roll(x, w - ROPE_PAIRS, 1), pltpu.roll(x, ROPE_PAIRS, 1))
    return x * cos + partner * sin_signed


class _Rows(NamedTuple):
    read: Callable
    specs: list
    args: list
    shape: tuple


def _stream_of(ctx, x):
    b, c, d = ctx.shape
    tm = ROW_TILE
    ctx_tiles = c // tm

    def read(refs):
        c_ref, x_ref = refs
        return jnp.where(pl.program_id(1) < ctx_tiles, c_ref[0], x_ref[0])

    specs = [pl.BlockSpec((1, tm, d), lambda i, t: (i, jnp.minimum(t, ctx_tiles - 1), 0)),
             pl.BlockSpec((1, tm, d), lambda i, t: (i, jnp.maximum(t - ctx_tiles, 0), 0))]
    return _Rows(read, specs, [ctx, x], (b, c + x.shape[1], d))


def _rows_from(xs, tile_offset):
    b, n, d = xs.shape
    spec = pl.BlockSpec((1, ROW_TILE, d), lambda i, t: (i, t + tile_offset, 0))
    return _Rows(lambda refs: refs[0][0], [spec], [xs], (b, n - tile_offset * ROW_TILE, d))


def _qkv_kernel(*refs, nq, nk, read_x, n_x):
    x = read_x(refs[:n_x])
    (mod_ref, g_ref, w_ref, qg_ref, kg_ref, cos_ref, sin_ref, seg_ref, ek_ref, ev_ref,
     q_ref, k_ref, v_ref) = refs[n_x:]
    mod = mod_ref[0, 0]
    h = _norm_mod(x, g_ref[...], mod[0:1], mod[1:2]).astype(BF16)
    qkv = _dot(h, w_ref[...])
    q, k, v = qkv[:, :nq], qkv[:, nq:nq + nk], qkv[:, nq + nk:]
    cos2, sin2 = cos_ref[...], sin_ref[...]

    def head_norm_rope(z, gain):
        reps = z.shape[1] // V7X_LANES
        cos = jnp.concatenate([cos2] * reps, axis=1)
        sin = jnp.concatenate([sin2] * reps, axis=1)
        ss = _group_sum(z * z, seg_ref)
        zn = z * lax.rsqrt(ss * (1.0 / HEAD_DIM) + NORM_EPS) * gain
        return _rope(zn, cos, sin)

    qn = head_norm_rope(q, qg_ref[...]) * (HEAD_DIM ** -0.5 * LOG2_E)
    q_ref[0] = qn.astype(BF16)
    kn = head_norm_rope(k, kg_ref[...]).astype(BF16)
    k_t = _dot(kn, ek_ref[...]).astype(BF16)
    v_bf = v.astype(BF16)
    ones = jnp.ones((ATTN_V_ROWS - HEAD_DIM, v_bf.shape[0]), BF16)
    shift_lane = lax.broadcasted_iota(jnp.int32, (v_bf.shape[0], V7X_LANES), 1) == HEAD_DIM
    for j in range(N_KV_HEADS):
        k_ref[0, j] = jnp.where(shift_lane, jnp.ones((), BF16), k_t[:, j * V7X_LANES:(j + 1) * V7X_LANES])
        v_ref[0, j, 0:HEAD_DIM, :] = _dot_nt(ev_ref[j], v_bf).astype(BF16)
        v_ref[0, j, HEAD_DIM:, :] = ones


def _head_select(n_heads, rows):
    sel = np.zeros((n_heads, rows, n_heads * HEAD_DIM), np.float32)
    for j in range(n_heads):
        sel[j, np.arange(HEAD_DIM), j * HEAD_DIM + np.arange(HEAD_DIM)] = 1.0
    return sel


def _qkv_project(rows, mods, g1, wqkv, q_gain, k_gain, cos2, sin2, seg, ctx_tiles):
    b, nt, d = rows.shape
    nq, nk = N_HEADS * HEAD_DIM, N_KV_HEADS * HEAD_DIM
    tm = ROW_TILE
    ek = np.zeros((nk, N_KV_HEADS * V7X_LANES), np.float32)
    for j in range(N_KV_HEADS):
        ek[j * HEAD_DIM + np.arange(HEAD_DIM), j * V7X_LANES + np.arange(HEAD_DIM)] = 1.0
    ev = _head_select(N_KV_HEADS, HEAD_DIM)
    kern = functools.partial(_qkv_kernel, nq=nq, nk=nk, read_x=rows.read, n_x=len(rows.specs))
    return pl.pallas_call(
        kern,
        out_shape=(
            jax.ShapeDtypeStruct((b, nt, nq), BF16),
            jax.ShapeDtypeStruct((b, N_KV_HEADS, nt, V7X_LANES), BF16),
            jax.ShapeDtypeStruct((b, N_KV_HEADS, ATTN_V_ROWS, nt), BF16),
        ),
        grid=(b, nt // tm),
        in_specs=rows.specs + [
            pl.BlockSpec((1, 1, N_MOD_ROWS, d), lambda i, t: (i, jnp.where(t < ctx_tiles, 0, 1), 0, 0)),
            _const_spec((1, d)),
            _const_spec((d, nq + 2 * nk)),
            _const_spec((1, nq)),
            _const_spec((1, nk)),
            pl.BlockSpec((tm, V7X_LANES), lambda i, t: (t, 0)),
            pl.BlockSpec((tm, V7X_LANES), lambda i, t: (t, 0)),
            _const_spec((GROUP_LANES, GROUP_LANES)),
            _const_spec(ek.shape),
            _const_spec(ev.shape),
        ],
        out_specs=(
            pl.BlockSpec((1, tm, nq), lambda i, t: (i, t, 0)),
            pl.BlockSpec((1, N_KV_HEADS, tm, V7X_LANES), lambda i, t: (i, 0, t, 0)),
            pl.BlockSpec((1, N_KV_HEADS, ATTN_V_ROWS, tm), lambda i, t: (i, 0, 0, t)),
        ),
        compiler_params=_cparams(2),
        name="qkv_project",
    )(*rows.args, mods, g1.reshape(1, d), wqkv.astype(BF16),
      jnp.tile(q_gain, N_HEADS).reshape(1, nq), jnp.tile(k_gain, N_KV_HEADS).reshape(1, nk),
      cos2, sin2, seg, jnp.asarray(ek, BF16), jnp.asarray(ev, BF16))


def _attn_kernel(*refs, n_sub, tk, n_chunks, unroll):
    bound_ref = refs[0]
    q_refs = refs[1:1 + n_sub]
    k_ref, vt_ref, o_ref, qt_ref, m_ref, acc_ref, sa_ref, sb_ref, p_ref = refs[1 + n_sub:]
    ts = q_refs[0].shape[1]
    width = qt_ref.shape[1]
    for s, q_ref in enumerate(q_refs):
        q_t = q_ref[0].astype(F32).T
        for g in range(KV_GROUP):
            col = (s * KV_GROUP + g) * ts
            qt_ref[0:HEAD_DIM, col:col + ts] = q_t[g * HEAD_DIM:(g + 1) * HEAD_DIM].astype(BF16)
    qt_ref[HEAD_DIM:, :] = jnp.full((qt_ref.shape[0] - HEAD_DIM, width), bound_ref[0], F32).astype(BF16)
    acc_ref[...] = jnp.zeros(acc_ref.shape, F32)

    def chunk(c, n=1):
        return pl.ds(c * tk if isinstance(c, int) else pl.multiple_of(c * tk, n * tk), n * tk)

    def scores(c):
        return _dot(k_ref[0, 0, chunk(c), :], qt_ref[...])

    @pl.when(bound_ref[1] > 0.5)
    def _():
        def accumulate(c0, n):
            for u in range(n):
                p_ref[u * tk:(u + 1) * tk, :] = jnp.exp2(scores(c0 + u)).astype(BF16)
            acc_ref[...] += _dot(vt_ref[0, 0, :, chunk(c0, n)], p_ref[0:n * tk, :])

        per_trip = p_ref.shape[0] // tk

        def trip(j, carry):
            accumulate(per_trip * j, per_trip)
            return carry

        n_trips, rem = n_chunks // per_trip, n_chunks % per_trip
        if n_trips:
            lax.fori_loop(0, n_trips, trip, 0)
        if rem:
            accumulate(n_trips * per_trip, rem)

    @pl.when(bound_ref[1] <= 0.5)
    def _():
        _attn_online_softmax(scores, chunk, vt_ref, m_ref, acc_ref, sa_ref, sb_ref, p_ref,
                             n_chunks=n_chunks, unroll=unroll)

    acc = acc_ref[...]
    ot = acc[0:HEAD_DIM] / acc[HEAD_DIM:HEAD_DIM + 1]
    for s in range(n_sub):
        heads = [ot[:, (s * KV_GROUP + g) * ts:(s * KV_GROUP + g + 1) * ts] for g in range(KV_GROUP)]
        o_ref[0, s * ts:(s + 1) * ts, :] = jnp.concatenate(heads, axis=0).T.astype(o_ref.dtype)


def _attn_online_softmax(scores, chunk, vt_ref, m_ref, acc_ref, sa_ref, sb_ref, p_ref, *, n_chunks, unroll):
    m_ref[...] = jnp.full(m_ref.shape, -jnp.inf, F32)
    tk = sa_ref.shape[0]

    def update(c, s_ref):
        vtc = vt_ref[0, 0, :, chunk(c)]
        alphas = []
        for cb in range(s_ref.shape[1] // V7X_LANES):
            cols = slice(cb * V7X_LANES, (cb + 1) * V7X_LANES)
            st = s_ref[:, cols]
            m_prev = m_ref[:, cols]
            m_new = jnp.maximum(m_prev, jnp.max(st, axis=0, keepdims=True))
            alphas.append(jnp.exp2(m_prev - m_new))
            p_ref[0:tk, cols] = jnp.exp2(st - m_new).astype(BF16)
            m_ref[:, cols] = m_new
        alpha = jnp.concatenate(alphas, axis=1)
        acc_ref[...] = alpha * acc_ref[...] + _dot(vtc, p_ref[0:tk, :])

    slots = (sa_ref, sb_ref)
    sa_ref[...] = scores(0)

    def body(j, carry):
        c0 = unroll * j
        for u in range(unroll):
            slots[(u + 1) % 2][...] = scores(c0 + u + 1)
            update(c0 + u, slots[u % 2])
        return carry

    if n_chunks > 1:
        lax.fori_loop(0, (n_chunks - 1) // unroll, body, 0)
    update(n_chunks - 1, slots[0])


def _attention_call(bound, q, k, vt, *, row0, n_rows, n_keys, n_sub, name):
    b, _, nq = q.shape
    ts = ROW_TILE
    tk = ROW_TILE
    tq = n_sub * ts
    n_chunks = n_keys // tk
    assert n_rows % tq == 0 and row0 % ts == 0 and n_keys % tk == 0 and n_chunks % 2 == 1
    unroll = max(u for u in (2, 4) if (n_chunks - 1) % u == 0)
    kern = functools.partial(_attn_kernel, n_sub=n_sub, tk=tk, n_chunks=n_chunks, unroll=unroll)
    width = n_sub * KV_GROUP * ts

    def q_spec(s):
        return pl.BlockSpec((1, ts, GROUP_LANES), lambda i, j, t: (i, row0 // ts + n_sub * t + s, j))

    return pl.pallas_call(
        kern,
        out_shape=jax.ShapeDtypeStruct((b, n_rows, nq), BF16),
        grid=(b, N_KV_HEADS, n_rows // tq),
        in_specs=[pl.BlockSpec(memory_space=pltpu.SMEM)] + [q_spec(s) for s in range(n_sub)] + [
            pl.BlockSpec((1, 1, n_keys, V7X_LANES), lambda i, j, t: (i, j, 0, 0)),
            pl.BlockSpec((1, 1, ATTN_V_ROWS, n_keys), lambda i, j, t: (i, j, 0, 0)),
        ],
        out_specs=pl.BlockSpec((1, tq, GROUP_LANES), lambda i, j, t: (i, t, j)),
        scratch_shapes=[
            pltpu.VMEM((V7X_LANES, width), BF16),
            pltpu.VMEM((1, width), F32),
            pltpu.VMEM((ATTN_V_ROWS, width), F32),
            pltpu.VMEM((tk, width), F32),
            pltpu.VMEM((tk, width), F32),
            pltpu.VMEM((min(ATTN_CHUNKS_PER_PV, n_chunks) * tk, width), BF16),
        ],
        compiler_params=_cparams(3),
        name=name,
    )(bound, *([q] * n_sub), k, vt)


def _score_bound(q_gain, k_gain):
    bound = (HEAD_DIM * (HEAD_DIM ** -0.5 * LOG2_E) * 1.02) * jnp.max(jnp.abs(q_gain)) * jnp.max(jnp.abs(k_gain))
    bound = bound.astype(BF16).astype(F32)
    return jnp.stack([-bound, (2.0 * bound <= EXP2_SAFE_RANGE).astype(F32)])


def _attention(q, k, vt, bound, ctx_len):
    nt = q.shape[1]
    n_lat = nt - ctx_len
    n_sub = max(s for s in (1, 2, 4) if n_lat % (s * ROW_TILE) == 0)
    o_ctx = _attention_call(bound, q, k, vt, row0=0, n_rows=ctx_len, n_keys=ctx_len, n_sub=1,
                            name="flash_attention_ctx")
    o_lat = _attention_call(bound, q, k, vt, row0=ctx_len, n_rows=n_lat, n_keys=nt, n_sub=n_sub,
                            name="flash_attention")
    return o_ctx, o_lat


def _out_ffn_kernel(*refs, final_norm, read_mix, n_mix, read_x, n_x):
    y = read_mix(refs[:n_mix])
    x = read_x(refs[n_mix:n_mix + n_x])
    mod_ref, g_ref, wo_ref, wg_ref, wu_ref, wd_ref, fg_ref, o_ref = refs[n_mix + n_x:]
    mod = mod_ref[0, 0]
    x1 = x + mod[2:3] * _dot(y, wo_ref[...])
    h2 = _norm_mod(x1, g_ref[...], mod[3:4], mod[4:5]).astype(BF16)
    a = _dot(h2, wg_ref[...])
    u = _dot(h2, wu_ref[...])
    hid = (a * jax.nn.sigmoid(a) * u).astype(BF16)
    x2 = x1 + mod[5:6] * _dot(hid, wd_ref[...])
    if final_norm:
        ms = jnp.mean(x2 * x2, axis=-1, keepdims=True)
        x2 = x2 * lax.rsqrt(ms + NORM_EPS) * fg_ref[...]
    o_ref[0] = x2


def _attention_mixer(o_ctx, o_lat):
    d = o_lat.shape[2]
    tm = ROW_TILE
    ctx_tiles = o_ctx.shape[1] // tm

    def read(refs):
        yc_ref, yl_ref = refs
        return jnp.where(pl.program_id(1) < ctx_tiles, yc_ref[0], yl_ref[0])

    specs = [pl.BlockSpec((1, tm, d), lambda i, t: (i, jnp.minimum(t, ctx_tiles - 1), 0)),
             pl.BlockSpec((1, tm, d), lambda i, t: (i, jnp.maximum(t - ctx_tiles, 0), 0))]
    return _Rows(read, specs, [o_ctx, o_lat], (o_lat.shape[0], o_ctx.shape[1] + o_lat.shape[1], d))


def _out_ffn(mix, rows, mods, g2, wo, wg, wu, wd, final_g, *, first_tile, ctx_tiles, final_norm):
    b, n, d = rows.shape
    assert mix.shape == rows.shape
    f = wg.shape[1]
    tm = ROW_TILE
    kern = functools.partial(_out_ffn_kernel, final_norm=final_norm, read_mix=mix.read,
                             n_mix=len(mix.specs), read_x=rows.read, n_x=len(rows.specs))

    def mod_sel(i, t):
        return (i, jnp.where(t + first_tile < ctx_tiles, 0, 1), 0, 0)

    return pl.pallas_call(
        kern,
        out_shape=jax.ShapeDtypeStruct((b, n, d), F32),
        grid=(b, n // tm),
        in_specs=mix.specs + rows.specs + [
            pl.BlockSpec((1, 1, N_MOD_ROWS, d), mod_sel),
            _const_spec((1, d)),
            _const_spec((d, d)),
            _const_spec((d, f)),
            _const_spec((d, f)),
            _const_spec((f, d)),
            _const_spec((1, d)),
        ],
        out_specs=pl.BlockSpec((1, tm, d), lambda i, t: (i, t, 0)),
        compiler_params=_cparams(2),
        name="out_ffn_final" if final_norm else "out_ffn",
    )(*mix.args, *rows.args, mods, g2.reshape(1, d), wo.astype(BF16), wg.astype(BF16), wu.astype(BF16),
      wd.astype(BF16), final_g.reshape(1, d))


def _rwkv_proj_kernel(x_ref, xp_ref, xn_ref, mod_ref, g_ref, mix_ref, wr_ref, wk_ref, wv_ref,
                      w1_ref, w2_ref, a1_ref, a2_ref, g1_ref, g2_ref, w0_ref, a0_ref, kk_ref, ka_ref,
                      seg_ref,
                      r_ref, v_ref, kkn_ref, gate_ref, lw0_ref, lw1_ref, kd0_ref, kd1_ref,
                      b0_ref, b1_ref, *, tm, seq_starts, seq_ends):
    t = pl.program_id(1)
    mod = mod_ref[0, 0]
    g = g_ref[...]
    h = _norm_mod(x_ref[0], g, mod[0:1], mod[1:2])
    h_prev = _norm_mod(xp_ref[0], g, mod[0:1], mod[1:2])[7:8]
    h_next = _norm_mod(xn_ref[0], g, mod[0:1], mod[1:2])[0:1]
    row = lax.broadcasted_iota(jnp.int32, h.shape, 0)
    pos = row + t * tm
    at_start = functools.reduce(jnp.logical_or, [pos == s for s in seq_starts])
    at_end = functools.reduce(jnp.logical_or, [pos == e for e in seq_ends])
    before = jnp.where(row == 0, h_prev, pltpu.roll(h, 1, 0))
    after = jnp.where(row == tm - 1, h_next, pltpu.roll(h, tm - 1, 0))
    before = jnp.where(at_start, 0.0, before)
    after = jnp.where(at_end, 0.0, after)
    xx = 0.5 * (before + after) - h
    mix = mix_ref[...]

    def lerp(j):
        return (h + xx * mix[j:j + 1]).astype(BF16)

    r_ref[0] = _dot(lerp(0), wr_ref[...]).astype(r_ref.dtype)
    k = _dot(lerp(2), wk_ref[...])
    v_ref[0] = _dot(lerp(3), wv_ref[...]).astype(v_ref.dtype)
    kkr = k * kk_ref[...]
    ss = _group_sum(kkr * kkr, seg_ref)
    kkn = kkr * lax.rsqrt(jnp.maximum(ss, 1e-24))
    kkn_ref[0] = kkn.astype(kkn_ref.dtype)
    xw, xa = lerp(1), lerp(4)
    ka = ka_ref[...]
    for d, (lw_ref, kd_ref, b_ref) in enumerate(((lw0_ref, kd0_ref, b0_ref), (lw1_ref, kd1_ref, b1_ref))):
        z = w0_ref[d:d + 1] + _dot(jnp.tanh(_dot(xw, w1_ref[d])).astype(BF16), w2_ref[d])
        lw_ref[0] = -DECAY_SCALE * jax.nn.sigmoid(z)
        a = jax.nn.sigmoid(a0_ref[d:d + 1] + _dot(_dot(xa, a1_ref[d]).astype(BF16), a2_ref[d]))
        kd_ref[0] = (k * (1.0 + (a - 1.0) * ka)).astype(kd_ref.dtype)
        b_ref[0] = (kkn * a).astype(b_ref.dtype)
    gate = _dot(jax.nn.sigmoid(_dot(lerp(5), g1_ref[...])).astype(BF16), g2_ref[...])
    gate_ref[0] = gate.astype(gate_ref.dtype)


def _rwkv_project(xs, mods, g1n, mix, w_rkv, w0, w1, w2, a0, a1, a2, gw1, gw2, k_k, k_a, seg,
                  ctx_len):
    b, nt, d = xs.shape
    tm = ROW_TILE
    halo = 8
    per = tm // halo
    n_halo = nt // halo
    kern = functools.partial(_rwkv_proj_kernel, tm=tm, seq_starts=(0, ctx_len),
                             seq_ends=(ctx_len - 1, nt - 1))
    tok = pl.BlockSpec((1, tm, d), lambda i, t: (i, t, 0))
    bf = lambda w: w.astype(BF16)
    outs = pl.pallas_call(
        kern,
        out_shape=tuple(jax.ShapeDtypeStruct((b, nt, d), dt)
                        for dt in (BF16, BF16, BF16, BF16, F32, F32, BF16, BF16, BF16, BF16)),
        grid=(b, nt // tm),
        in_specs=[
            tok,
            pl.BlockSpec((1, halo, d), lambda i, t: (i, jnp.maximum(t * per - 1, 0), 0)),
            pl.BlockSpec((1, halo, d), lambda i, t: (i, jnp.minimum((t + 1) * per, n_halo - 1), 0)),
            pl.BlockSpec((1, 1, N_MOD_ROWS, d), lambda i, t: (i, jnp.where(t * tm < ctx_len, 0, 1), 0, 0)),
            _const_spec((1, d)),
            _const_spec((N_MOD_ROWS, d)),
            _const_spec((d, d)), _const_spec((d, d)), _const_spec((d, d)),
            _const_spec(w1.shape), _const_spec(w2.shape), _const_spec(a1.shape), _const_spec(a2.shape),
            _const_spec(gw1.shape), _const_spec(gw2.shape),
            _const_spec((2, d)), _const_spec((2, d)), _const_spec((1, d)), _const_spec((1, d)),
            _const_spec((GROUP_LANES, GROUP_LANES)),
        ],
        out_specs=tuple(tok for _ in range(10)),
        compiler_params=_cparams(2),
        name="rwkv_project",
    )(xs, xs, xs, mods, g1n.reshape(1, d),
      jnp.concatenate([mix, jnp.zeros((N_MOD_ROWS - mix.shape[0], d), F32)], axis=0),
      bf(w_rkv[0]), bf(w_rkv[1]), bf(w_rkv[2]), bf(w1), bf(w2), bf(a1), bf(a2), bf(gw1), bf(gw2),
      w0, a0, k_k.reshape(1, d), k_a.reshape(1, d), seg)
    return outs


def _block_diag(x_bf, mask):
    return jnp.where(mask, jnp.concatenate([x_bf] * HEADS_PER_GROUP, axis=0), jnp.zeros((), BF16))


def _cumsum_rows(x, reverse):
    n = x.shape[0]
    row = lax.broadcasted_iota(jnp.int32, x.shape, 0)
    shift = 1
    while shift < n:
        if reverse:
            x = x + jnp.where(row < n - shift, pltpu.roll(x, n - shift, 0), 0.0)
        else:
            x = x + jnp.where(row >= shift, pltpu.roll(x, shift, 0), 0.0)
        shift *= 2
    return x


def _block_transpose(x):
    xt = x.T
    return jnp.concatenate([xt[j * RWKV_HEAD:(j + 1) * RWKV_HEAD] for j in range(HEADS_PER_GROUP)], axis=1)


def _wkv_masks(reverse):
    c = WKV_CHUNK
    rows = lax.broadcasted_iota(jnp.int32, (c, GROUP_LANES), 0)
    cols = lax.broadcasted_iota(jnp.int32, (c, GROUP_LANES), 1) % RWKV_HEAD
    if reverse:
        return cols > rows, cols >= rows
    return cols < rows, cols <= rows


def _bd_mask():
    bd_r = lax.broadcasted_iota(jnp.int32, (GROUP_LANES, GROUP_LANES), 0) // RWKV_HEAD
    bd_c = lax.broadcasted_iota(jnp.int32, (GROUP_LANES, GROUP_LANES), 1) // RWKV_HEAD
    return bd_r == bd_c


def _wkv_prepare(chains):
    c = WKV_CHUNK
    n = len(chains)
    rng = range(n)
    rows = lax.broadcasted_iota(jnp.int32, (c, GROUP_LANES), 0)
    cols = lax.broadcasted_iota(jnp.int32, (c, GROUP_LANES), 1) % RWKV_HEAD
    eye = (cols == rows).astype(F32)
    bmask = _bd_mask()
    masks = {rev: _wkv_masks(rev) for rev in sorted({ch[6] for ch in chains})}
    strict = [masks[ch[6]][0] for ch in chains]
    incl = [masks[ch[6]][1] for ch in chains]
    lw, kd, bb, kk, v, r = ([ch[i] for ch in chains] for i in range(6))
    bd = lambda x: _block_diag(x.astype(BF16), bmask)

    cs = [_cumsum_rows(lw[i], chains[i][6]) for i in rng]
    total = [cs[i][0:1] if chains[i][6] else cs[i][c - 1:c] for i in rng]
    a_s = [-kk[i] * jnp.exp(cs[i] - lw[i]) for i in rng]
    g_inv = [jnp.exp(-cs[i]) for i in rng]
    b_s = [bb[i] * g_inv[i] for i in rng]
    k_s = [kd[i] * g_inv[i] for i in rng]
    r_s = [r[i] * jnp.exp(cs[i]) for i in rng]
    g_rest = [jnp.exp(total[i] - cs[i]) for i in rng]
    b_e = [(bb[i] * g_rest[i]).astype(BF16) for i in rng]
    k_e = [(kd[i] * g_rest[i]).astype(BF16) for i in rng]
    g_end = [jnp.exp(total[i]) for i in rng]

    lhs = [jnp.concatenate([a_s[i], r_s[i]], axis=0).astype(BF16) for i in rng]
    pb = [_dot_nt(lhs[i], bd(b_s[i])) for i in rng]
    pk = [_dot_nt(lhs[i], bd(k_s[i])) for i in rng]
    l_ab = [jnp.where(strict[i], pb[i][:c], 0.0) for i in rng]
    m_rb = [jnp.where(incl[i], pb[i][c:], 0.0).astype(BF16) for i in rng]
    l_ak = [jnp.where(strict[i], pk[i][:c], 0.0) for i in rng]
    m_rk = [jnp.where(incl[i], pk[i][c:], 0.0) for i in rng]

    blk_r, blk_c = rows, cols

    def coupling(i, w):
        pair = ((blk_r // (2 * w)) == (blk_c // (2 * w))) & ((blk_r // w) != (blk_c // w))
        return jnp.where(pair, l_ab[i], 0.0)

    tmat = [eye + coupling(i, 1) for i in rng]
    w = 2
    while w < c:
        ed = [_dot(coupling(i, w).astype(BF16), bd(tmat[i])) for i in rng]
        tmat = [tmat[i] + _dot(tmat[i].astype(BF16), bd(ed[i])) for i in rng]
        w *= 2
    t_bf = [tmat[i].astype(BF16) for i in rng]

    zv = [_dot(jnp.concatenate([l_ak[i], m_rk[i]], axis=0).astype(BF16), bd(v[i])) for i in rng]
    a_hat = [_dot(t_bf[i], bd(a_s[i])) for i in rng]
    u_til = [_dot(t_bf[i], bd(zv[i][:c])) for i in rng]
    r_hat = [r_s[i] + _dot(m_rb[i], bd(a_hat[i])) for i in rng]
    y_til = [zv[i][c:] + _dot(m_rb[i], bd(u_til[i])) for i in rng]

    au_t = [jnp.concatenate([_block_transpose(a_hat[i]), _block_transpose(u_til[i])], axis=0) for i in rng]
    gh = [_dot(au_t[i].astype(BF16), bd(b_e[i])) for i in rng]
    g_mat = [gh[i][:c] for i in rng]
    h_mat = [gh[i][c:] + _dot(_block_transpose(v[i].astype(F32)).astype(BF16), bd(k_e[i])) for i in rng]
    return [(r_hat[i].astype(BF16), y_til[i], g_mat[i].astype(BF16), h_mat[i], g_end[i]) for i in rng]


def _wkv_kernel(lwf, kdf, bf_, kkf, vf, rf, lwr, kdr, br, kkr, vr, rr, yf_ref, yr_ref, s_ref):
    @pl.when(pl.program_id(1) == 0)
    def _():
        s_ref[...] = jnp.zeros(s_ref.shape, F32)

    c = WKV_CHUNK
    n_sub = lwf.shape[1] // c
    n_groups = lwf.shape[2] // GROUP_LANES
    dirs = (((lwf, kdf, bf_, kkf, vf, rf), yf_ref), ((lwr, kdr, br, kkr, vr, rr), yr_ref))
    chains, keys = [], []
    for u in range(n_sub):
        rs = slice(u * c, (u + 1) * c)
        for j in range(n_groups):
            sl = slice(j * GROUP_LANES, (j + 1) * GROUP_LANES)
            for d, (refs, _) in enumerate(dirs):
                lw, kd, bb, kk, v, r = (ref[0, rs, sl] for ref in refs)
                chains.append((lw, kd.astype(F32), bb.astype(F32), kk.astype(F32), v, r.astype(F32),
                               d == 1))
                keys.append((u, j, d))
    prepared = dict(zip(keys, _wkv_prepare(chains)))

    bmask = _bd_mask()
    for j in range(n_groups):
        sl = slice(j * GROUP_LANES, (j + 1) * GROUP_LANES)
        for d, (_, y_ref) in enumerate(dirs):
            s = s_ref[d, :, sl]
            for u in (range(n_sub) if d == 0 else reversed(range(n_sub))):
                r_hat, y_til, g_mat, h_mat, g_end = prepared[(u, j, d)]
                s_bf = s.astype(BF16)
                y = y_til + _dot_nt(r_hat, _block_diag(s_bf, bmask))
                y_ref[0, u * c:(u + 1) * c, sl] = y.astype(y_ref.dtype)
                s = s * g_end + _dot(s_bf, _block_diag(g_mat, bmask)) + h_mat
            s_ref[d, :, sl] = s


def _wkv_scan(lw0, lw1, kd0, kd1, b0, b1, kk, v, r, ctx_len):
    b, nt, d = v.shape
    c = WKV_CHUNKS_PER_STEP * WKV_CHUNK
    assert ctx_len % c == 0 and nt % c == 0
    n_steps = nt // c
    ctx_chunks = ctx_len // c

    def fwd(i, s):
        return (i, s, 0)

    def rev(i, s):
        return (i, jnp.where(s < ctx_chunks, ctx_chunks - 1 - s, n_steps - 1 + ctx_chunks - s), 0)

    blk_f = pl.BlockSpec((1, c, d), fwd)
    blk_r = pl.BlockSpec((1, c, d), rev)
    return pl.pallas_call(
        _wkv_kernel,
        out_shape=(jax.ShapeDtypeStruct((b, nt, d), BF16), jax.ShapeDtypeStruct((b, nt, d), BF16)),
        grid=(b, n_steps),
        in_specs=[blk_f] * 6 + [blk_r] * 6,
        out_specs=(blk_f, blk_r),
        scratch_shapes=[pltpu.VMEM((2, RWKV_HEAD, d), F32)],
        compiler_params=_cparams(2),
        name="wkv_scan",
    )(lw0, kd0, b0, kk, v, r, lw1, kd1, b1, kk, v, r)


def _read_rwkv_mix(refs):
    yf_ref, yr_ref, r_ref, v_ref, kd0_ref, kd1_ref, gate_ref, rk_ref, lg_ref, lb_ref, seg_ref = refs
    wkv = yf_ref[0].astype(F32) + yr_ref[0].astype(F32)
    inv_n = 1.0 / RWKV_HEAD
    mu = _group_sum(wkv, seg_ref, two_pass=True) * inv_n
    dev = wkv - mu
    var = _group_sum(dev * dev, seg_ref) * inv_n
    gn = dev * lax.rsqrt(var + GN_EPS) * lg_ref[...] + lb_ref[...]
    rk = r_ref[0].astype(F32) * rk_ref[...]
    kd_sum = kd0_ref[0].astype(F32) + kd1_ref[0].astype(F32)
    bonus = _group_sum(rk * kd_sum, seg_ref) * v_ref[0].astype(F32)
    return ((gn + bonus) * gate_ref[0].astype(F32)).astype(BF16)


def _rwkv_mixer(yf, yr, r, v, kd0, kd1, gate, r_k, ln_g, ln_b, seg, ctx_len):
    b, nt, d = v.shape
    tm = ROW_TILE
    off = ctx_len // tm
    tok = pl.BlockSpec((1, tm, d), lambda i, t: (i, t + off, 0))
    specs = [tok] * 7 + [_const_spec((1, d))] * 3 + [_const_spec((GROUP_LANES, GROUP_LANES))]
    args = [yf, yr, r, v, kd0, kd1, gate, r_k.reshape(1, d), ln_g.reshape(1, d), ln_b.reshape(1, d), seg]
    return _Rows(_read_rwkv_mix, specs, args, (b, nt - ctx_len, d))


def _rope_tables(ctx_len, seq_len):
    f32 = np.float32
    rows = seq_len // GRID_W
    row = np.repeat(np.arange(rows, dtype=f32), GRID_W)
    col = np.tile(np.arange(GRID_W, dtype=f32), rows)
    inv_freq = np.power(f32(ROPE_THETA), -np.arange(ROPE_PAIRS, dtype=f32) / f32(ROPE_PAIRS)).astype(f32)
    row_ang, col_ang = row[:, None] * inv_freq, col[:, None] * inv_freq
    ang = np.concatenate([row_ang, row_ang, col_ang, col_ang], axis=1).astype(np.float64)
    cos = np.concatenate([np.ones((ctx_len, HEAD_DIM)), np.cos(ang)], axis=0)
    sin = np.concatenate([np.zeros((ctx_len, HEAD_DIM)), np.sin(ang)], axis=0)
    sign = np.concatenate([-np.ones(ROPE_PAIRS), np.ones(ROPE_PAIRS)] * 2)
    reps = V7X_LANES // HEAD_DIM
    return (jnp.asarray(np.tile(cos, (1, reps)), F32), jnp.asarray(np.tile(sin * sign, (1, reps)), F32))


def kernel(x, c, ctx, c_ctx, mod_w, mod_b, norm1_g, norm2_g, ffn_wg, ffn_wu, ffn_wd, attn_wqkv,
           attn_q_gain, attn_k_gain, attn_wo, rwkv_mix, rwkv_wrkv, rwkv_w0, rwkv_w1, rwkv_w2, rwkv_a0,
           rwkv_a1, rwkv_a2, rwkv_g1, rwkv_g2, rwkv_k_k, rwkv_k_a, rwkv_r_k, rwkv_ln_g, rwkv_ln_b,
           rwkv_wo, final_g):
    b, seq_len, d = x.shape
    ctx_len = ctx.shape[1]
    depth = mod_w.shape[0]
    assert depth == 2 and d == N_HEADS * HEAD_DIM
    assert ctx_len % ROW_TILE == 0 and seq_len % ROW_TILE == 0 and seq_len % GRID_W == 0
    ctx_tiles = ctx_len // ROW_TILE

    n_rows = -(-(b + 1) // 8) * 8
    c_rows = jnp.concatenate([c, c_ctx[None], jnp.zeros((n_rows - b - 1, d), F32)], axis=0)
    m_all = _modulation(c_rows, mod_w, mod_b).reshape(depth, n_rows, 6, d)
    pad = jnp.zeros((depth, b, N_MOD_ROWS - 6, d), F32)
    lat = jnp.concatenate([m_all[:, :b], pad], axis=2)
    con = jnp.concatenate([jnp.broadcast_to(m_all[:, b:b + 1], (depth, b, 6, d)), pad], axis=2)
    mods = jnp.stack([con, lat], axis=2)

    cos2, sin2 = _rope_tables(ctx_len, seq_len)
    seg = jnp.asarray(np.kron(np.eye(HEADS_PER_GROUP), np.ones((HEAD_DIM, HEAD_DIM))), BF16)

    stream = _stream_of(ctx, x)

    q, k, vt = _qkv_project(stream, mods[0], norm1_g[0], attn_wqkv[0], attn_q_gain[0], attn_k_gain[0],
                            cos2, sin2, seg, ctx_tiles)
    o_ctx, o_lat = _attention(q, k, vt, _score_bound(attn_q_gain[0], attn_k_gain[0]), ctx_len)
    xs = _out_ffn(_attention_mixer(o_ctx, o_lat), stream, mods[0], norm2_g[0], attn_wo[0], ffn_wg[0],
                  ffn_wu[0], ffn_wd[0], final_g, first_tile=0, ctx_tiles=ctx_tiles, final_norm=False)

    r, v, kk, gate, lw0, lw1, kd0, kd1, b0, b1 = _rwkv_project(
        xs, mods[1], norm1_g[1], rwkv_mix[0], rwkv_wrkv[0], rwkv_w0[0], rwkv_w1[0], rwkv_w2[0],
        rwkv_a0[0], rwkv_a1[0], rwkv_a2[0], rwkv_g1[0], rwkv_g2[0], rwkv_k_k[0], rwkv_k_a[0], seg,
        ctx_len)
    yf, yr = _wkv_scan(lw0, lw1, kd0, kd1, b0, b1, kk, v, r, ctx_len)
    mix = _rwkv_mixer(yf, yr, r, v, kd0, kd1, gate, rwkv_r_k[0].reshape(-1), rwkv_ln_g[0],
                      rwkv_ln_b[0], seg, ctx_len)
    return _out_ffn(mix, _rows_from(xs, ctx_tiles), mods[1], norm2_g[1], rwkv_wo[0], ffn_wg[1], ffn_wu[1],
                    ffn_wd[1], final_g, first_tile=ctx_tiles, ctx_tiles=ctx_tiles, final_norm=True)
```

```python
import functools
from typing import Callable, NamedTuple

import jax
import jax.numpy as jnp
import numpy as np
from jax import lax
from jax.experimental import pallas as pl
from jax.experimental.pallas import tpu as pltpu

F32 = jnp.float32
BF16 = jnp.bfloat16

NORM_EPS = 1e-6
GN_EPS = 64e-5
GRID_W = 64
N_HEADS = 16
N_KV_HEADS = 4
KV_GROUP = N_HEADS // N_KV_HEADS
HEAD_DIM = 64
ROPE_THETA = 10000.0
ROPE_PAIRS = HEAD_DIM // 4
RWKV_HEAD = 64
DECAY_SCALE = float(np.exp(-0.5))
LOG2_E = float(np.log2(np.e))
EXP2_SAFE_RANGE = 120.0

V7X_LANES = 128
V7X_MXU_DIM = 256
V7X_VMEM_LIMIT_BYTES = 60000 * 1024

ROW_TILE = 256
GROUP_LANES = V7X_MXU_DIM
HEADS_PER_GROUP = GROUP_LANES // HEAD_DIM
BF16_SUBLANES = 16
ATTN_V_ROWS = HEAD_DIM + BF16_SUBLANES
ATTN_CHUNKS_PER_PV = 11
WKV_CHUNK = 64
WKV_CHUNKS_PER_STEP = 2
N_MOD_ROWS = 8


def _cparams(n_axes):
    return pltpu.CompilerParams(
        dimension_semantics=("arbitrary",) * n_axes,
        vmem_limit_bytes=V7X_VMEM_LIMIT_BYTES,
    )


def _const_spec(shape):
    nd = len(shape)
    return pl.BlockSpec(shape, lambda *_: (0,) * nd, pipeline_mode=pl.Buffered(1))


def _dot(a, b):
    return jnp.dot(a, b, preferred_element_type=F32)


def _dot_nt(a, b):
    return lax.dot_general(a, b, (((1,), (1,)), ((), ())), preferred_element_type=F32)


def _norm_mod(x, g, shift, scale):
    ms = jnp.mean(x * x, axis=-1, keepdims=True)
    return (x * lax.rsqrt(ms + NORM_EPS) * g) * (1.0 + scale) + shift


def _group_sum(x, seg_ref, two_pass=False):
    seg = seg_ref[...]
    outs = []
    for j in range(x.shape[1] // GROUP_LANES):
        xs = x[:, j * GROUP_LANES:(j + 1) * GROUP_LANES]
        hi = xs.astype(BF16)
        out = _dot(hi, seg)
        if two_pass:
            out = out + _dot((xs - hi.astype(F32)).astype(BF16), seg)
        outs.append(out)
    return outs[0] if len(outs) == 1 else jnp.concatenate(outs, axis=1)


def _mod_kernel(c_ref, w_ref, b_ref, o_ref):
    c = c_ref[...]
    s = c * jax.nn.sigmoid(c)
    o_ref[0] = _dot(s.astype(BF16), w_ref[0]) + b_ref[0]


def _modulation(c_rows, mod_w, mod_b):
    depth, d, n = mod_w.shape
    rows = c_rows.shape[0]
    tn = n // 4
    return pl.pallas_call(
        _mod_kernel,
        out_shape=jax.ShapeDtypeStruct((depth, rows, n), F32),
        grid=(depth, n // tn),
        in_specs=[
            pl.BlockSpec((rows, d), lambda i, j: (0, 0)),
            pl.BlockSpec((1, d, tn), lambda i, j: (i, 0, j)),
            pl.BlockSpec((1, 1, tn), lambda i, j: (i, 0, j)),
        ],
        out_specs=pl.BlockSpec((1, rows, tn), lambda i, j: (i, 0, j)),
        compiler_params=_cparams(2),
        name="modulation",
    )(c_rows, mod_w.astype(BF16), mod_b.reshape(depth, 1, n))


def _rope(x, cos, sin_signed):
    w = x.shape[1]
    lane = lax.broadcasted_iota(jnp.int32, x.shape, 1)
    first_half = (lane % (2 * ROPE_PAIRS)) < ROPE_PAIRS
    partner = jnp.where(first_half, pltpu.roll(x, w - ROPE_PAIRS, 1), pltpu.roll(x, ROPE_PAIRS, 1))
    return x * cos + partner * sin_signed


class _Rows(NamedTuple):
    read: Callable
    specs: list
    args: list
    shape: tuple


def _stream_of(ctx, x):
    b, c, d = ctx.shape
    tm = ROW_TILE
    ctx_tiles = c // tm

    def read(refs):
        c_ref, x_ref = refs
        return jnp.where(pl.program_id(1) < ctx_tiles, c_ref[0], x_ref[0])

    specs = [pl.BlockSpec((1, tm, d), lambda i, t: (i, jnp.minimum(t, ctx_tiles - 1), 0)),
             pl.BlockSpec((1, tm, d), lambda i, t: (i, jnp.maximum(t - ctx_tiles, 0), 0))]
    return _Rows(read, specs, [ctx, x], (b, c + x.shape[1], d))


def _rows_from(xs, tile_offset):
    b, n, d = xs.shape
    spec = pl.BlockSpec((1, ROW_TILE, d), lambda i, t: (i, t + tile_offset, 0))
    return _Rows(lambda refs: refs[0][0], [spec], [xs], (b, n - tile_offset * ROW_TILE, d))


def _qkv_kernel(*refs, nq, nk, read_x, n_x):
    x = read_x(refs[:n_x])
    (mod_ref, g_ref, w_ref, qg_ref, kg_ref, cos_ref, sin_ref, seg_ref, ek_ref, ev_ref,
     q_ref, k_ref, v_ref) = refs[n_x:]
    mod = mod_ref[0, 0]
    h = _norm_mod(x, g_ref[...], mod[0:1], mod[1:2]).astype(BF16)
    qkv = _dot(h, w_ref[...])
    q, k, v = qkv[:, :nq], qkv[:, nq:nq + nk], qkv[:, nq + nk:]
    cos2, sin2 = cos_ref[...], sin_ref[...]

    def head_norm_rope(z, gain):
        reps = z.shape[1] // V7X_LANES
        cos = jnp.concatenate([cos2] * reps, axis=1)
        sin = jnp.concatenate([sin2] * reps, axis=1)
        ss = _group_sum(z * z, seg_ref)
        zn = z * lax.rsqrt(ss * (1.0 / HEAD_DIM) + NORM_EPS) * gain
        return _rope(zn, cos, sin)

    qn = head_norm_rope(q, qg_ref[...]) * (HEAD_DIM ** -0.5 * LOG2_E)
    q_ref[0] = qn.astype(BF16)
    kn = head_norm_rope(k, kg_ref[...]).astype(BF16)
    k_t = _dot(kn, ek_ref[...]).astype(BF16)
    v_bf = v.astype(BF16)
    ones = jnp.ones((ATTN_V_ROWS - HEAD_DIM, v_bf.shape[0]), BF16)
    shift_lane = lax.broadcasted_iota(jnp.int32, (v_bf.shape[0], V7X_LANES), 1) == HEAD_DIM
    for j in range(N_KV_HEADS):
        k_ref[0, j] = jnp.where(shift_lane, jnp.ones((), BF16), k_t[:, j * V7X_LANES:(j + 1) * V7X_LANES])
        v_ref[0, j, 0:HEAD_DIM, :] = _dot_nt(ev_ref[j], v_bf).astype(BF16)
        v_ref[0, j, HEAD_DIM:, :] = ones


def _head_select(n_heads, rows):
    sel = np.zeros((n_heads, rows, n_heads * HEAD_DIM), np.float32)
    for j in range(n_heads):
        sel[j, np.arange(HEAD_DIM), j * HEAD_DIM + np.arange(HEAD_DIM)] = 1.0
    return sel


def _qkv_project(rows, mods, g1, wqkv, q_gain, k_gain, cos2, sin2, seg, ctx_tiles):
    b, nt, d = rows.shape
    nq, nk = N_HEADS * HEAD_DIM, N_KV_HEADS * HEAD_DIM
    tm = ROW_TILE
    ek = np.zeros((nk, N_KV_HEADS * V7X_LANES), np.float32)
    for j in range(N_KV_HEADS):
        ek[j * HEAD_DIM + np.arange(HEAD_DIM), j * V7X_LANES + np.arange(HEAD_DIM)] = 1.0
    ev = _head_select(N_KV_HEADS, HEAD_DIM)
    kern = functools.partial(_qkv_kernel, nq=nq, nk=nk, read_x=rows.read, n_x=len(rows.specs))
    return pl.pallas_call(
        kern,
        out_shape=(
            jax.ShapeDtypeStruct((b, nt, nq), BF16),
            jax.ShapeDtypeStruct((b, N_KV_HEADS, nt, V7X_LANES), BF16),
            jax.ShapeDtypeStruct((b, N_KV_HEADS, ATTN_V_ROWS, nt), BF16),
        ),
        grid=(b, nt // tm),
        in_specs=rows.specs + [
            pl.BlockSpec((1, 1, N_MOD_ROWS, d), lambda i, t: (i, jnp.where(t < ctx_tiles, 0, 1), 0, 0)),
            _const_spec((1, d)),
            _const_spec((d, nq + 2 * nk)),
            _const_spec((1, nq)),
            _const_spec((1, nk)),
            pl.BlockSpec((tm, V7X_LANES), lambda i, t: (t, 0)),
            pl.BlockSpec((tm, V7X_LANES), lambda i, t: (t, 0)),
            _const_spec((GROUP_LANES, GROUP_LANES)),
            _const_spec(ek.shape),
            _const_spec(ev.shape),
        ],
        out_specs=(
            pl.BlockSpec((1, tm, nq), lambda i, t: (i, t, 0)),
            pl.BlockSpec((1, N_KV_HEADS, tm, V7X_LANES), lambda i, t: (i, 0, t, 0)),
            pl.BlockSpec((1, N_KV_HEADS, ATTN_V_ROWS, tm), lambda i, t: (i, 0, 0, t)),
        ),
        compiler_params=_cparams(2),
        name="qkv_project",
    )(*rows.args, mods, g1.reshape(1, d), wqkv.astype(BF16),
      jnp.tile(q_gain, N_HEADS).reshape(1, nq), jnp.tile(k_gain, N_KV_HEADS).reshape(1, nk),
      cos2, sin2, seg, jnp.asarray(ek, BF16), jnp.asarray(ev, BF16))


def _attn_kernel(*refs, n_sub, tk, n_chunks, unroll):
    bound_ref = refs[0]
    q_refs = refs[1:1 + n_sub]
    k_ref, vt_ref, o_ref, qt_ref, m_ref, acc_ref, sa_ref, sb_ref, p_ref = refs[1 + n_sub:]
    ts = q_refs[0].shape[1]
    width = qt_ref.shape[1]
    for s, q_ref in enumerate(q_refs):
        q_t = q_ref[0].astype(F32).T
        for g in range(KV_GROUP):
            col = (s * KV_GROUP + g) * ts
            qt_ref[0:HEAD_DIM, col:col + ts] = q_t[g * HEAD_DIM:(g + 1) * HEAD_DIM].astype(BF16)
    qt_ref[HEAD_DIM:, :] = jnp.full((qt_ref.shape[0] - HEAD_DIM, width), bound_ref[0], F32).astype(BF16)
    acc_ref[...] = jnp.zeros(acc_ref.shape, F32)

    def chunk(c, n=1):
        return pl.ds(c * tk if isinstance(c, int) else pl.multiple_of(c * tk, n * tk), n * tk)

    def scores(c):
        return _dot(k_ref[0, 0, chunk(c), :], qt_ref[...])

    @pl.when(bound_ref[1] > 0.5)
    def _():
        def accumulate(c0, n):
            for u in range(n):
                p_ref[u * tk:(u + 1) * tk, :] = jnp.exp2(scores(c0 + u)).astype(BF16)
            acc_ref[...] += _dot(vt_ref[0, 0, :, chunk(c0, n)], p_ref[0:n * tk, :])

        per_trip = p_ref.shape[0] // tk

        def trip(j, carry):
            accumulate(per_trip * j, per_trip)
            return carry

        n_trips, rem = n_chunks // per_trip, n_chunks % per_trip
        if n_trips:
            lax.fori_loop(0, n_trips, trip, 0)
        if rem:
            accumulate(n_trips * per_trip, rem)

    @pl.when(bound_ref[1] <= 0.5)
    def _():
        _attn_online_softmax(scores, chunk, vt_ref, m_ref, acc_ref, sa_ref, sb_ref, p_ref,
                             n_chunks=n_chunks, unroll=unroll)

    acc = acc_ref[...]
    ot = acc[0:HEAD_DIM] / acc[HEAD_DIM:HEAD_DIM + 1]
    for s in range(n_sub):
        heads = [ot[:, (s * KV_GROUP + g) * ts:(s * KV_GROUP + g + 1) * ts] for g in range(KV_GROUP)]
        o_ref[0, s * ts:(s + 1) * ts, :] = jnp.concatenate(heads, axis=0).T.astype(o_ref.dtype)


def _attn_online_softmax(scores, chunk, vt_ref, m_ref, acc_ref, sa_ref, sb_ref, p_ref, *, n_chunks, unroll):
    m_ref[...] = jnp.full(m_ref.shape, -jnp.inf, F32)
    tk = sa_ref.shape[0]

    def update(c, s_ref):
        vtc = vt_ref[0, 0, :, chunk(c)]
        alphas = []
        for cb in range(s_ref.shape[1] // V7X_LANES):
            cols = slice(cb * V7X_LANES, (cb + 1) * V7X_LANES)
            st = s_ref[:, cols]
            m_prev = m_ref[:, cols]
            m_new = jnp.maximum(m_prev, jnp.max(st, axis=0, keepdims=True))
            alphas.append(jnp.exp2(m_prev - m_new))
            p_ref[0:tk, cols] = jnp.exp2(st - m_new).astype(BF16)
            m_ref[:, cols] = m_new
        alpha = jnp.concatenate(alphas, axis=1)
        acc_ref[...] = alpha * acc_ref[...] + _dot(vtc, p_ref[0:tk, :])

    slots = (sa_ref, sb_ref)
    sa_ref[...] = scores(0)

    def body(j, carry):
        c0 = unroll * j
        for u in range(unroll):
            slots[(u + 1) % 2][...] = scores(c0 + u + 1)
            update(c0 + u, slots[u % 2])
        return carry

    if n_chunks > 1:
        lax.fori_loop(0, (n_chunks - 1) // unroll, body, 0)
    update(n_chunks - 1, slots[0])


def _attention_call(bound, q, k, vt, *, row0, n_rows, n_keys, n_sub, name):
    b, _, nq = q.shape
    ts = ROW_TILE
    tk = ROW_TILE
    tq = n_sub * ts
    n_chunks = n_keys // tk
    assert n_rows % tq == 0 and row0 % ts == 0 and n_keys % tk == 0 and n_chunks % 2 == 1
    unroll = max(u for u in (2, 4) if (n_chunks - 1) % u == 0)
    kern = functools.partial(_attn_kernel, n_sub=n_sub, tk=tk, n_chunks=n_chunks, unroll=unroll)
    width = n_sub * KV_GROUP * ts

    def q_spec(s):
        return pl.BlockSpec((1, ts, GROUP_LANES), lambda i, j, t: (i, row0 // ts + n_sub * t + s, j))

    return pl.pallas_call(
        kern,
        out_shape=jax.ShapeDtypeStruct((b, n_rows, nq), BF16),
        grid=(b, N_KV_HEADS, n_rows // tq),
        in_specs=[pl.BlockSpec(memory_space=pltpu.SMEM)] + [q_spec(s) for s in range(n_sub)] + [
            pl.BlockSpec((1, 1, n_keys, V7X_LANES), lambda i, j, t: (i, j, 0, 0)),
            pl.BlockSpec((1, 1, ATTN_V_ROWS, n_keys), lambda i, j, t: (i, j, 0, 0)),
        ],
        out_specs=pl.BlockSpec((1, tq, GROUP_LANES), lambda i, j, t: (i, t, j)),
        scratch_shapes=[
            pltpu.VMEM((V7X_LANES, width), BF16),
            pltpu.VMEM((1, width), F32),
            pltpu.VMEM((ATTN_V_ROWS, width), F32),
            pltpu.VMEM((tk, width), F32),
            pltpu.VMEM((tk, width), F32),
            pltpu.VMEM((min(ATTN_CHUNKS_PER_PV, n_chunks) * tk, width), BF16),
        ],
        compiler_params=_cparams(3),
        name=name,
    )(bound, *([q] * n_sub), k, vt)


def _score_bound(q_gain, k_gain):
    bound = (HEAD_DIM * (HEAD_DIM ** -0.5 * LOG2_E) * 1.02) * jnp.max(jnp.abs(q_gain)) * jnp.max(jnp.abs(k_gain))
    bound = bound.astype(BF16).astype(F32)
    return jnp.stack([-bound, (2.0 * bound <= EXP2_SAFE_RANGE).astype(F32)])


def _attention(q, k, vt, bound, ctx_len):
    nt = q.shape[1]
    n_lat = nt - ctx_len
    n_sub = max(s for s in (1, 2, 4) if n_lat % (s * ROW_TILE) == 0)
    o_ctx = _attention_call(bound, q, k, vt, row0=0, n_rows=ctx_len, n_keys=ctx_len, n_sub=1,
                            name="flash_attention_ctx")
    o_lat = _attention_call(bound, q, k, vt, row0=ctx_len, n_rows=n_lat, n_keys=nt, n_sub=n_sub,
                            name="flash_attention")
    return o_ctx, o_lat


def _out_ffn_kernel(*refs, final_norm, read_mix, n_mix, read_x, n_x):
    y = read_mix(refs[:n_mix])
    x = read_x(refs[n_mix:n_mix + n_x])
    mod_ref, g_ref, wo_ref, wg_ref, wu_ref, wd_ref, fg_ref, o_ref = refs[n_mix + n_x:]
    mod = mod_ref[0, 0]
    x1 = x + mod[2:3] * _dot(y, wo_ref[...])
    h2 = _norm_mod(x1, g_ref[...], mod[3:4], mod[4:5]).astype(BF16)
    a = _dot(h2, wg_ref[...])
    u = _dot(h2, wu_ref[...])
    hid = (a * jax.nn.sigmoid(a) * u).astype(BF16)
    x2 = x1 + mod[5:6] * _dot(hid, wd_ref[...])
    if final_norm:
        ms = jnp.mean(x2 * x2, axis=-1, keepdims=True)
        x2 = x2 * lax.rsqrt(ms + NORM_EPS) * fg_ref[...]
    o_ref[0] = x2


def _attention_mixer(o_ctx, o_lat):
    d = o_lat.shape[2]
    tm = ROW_TILE
    ctx_tiles = o_ctx.shape[1] // tm

    def read(refs):
        yc_ref, yl_ref = refs
        return jnp.where(pl.program_id(1) < ctx_tiles, yc_ref[0], yl_ref[0])

    specs = [pl.BlockSpec((1, tm, d), lambda i, t: (i, jnp.minimum(t, ctx_tiles - 1), 0)),
             pl.BlockSpec((1, tm, d), lambda i, t: (i, jnp.maximum(t - ctx_tiles, 0), 0))]
    return _Rows(read, specs, [o_ctx, o_lat], (o_lat.shape[0], o_ctx.shape[1] + o_lat.shape[1], d))


def _out_ffn(mix, rows, mods, g2, wo, wg, wu, wd, final_g, *, first_tile, ctx_tiles, final_norm):
    b, n, d = rows.shape
    assert mix.shape == rows.shape
    f = wg.shape[1]
    tm = ROW_TILE
    kern = functools.partial(_out_ffn_kernel, final_norm=final_norm, read_mix=mix.read,
                             n_mix=len(mix.specs), read_x=rows.read, n_x=len(rows.specs))

    def mod_sel(i, t):
        return (i, jnp.where(t + first_tile < ctx_tiles, 0, 1), 0, 0)

    return pl.pallas_call(
        kern,
        out_shape=jax.ShapeDtypeStruct((b, n, d), F32),
        grid=(b, n // tm),
        in_specs=mix.specs + rows.specs + [
            pl.BlockSpec((1, 1, N_MOD_ROWS, d), mod_sel),
            _const_spec((1, d)),
            _const_spec((d, d)),
            _const_spec((d, f)),
            _const_spec((d, f)),
            _const_spec((f, d)),
            _const_spec((1, d)),
        ],
        out_specs=pl.BlockSpec((1, tm, d), lambda i, t: (i, t, 0)),
        compiler_params=_cparams(2),
        name="out_ffn_final" if final_norm else "out_ffn",
    )(*mix.args, *rows.args, mods, g2.reshape(1, d), wo.astype(BF16), wg.astype(BF16), wu.astype(BF16),
      wd.astype(BF16), final_g.reshape(1, d))


def _rwkv_proj_kernel(x_ref, xp_ref, xn_ref, mod_ref, g_ref, mix_ref, wr_ref, wk_ref, wv_ref,
                      w1_ref, w2_ref, a1_ref, a2_ref, g1_ref, g2_ref, w0_ref, a0_ref, kk_ref, ka_ref,
                      seg_ref,
                      r_ref, v_ref, kkn_ref, gate_ref, lw0_ref, lw1_ref, kd0_ref, kd1_ref,
                      b0_ref, b1_ref, *, tm, seq_starts, seq_ends):
    t = pl.program_id(1)
    mod = mod_ref[0, 0]
    g = g_ref[...]
    h = _norm_mod(x_ref[0], g, mod[0:1], mod[1:2])
    h_prev = _norm_mod(xp_ref[0], g, mod[0:1], mod[1:2])[7:8]
    h_next = _norm_mod(xn_ref[0], g, mod[0:1], mod[1:2])[0:1]
    row = lax.broadcasted_iota(jnp.int32, h.shape, 0)
    pos = row + t * tm
    at_start = functools.reduce(jnp.logical_or, [pos == s for s in seq_starts])
    at_end = functools.reduce(jnp.logical_or, [pos == e for e in seq_ends])
    before = jnp.where(row == 0, h_prev, pltpu.roll(h, 1, 0))
    after = jnp.where(row == tm - 1, h_next, pltpu.roll(h, tm - 1, 0))
    before = jnp.where(at_start, 0.0, before)
    after = jnp.where(at_end, 0.0, after)
    xx = 0.5 * (before + after) - h
    mix = mix_ref[...]

    def lerp(j):
        return (h + xx * mix[j:j + 1]).astype(BF16)

    r_ref[0] = _dot(lerp(0), wr_ref[...]).astype(r_ref.dtype)
    k = _dot(lerp(2), wk_ref[...])
    v_ref[0] = _dot(lerp(3), wv_ref[...]).astype(v_ref.dtype)
    kkr = k * kk_ref[...]
    ss = _group_sum(kkr * kkr, seg_ref)
    kkn = kkr * lax.rsqrt(jnp.maximum(ss, 1e-24))
    kkn_ref[0] = kkn.astype(kkn_ref.dtype)
    xw, xa = lerp(1), lerp(4)
    ka = ka_ref[...]
    for d, (lw_ref, kd_ref, b_ref) in enumerate(((lw0_ref, kd0_ref, b0_ref), (lw1_ref, kd1_ref, b1_ref))):
        z = w0_ref[d:d + 1] + _dot(jnp.tanh(_dot(xw, w1_ref[d])).astype(BF16), w2_ref[d])
        lw_ref[0] = -DECAY_SCALE * jax.nn.sigmoid(z)
        a = jax.nn.sigmoid(a0_ref[d:d + 1] + _dot(_dot(xa, a1_ref[d]).astype(BF16), a2_ref[d]))
        kd_ref[0] = (k * (1.0 + (a - 1.0) * ka)).astype(kd_ref.dtype)
        b_ref[0] = (kkn * a).astype(b_ref.dtype)
    gate = _dot(jax.nn.sigmoid(_dot(lerp(5), g1_ref[...])).astype(BF16), g2_ref[...])
    gate_ref[0] = gate.astype(gate_ref.dtype)


def _rwkv_project(xs, mods, g1n, mix, w_rkv, w0, w1, w2, a0, a1, a2, gw1, gw2, k_k, k_a, seg,
                  ctx_len):
    b, nt, d = xs.shape
    tm = ROW_TILE
    halo = 8
    per = tm // halo
    n_halo = nt // halo
    kern = functools.partial(_rwkv_proj_kernel, tm=tm, seq_starts=(0, ctx_len),
                             seq_ends=(ctx_len - 1, nt - 1))
    tok = pl.BlockSpec((1, tm, d), lambda i, t: (i, t, 0))
    bf = lambda w: w.astype(BF16)
    outs = pl.pallas_call(
        kern,
        out_shape=tuple(jax.ShapeDtypeStruct((b, nt, d), dt)
                        for dt in (BF16, BF16, BF16, BF16, F32, F32, BF16, BF16, BF16, BF16)),
        grid=(b, nt // tm),
        in_specs=[
            tok,
            pl.BlockSpec((1, halo, d), lambda i, t: (i, jnp.maximum(t * per - 1, 0), 0)),
            pl.BlockSpec((1, halo, d), lambda i, t: (i, jnp.minimum((t + 1) * per, n_halo - 1), 0)),
            pl.BlockSpec((1, 1, N_MOD_ROWS, d), lambda i, t: (i, jnp.where(t * tm < ctx_len, 0, 1), 0, 0)),
            _const_spec((1, d)),
            _const_spec((N_MOD_ROWS, d)),
            _const_spec((d, d)), _const_spec((d, d)), _const_spec((d, d)),
            _const_spec(w1.shape), _const_spec(w2.shape), _const_spec(a1.shape), _const_spec(a2.shape),
            _const_spec(gw1.shape), _const_spec(gw2.shape),
            _const_spec((2, d)), _const_spec((2, d)), _const_spec((1, d)), _const_spec((1, d)),
            _const_spec((GROUP_LANES, GROUP_LANES)),
        ],
        out_specs=tuple(tok for _ in range(10)),
        compiler_params=_cparams(2),
        name="rwkv_project",
    )(xs, xs, xs, mods, g1n.reshape(1, d),
      jnp.concatenate([mix, jnp.zeros((N_MOD_ROWS - mix.shape[0], d), F32)], axis=0),
      bf(w_rkv[0]), bf(w_rkv[1]), bf(w_rkv[2]), bf(w1), bf(w2), bf(a1), bf(a2), bf(gw1), bf(gw2),
      w0, a0, k_k.reshape(1, d), k_a.reshape(1, d), seg)
    return outs


def _block_diag(x_bf, mask):
    return jnp.where(mask, jnp.concatenate([x_bf] * HEADS_PER_GROUP, axis=0), jnp.zeros((), BF16))


def _cumsum_rows(x, reverse):
    n = x.shape[0]
    row = lax.broadcasted_iota(jnp.int32, x.shape, 0)
    shift = 1
    while shift < n:
        if reverse:
            x = x + jnp.where(row < n - shift, pltpu.roll(x, n - shift, 0), 0.0)
        else:
            x = x + jnp.where(row >= shift, pltpu.roll(x, shift, 0), 0.0)
        shift *= 2
    return x


def _block_transpose(x):
    xt = x.T
    return jnp.concatenate([xt[j * RWKV_HEAD:(j + 1) * RWKV_HEAD] for j in range(HEADS_PER_GROUP)], axis=1)


def _wkv_masks(reverse):
    c = WKV_CHUNK
    rows = lax.broadcasted_iota(jnp.int32, (c, GROUP_LANES), 0)
    cols = lax.broadcasted_iota(jnp.int32, (c, GROUP_LANES), 1) % RWKV_HEAD
    if reverse:
        return cols > rows, cols >= rows
    return cols < rows, cols <= rows


def _bd_mask():
    bd_r = lax.broadcasted_iota(jnp.int32, (GROUP_LANES, GROUP_LANES), 0) // RWKV_HEAD
    bd_c = lax.broadcasted_iota(jnp.int32, (GROUP_LANES, GROUP_LANES), 1) // RWKV_HEAD
    return bd_r == bd_c


def _wkv_prepare(chains):
    c = WKV_CHUNK
    n = len(chains)
    rng = range(n)
    rows = lax.broadcasted_iota(jnp.int32, (c, GROUP_LANES), 0)
    cols = lax.broadcasted_iota(jnp.int32, (c, GROUP_LANES), 1) % RWKV_HEAD
    eye = (cols == rows).astype(F32)
    bmask = _bd_mask()
    masks = {rev: _wkv_masks(rev) for rev in sorted({ch[6] for ch in chains})}
    strict = [masks[ch[6]][0] for ch in chains]
    incl = [masks[ch[6]][1] for ch in chains]
    lw, kd, bb, kk, v, r = ([ch[i] for ch in chains] for i in range(6))
    bd = lambda x: _block_diag(x.astype(BF16), bmask)

    cs = [_cumsum_rows(lw[i], chains[i][6]) for i in rng]
    total = [cs[i][0:1] if chains[i][6] else cs[i][c - 1:c] for i in rng]
    a_s = [-kk[i] * jnp.exp(cs[i] - lw[i]) for i in rng]
    g_inv = [jnp.exp(-cs[i]) for i in rng]
    b_s = [bb[i] * g_inv[i] for i in rng]
    k_s = [kd[i] * g_inv[i] for i in rng]
    r_s = [r[i] * jnp.exp(cs[i]) for i in rng]
    g_rest = [jnp.exp(total[i] - cs[i]) for i in rng]
    b_e = [(bb[i] * g_rest[i]).astype(BF16) for i in rng]
    k_e = [(kd[i] * g_rest[i]).astype(BF16) for i in rng]
    g_end = [jnp.exp(total[i]) for i in rng]

    lhs = [jnp.concatenate([a_s[i], r_s[i]], axis=0).astype(BF16) for i in rng]
    pb = [_dot(lhs[i], bd(_block_transpose(b_s[i]))) for i in rng]
    pk = [_dot(lhs[i], bd(_block_transpose(k_s[i]))) for i in rng]
    l_ab = [jnp.where(strict[i], pb[i][:c], 0.0) for i in rng]
    m_rb = [jnp.where(incl[i], pb[i][c:], 0.0).astype(BF16) for i in rng]
    l_ak = [jnp.where(strict[i], pk[i][:c], 0.0) for i in rng]
    m_rk = [jnp.where(incl[i], pk[i][c:], 0.0) for i in rng]

    blk_r, blk_c = rows, cols

    def coupling(i, w):
        pair = ((blk_r // (2 * w)) == (blk_c // (2 * w))) & ((blk_r // w) != (blk_c // w))
        return jnp.where(pair, l_ab[i], 0.0)

    tmat = [eye + coupling(i, 1) for i in rng]
    w = 2
    while w < c:
        ed = [_dot(coupling(i, w).astype(BF16), bd(tmat[i])) for i in rng]
        tmat = [tmat[i] + _dot(tmat[i].astype(BF16), bd(ed[i])) for i in rng]
        w *= 2
    t_bf = [tmat[i].astype(BF16) for i in rng]

    zv = [_dot(jnp.concatenate([l_ak[i], m_rk[i]], axis=0).astype(BF16), bd(v[i])) for i in rng]
    a_hat = [_dot(t_bf[i], bd(a_s[i])) for i in rng]
    u_til = [_dot(t_bf[i], bd(zv[i][:c])) for i in rng]
    r_hat = [r_s[i] + _dot(m_rb[i], bd(a_hat[i])) for i in rng]
    y_til = [zv[i][c:] + _dot(m_rb[i], bd(u_til[i])) for i in rng]

    au_t = [jnp.concatenate([_block_transpose(a_hat[i]), _block_transpose(u_til[i])], axis=0) for i in rng]
    gh = [_dot(au_t[i].astype(BF16), bd(b_e[i])) for i in rng]
    g_mat = [gh[i][:c] for i in rng]
    h_mat = [gh[i][c:] + _dot(_block_transpose(v[i].astype(F32)).astype(BF16), bd(k_e[i])) for i in rng]
    return [(r_hat[i].astype(BF16), y_til[i], g_mat[i].astype(BF16), h_mat[i], g_end[i]) for i in rng]


def _wkv_kernel(lwf, kdf, bf_, kkf, vf, rf, lwr, kdr, br, kkr, vr, rr, yf_ref, yr_ref, s_ref):
    @pl.when(pl.program_id(1) == 0)
    def _():
        s_ref[...] = jnp.zeros(s_ref.shape, F32)

    c = WKV_CHUNK
    n_sub = lwf.shape[1] // c
    n_groups = lwf.shape[2] // GROUP_LANES
    dirs = (((lwf, kdf, bf_, kkf, vf, rf), yf_ref), ((lwr, kdr, br, kkr, vr, rr), yr_ref))
    chains, keys = [], []
    for u in range(n_sub):
        rs = slice(u * c, (u + 1) * c)
        for j in range(n_groups):
            sl = slice(j * GROUP_LANES, (j + 1) * GROUP_LANES)
            for d, (refs, _) in enumerate(dirs):
                lw, kd, bb, kk, v, r = (ref[0, rs, sl] for ref in refs)
                chains.append((lw, kd.astype(F32), bb.astype(F32), kk.astype(F32), v, r.astype(F32),
                               d == 1))
                keys.append((u, j, d))
    prepared = dict(zip(keys, _wkv_prepare(chains)))

    bmask = _bd_mask()
    for j in range(n_groups):
        sl = slice(j * GROUP_LANES, (j + 1) * GROUP_LANES)
        for d, (_, y_ref) in enumerate(dirs):
            s = s_ref[d, :, sl]
            for u in (range(n_sub) if d == 0 else reversed(range(n_sub))):
                r_hat, y_til, g_mat, h_mat, g_end = prepared[(u, j, d)]
                s_bf = s.astype(BF16)
                y = y_til + _dot_nt(r_hat, _block_diag(s_bf, bmask))
                y_ref[0, u * c:(u + 1) * c, sl] = y.astype(y_ref.dtype)
                s = s * g_end + _dot(s_bf, _block_diag(g_mat, bmask)) + h_mat
            s_ref[d, :, sl] = s


def _wkv_scan(lw0, lw1, kd0, kd1, b0, b1, kk, v, r, ctx_len):
    b, nt, d = v.shape
    c = WKV_CHUNKS_PER_STEP * WKV_CHUNK
    assert ctx_len % c == 0 and nt % c == 0
    n_steps = nt // c
    ctx_chunks = ctx_len // c

    def fwd(i, s):
        return (i, s, 0)

    def rev(i, s):
        return (i, jnp.where(s < ctx_chunks, ctx_chunks - 1 - s, n_steps - 1 + ctx_chunks - s), 0)

    blk_f = pl.BlockSpec((1, c, d), fwd)
    blk_r = pl.BlockSpec((1, c, d), rev)
    return pl.pallas_call(
        _wkv_kernel,
        out_shape=(jax.ShapeDtypeStruct((b, nt, d), BF16), jax.ShapeDtypeStruct((b, nt, d), BF16)),
        grid=(b, n_steps),
        in_specs=[blk_f] * 6 + [blk_r] * 6,
        out_specs=(blk_f, blk_r),
        scratch_shapes=[pltpu.VMEM((2, RWKV_HEAD, d), F32)],
        compiler_params=_cparams(2),
        name="wkv_scan",
    )(lw0, kd0, b0, kk, v, r, lw1, kd1, b1, kk, v, r)


def _read_rwkv_mix(refs):
    yf_ref, yr_ref, r_ref, v_ref, kd0_ref, kd1_ref, gate_ref, rk_ref, lg_ref, lb_ref, seg_ref = refs
    wkv = yf_ref[0].astype(F32) + yr_ref[0].astype(F32)
    inv_n = 1.0 / RWKV_HEAD
    mu = _group_sum(wkv, seg_ref, two_pass=True) * inv_n
    dev = wkv - mu
    var = _group_sum(dev * dev, seg_ref) * inv_n
    gn = dev * lax.rsqrt(var + GN_EPS) * lg_ref[...] + lb_ref[...]
    rk = r_ref[0].astype(F32) * rk_ref[...]
    kd_sum = kd0_ref[0].astype(F32) + kd1_ref[0].astype(F32)
    bonus = _group_sum(rk * kd_sum, seg_ref) * v_ref[0].astype(F32)
    return ((gn + bonus) * gate_ref[0].astype(F32)).astype(BF16)


def _rwkv_mixer(yf, yr, r, v, kd0, kd1, gate, r_k, ln_g, ln_b, seg, ctx_len):
    b, nt, d = v.shape
    tm = ROW_TILE
    off = ctx_len // tm
    tok = pl.BlockSpec((1, tm, d), lambda i, t: (i, t + off, 0))
    specs = [tok] * 7 + [_const_spec((1, d))] * 3 + [_const_spec((GROUP_LANES, GROUP_LANES))]
    args = [yf, yr, r, v, kd0, kd1, gate, r_k.reshape(1, d), ln_g.reshape(1, d), ln_b.reshape(1, d), seg]
    return _Rows(_read_rwkv_mix, specs, args, (b, nt - ctx_len, d))


def _rope_tables(ctx_len, seq_len):
    f32 = np.float32
    rows = seq_len // GRID_W
    row = np.repeat(np.arange(rows, dtype=f32), GRID_W)
    col = np.tile(np.arange(GRID_W, dtype=f32), rows)
    inv_freq = np.power(f32(ROPE_THETA), -np.arange(ROPE_PAIRS, dtype=f32) / f32(ROPE_PAIRS)).astype(f32)
    row_ang, col_ang = row[:, None] * inv_freq, col[:, None] * inv_freq
    ang = np.concatenate([row_ang, row_ang, col_ang, col_ang], axis=1).astype(np.float64)
    cos = np.concatenate([np.ones((ctx_len, HEAD_DIM)), np.cos(ang)], axis=0)
    sin = np.concatenate([np.zeros((ctx_len, HEAD_DIM)), np.sin(ang)], axis=0)
    sign = np.concatenate([-np.ones(ROPE_PAIRS), np.ones(ROPE_PAIRS)] * 2)
    reps = V7X_LANES // HEAD_DIM
    return (jnp.asarray(np.tile(cos, (1, reps)), F32), jnp.asarray(np.tile(sin * sign, (1, reps)), F32))


def kernel(x, c, ctx, c_ctx, mod_w, mod_b, norm1_g, norm2_g, ffn_wg, ffn_wu, ffn_wd, attn_wqkv,
           attn_q_gain, attn_k_gain, attn_wo, rwkv_mix, rwkv_wrkv, rwkv_w0, rwkv_w1, rwkv_w2, rwkv_a0,
           rwkv_a1, rwkv_a2, rwkv_g1, rwkv_g2, rwkv_k_k, rwkv_k_a, rwkv_r_k, rwkv_ln_g, rwkv_ln_b,
           rwkv_wo, final_g):
    b, seq_len, d = x.shape
    ctx_len = ctx.shape[1]
    depth = mod_w.shape[0]
    assert depth == 2 and d == N_HEADS * HEAD_DIM
    assert ctx_len % ROW_TILE == 0 and seq_len % ROW_TILE == 0 and seq_len % GRID_W == 0
    ctx_tiles = ctx_len // ROW_TILE

    n_rows = -(-(b + 1) // 8) * 8
    c_rows = jnp.concatenate([c, c_ctx[None], jnp.zeros((n_rows - b - 1, d), F32)], axis=0)
    m_all = _modulation(c_rows, mod_w, mod_b).reshape(depth, n_rows, 6, d)
    pad = jnp.zeros((depth, b, N_MOD_ROWS - 6, d), F32)
    lat = jnp.concatenate([m_all[:, :b], pad], axis=2)
    con = jnp.concatenate([jnp.broadcast_to(m_all[:, b:b + 1], (depth, b, 6, d)), pad], axis=2)
    mods = jnp.stack([con, lat], axis=2)

    cos2, sin2 = _rope_tables(ctx_len, seq_len)
    seg = jnp.asarray(np.kron(np.eye(HEADS_PER_GROUP), np.ones((HEAD_DIM, HEAD_DIM))), BF16)

    stream = _stream_of(ctx, x)

    q, k, vt = _qkv_project(stream, mods[0], norm1_g[0], attn_wqkv[0], attn_q_gain[0], attn_k_gain[0],
                            cos2, sin2, seg, ctx_tiles)
    o_ctx, o_lat = _attention(q, k, vt, _score_bound(attn_q_gain[0], attn_k_gain[0]), ctx_len)
    xs = _out_ffn(_attention_mixer(o_ctx, o_lat), stream, mods[0], norm2_g[0], attn_wo[0], ffn_wg[0],
                  ffn_wu[0], ffn_wd[0], final_g, first_tile=0, ctx_tiles=ctx_tiles, final_norm=False)

    r, v, kk, gate, lw0, lw1, kd0, kd1, b0, b1 = _rwkv_project(
        xs, mods[1], norm1_g[1], rwkv_mix[0], rwkv_wrkv[0], rwkv_w0[0], rwkv_w1[0], rwkv_w2[0],
        rwkv_a0[0], rwkv_a1[0], rwkv_a2[0], rwkv_g1[0], rwkv_g2[0], rwkv_k_k[0], rwkv_k_a[0], seg,
        ctx_len)
    yf, yr = _wkv_scan(lw0, lw1, kd0, kd1, b0, b1, kk, v, r, ctx_len)
    mix = _rwkv_mixer(yf, yr, r, v, kd0, kd1, gate, rwkv_r_k[0].reshape(-1), rwkv_ln_g[0],
                      rwkv_ln_b[0], seg, ctx_len)
    return _out_ffn(mix, _rows_from(xs, ctx_tiles), mods[1], norm2_g[1], rwkv_wo[0], ffn_wg[1], ffn_wu[1],
                    ffn_wd[1], final_g, first_tile=ctx_tiles, ctx_tiles=ctx_tiles, final_norm=True)
```

```python
import functools
from typing import Callable, NamedTuple

import jax
import jax.numpy as jnp
import numpy as np
from jax import lax
from jax.experimental import pallas as pl
from jax.experimental.pallas import tpu as pltpu

F32 = jnp.float32
BF16 = jnp.bfloat16

NORM_EPS = 1e-6
GN_EPS = 64e-5
GRID_W = 64
N_HEADS = 16
N_KV_HEADS = 4
KV_GROUP = N_HEADS // N_KV_HEADS
HEAD_DIM = 64
ROPE_THETA = 10000.0
ROPE_PAIRS = HEAD_DIM // 4
RWKV_HEAD = 64
DECAY_SCALE = float(np.exp(-0.5))
LOG2_E = float(np.log2(np.e))
EXP2_SAFE_RANGE = 120.0

V7X_LANES = 128
V7X_MXU_DIM = 256
V7X_VMEM_LIMIT_BYTES = 60000 * 1024

ROW_TILE = 256
GROUP_LANES = V7X_MXU_DIM
HEADS_PER_GROUP = GROUP_LANES // HEAD_DIM
BF16_SUBLANES = 16
ATTN_V_ROWS = HEAD_DIM + BF16_SUBLANES
ATTN_CHUNKS_PER_PV = 11
WKV_CHUNK = 64
WKV_CHUNKS_PER_STEP = 2
N_MOD_ROWS = 8


def _cparams(n_axes):
    return pltpu.CompilerParams(
        dimension_semantics=("arbitrary",) * n_axes,
        vmem_limit_bytes=V7X_VMEM_LIMIT_BYTES,
    )


def _const_spec(shape):
    nd = len(shape)
    return pl.BlockSpec(shape, lambda *_: (0,) * nd, pipeline_mode=pl.Buffered(1))


def _dot(a, b):
    return jnp.dot(a, b, preferred_element_type=F32)


def _dot_nt(a, b):
    return lax.dot_general(a, b, (((1,), (1,)), ((), ())), preferred_element_type=F32)


def _norm_mod(x, g, shift, scale):
    ms = jnp.mean(x * x, axis=-1, keepdims=True)
    return (x * lax.rsqrt(ms + NORM_EPS) * g) * (1.0 + scale) + shift


def _group_sum(x, seg_ref, two_pass=False):
    seg = seg_ref[...]
    outs = []
    for j in range(x.shape[1] // GROUP_LANES):
        xs = x[:, j * GROUP_LANES:(j + 1) * GROUP_LANES]
        hi = xs.astype(BF16)
        out = _dot(hi, seg)
        if two_pass:
            out = out + _dot((xs - hi.astype(F32)).astype(BF16), seg)
        outs.append(out)
    return outs[0] if len(outs) == 1 else jnp.concatenate(outs, axis=1)


def _mod_kernel(c_ref, w_ref, b_ref, o_ref):
    c = c_ref[...]
    s = c * jax.nn.sigmoid(c)
    o_ref[0] = _dot(s.astype(BF16), w_ref[0]) + b_ref[0]


def _modulation(c_rows, mod_w, mod_b):
    depth, d, n = mod_w.shape
    rows = c_rows.shape[0]
    tn = n // 4
    return pl.pallas_call(
        _mod_kernel,
        out_shape=jax.ShapeDtypeStruct((depth, rows, n), F32),
        grid=(depth, n // tn),
        in_specs=[
            pl.BlockSpec((rows, d), lambda i, j: (0, 0)),
            pl.BlockSpec((1, d, tn), lambda i, j: (i, 0, j)),
            pl.BlockSpec((1, 1, tn), lambda i, j: (i, 0, j)),
        ],
        out_specs=pl.BlockSpec((1, rows, tn), lambda i, j: (i, 0, j)),
        compiler_params=_cparams(2),
        name="modulation",
    )(c_rows, mod_w.astype(BF16), mod_b.reshape(depth, 1, n))


def _rope(x, cos, sin_signed):
    w = x.shape[1]
    lane = lax.broadcasted_iota(jnp.int32, x.shape, 1)
    first_half = (lane % (2 * ROPE_PAIRS)) < ROPE_PAIRS
    partner = jnp.where(first_half, pltpu.roll(x, w - ROPE_PAIRS, 1), pltpu.roll(x, ROPE_PAIRS, 1))
    return x * cos + partner * sin_signed


class _Rows(NamedTuple):
    read: Callable
    specs: list
    args: list
    shape: tuple


def _stream_of(ctx, x):
    b, c, d = ctx.shape
    tm = ROW_TILE
    ctx_tiles = c // tm

    def read(refs):
        c_ref, x_ref = refs
        return jnp.where(pl.program_id(1) < ctx_tiles, c_ref[0], x_ref[0])

    specs = [pl.BlockSpec((1, tm, d), lambda i, t: (i, jnp.minimum(t, ctx_tiles - 1), 0)),
             pl.BlockSpec((1, tm, d), lambda i, t: (i, jnp.maximum(t - ctx_tiles, 0), 0))]
    return _Rows(read, specs, [ctx, x], (b, c + x.shape[1], d))


def _rows_from(xs, tile_offset):
    b, n, d = xs.shape
    spec = pl.BlockSpec((1, ROW_TILE, d), lambda i, t: (i, t + tile_offset, 0))
    return _Rows(lambda refs: refs[0][0], [spec], [xs], (b, n - tile_offset * ROW_TILE, d))


def _qkv_kernel(*refs, nq, nk, read_x, n_x):
    x = read_x(refs[:n_x])
    (mod_ref, g_ref, w_ref, qg_ref, kg_ref, cos_ref, sin_ref, seg_ref, ek_ref, ev_ref,
     q_ref, k_ref, v_ref) = refs[n_x:]
    mod = mod_ref[0, 0]
    h = _norm_mod(x, g_ref[...], mod[0:1], mod[1:2]).astype(BF16)
    qkv = _dot(h, w_ref[...])
    q, k, v = qkv[:, :nq], qkv[:, nq:nq + nk], qkv[:, nq + nk:]
    cos2, sin2 = cos_ref[...], sin_ref[...]

    def head_norm_rope(z, gain):
        reps = z.shape[1] // V7X_LANES
        cos = jnp.concatenate([cos2] * reps, axis=1)
        sin = jnp.concatenate([sin2] * reps, axis=1)
        ss = _group_sum(z * z, seg_ref)
        zn = z * lax.rsqrt(ss * (1.0 / HEAD_DIM) + NORM_EPS) * gain
        return _rope(zn, cos, sin)

    qn = head_norm_rope(q, qg_ref[...]) * (HEAD_DIM ** -0.5 * LOG2_E)
    q_ref[0] = qn.astype(BF16)
    kn = head_norm_rope(k, kg_ref[...]).astype(BF16)
    k_t = _dot(kn, ek_ref[...]).astype(BF16)
    v_bf = v.astype(BF16)
    ones = jnp.ones((ATTN_V_ROWS - HEAD_DIM, v_bf.shape[0]), BF16)
    shift_lane = lax.broadcasted_iota(jnp.int32, (v_bf.shape[0], V7X_LANES), 1) == HEAD_DIM
    for j in range(N_KV_HEADS):
        k_ref[0, j] = jnp.where(shift_lane, jnp.ones((), BF16), k_t[:, j * V7X_LANES:(j + 1) * V7X_LANES])
        v_ref[0, j, 0:HEAD_DIM, :] = _dot_nt(ev_ref[j], v_bf).astype(BF16)
        v_ref[0, j, HEAD_DIM:, :] = ones


def _head_select(n_heads, rows):
    sel = np.zeros((n_heads, rows, n_heads * HEAD_DIM), np.float32)
    for j in range(n_heads):
        sel[j, np.arange(HEAD_DIM), j * HEAD_DIM + np.arange(HEAD_DIM)] = 1.0
    return sel


def _qkv_project(rows, mods, g1, wqkv, q_gain, k_gain, cos2, sin2, seg, ctx_tiles):
    b, nt, d = rows.shape
    nq, nk = N_HEADS * HEAD_DIM, N_KV_HEADS * HEAD_DIM
    tm = ROW_TILE
    ek = np.zeros((nk, N_KV_HEADS * V7X_LANES), np.float32)
    for j in range(N_KV_HEADS):
        ek[j * HEAD_DIM + np.arange(HEAD_DIM), j * V7X_LANES + np.arange(HEAD_DIM)] = 1.0
    ev = _head_select(N_KV_HEADS, HEAD_DIM)
    kern = functools.partial(_qkv_kernel, nq=nq, nk=nk, read_x=rows.read, n_x=len(rows.specs))
    return pl.pallas_call(
        kern,
        out_shape=(
            jax.ShapeDtypeStruct((b, nt, nq), BF16),
            jax.ShapeDtypeStruct((b, N_KV_HEADS, nt, V7X_LANES), BF16),
            jax.ShapeDtypeStruct((b, N_KV_HEADS, ATTN_V_ROWS, nt), BF16),
        ),
        grid=(b, nt // tm),
        in_specs=rows.specs + [
            pl.BlockSpec((1, 1, N_MOD_ROWS, d), lambda i, t: (i, jnp.where(t < ctx_tiles, 0, 1), 0, 0)),
            _const_spec((1, d)),
            _const_spec((d, nq + 2 * nk)),
            _const_spec((1, nq)),
            _const_spec((1, nk)),
            pl.BlockSpec((tm, V7X_LANES), lambda i, t: (t, 0)),
            pl.BlockSpec((tm, V7X_LANES), lambda i, t: (t, 0)),
            _const_spec((GROUP_LANES, GROUP_LANES)),
            _const_spec(ek.shape),
            _const_spec(ev.shape),
        ],
        out_specs=(
            pl.BlockSpec((1, tm, nq), lambda i, t: (i, t, 0)),
            pl.BlockSpec((1, N_KV_HEADS, tm, V7X_LANES), lambda i, t: (i, 0, t, 0)),
            pl.BlockSpec((1, N_KV_HEADS, ATTN_V_ROWS, tm), lambda i, t: (i, 0, 0, t)),
        ),
        compiler_params=_cparams(2),
        name="qkv_project",
    )(*rows.args, mods, g1.reshape(1, d), wqkv.astype(BF16),
      jnp.tile(q_gain, N_HEADS).reshape(1, nq), jnp.tile(k_gain, N_KV_HEADS).reshape(1, nk),
      cos2, sin2, seg, jnp.asarray(ek, BF16), jnp.asarray(ev, BF16))


def _attn_kernel(*refs, n_sub, tk, n_chunks, unroll):
    bound_ref = refs[0]
    q_refs = refs[1:1 + n_sub]
    k_ref, vt_ref, o_ref, qt_ref, m_ref, acc_ref, sa_ref, sb_ref, p_ref = refs[1 + n_sub:]
    ts = q_refs[0].shape[1]
    width = qt_ref.shape[1]
    for s, q_ref in enumerate(q_refs):
        q_t = q_ref[0].astype(F32).T
        for g in range(KV_GROUP):
            col = (s * KV_GROUP + g) * ts
            qt_ref[0:HEAD_DIM, col:col + ts] = q_t[g * HEAD_DIM:(g + 1) * HEAD_DIM].astype(BF16)
    qt_ref[HEAD_DIM:, :] = jnp.full((qt_ref.shape[0] - HEAD_DIM, width), bound_ref[0], F32).astype(BF16)
    acc_ref[...] = jnp.zeros(acc_ref.shape, F32)

    def chunk(c, n=1):
        return pl.ds(c * tk if isinstance(c, int) else pl.multiple_of(c * tk, n * tk), n * tk)

    def scores(c):
        return _dot(k_ref[0, 0, chunk(c), :], qt_ref[...])

    @pl.when(bound_ref[1] > 0.5)
    def _():
        def accumulate(c0, n):
            for u in range(n):
                p_ref[u * tk:(u + 1) * tk, :] = jnp.exp2(scores(c0 + u)).astype(BF16)
            acc_ref[...] += _dot(vt_ref[0, 0, :, chunk(c0, n)], p_ref[0:n * tk, :])

        per_trip = p_ref.shape[0] // tk

        def trip(j, carry):
            accumulate(per_trip * j, per_trip)
            return carry

        n_trips, rem = n_chunks // per_trip, n_chunks % per_trip
        if n_trips:
            lax.fori_loop(0, n_trips, trip, 0)
        if rem:
            accumulate(n_trips * per_trip, rem)

    @pl.when(bound_ref[1] <= 0.5)
    def _():
        _attn_online_softmax(scores, chunk, vt_ref, m_ref, acc_ref, sa_ref, sb_ref, p_ref,
                             n_chunks=n_chunks, unroll=unroll)

    acc = acc_ref[...]
    ot = acc[0:HEAD_DIM] / acc[HEAD_DIM:HEAD_DIM + 1]
    for s in range(n_sub):
        heads = [ot[:, (s * KV_GROUP + g) * ts:(s * KV_GROUP + g + 1) * ts] for g in range(KV_GROUP)]
        o_ref[0, s * ts:(s + 1) * ts, :] = jnp.concatenate(heads, axis=0).T.astype(o_ref.dtype)


def _attn_online_softmax(scores, chunk, vt_ref, m_ref, acc_ref, sa_ref, sb_ref, p_ref, *, n_chunks, unroll):
    m_ref[...] = jnp.full(m_ref.shape, -jnp.inf, F32)
    tk = sa_ref.shape[0]

    def update(c, s_ref):
        vtc = vt_ref[0, 0, :, chunk(c)]
        alphas = []
        for cb in range(s_ref.shape[1] // V7X_LANES):
            cols = slice(cb * V7X_LANES, (cb + 1) * V7X_LANES)
            st = s_ref[:, cols]
            m_prev = m_ref[:, cols]
            m_new = jnp.maximum(m_prev, jnp.max(st, axis=0, keepdims=True))
            alphas.append(jnp.exp2(m_prev - m_new))
            p_ref[0:tk, cols] = jnp.exp2(st - m_new).astype(BF16)
            m_ref[:, cols] = m_new
        alpha = jnp.concatenate(alphas, axis=1)
        acc_ref[...] = alpha * acc_ref[...] + _dot(vtc, p_ref[0:tk, :])

    slots = (sa_ref, sb_ref)
    sa_ref[...] = scores(0)

    def body(j, carry):
        c0 = unroll * j
        for u in range(unroll):
            slots[(u + 1) % 2][...] = scores(c0 + u + 1)
            update(c0 + u, slots[u % 2])
        return carry

    if n_chunks > 1:
        lax.fori_loop(0, (n_chunks - 1) // unroll, body, 0)
    update(n_chunks - 1, slots[0])


def _attention_call(bound, q, k, vt, *, row0, n_rows, n_keys, n_sub, name):
    b, _, nq = q.shape
    ts = ROW_TILE
    tk = ROW_TILE
    tq = n_sub * ts
    n_chunks = n_keys // tk
    assert n_rows % tq == 0 and row0 % ts == 0 and n_keys % tk == 0 and n_chunks % 2 == 1
    unroll = max(u for u in (2, 4) if (n_chunks - 1) % u == 0)
    kern = functools.partial(_attn_kernel, n_sub=n_sub, tk=tk, n_chunks=n_chunks, unroll=unroll)
    width = n_sub * KV_GROUP * ts

    def q_spec(s):
        return pl.BlockSpec((1, ts, GROUP_LANES), lambda i, j, t: (i, row0 // ts + n_sub * t + s, j))

    return pl.pallas_call(
        kern,
        out_shape=jax.ShapeDtypeStruct((b, n_rows, nq), BF16),
        grid=(b, N_KV_HEADS, n_rows // tq),
        in_specs=[pl.BlockSpec(memory_space=pltpu.SMEM)] + [q_spec(s) for s in range(n_sub)] + [
            pl.BlockSpec((1, 1, n_keys, V7X_LANES), lambda i, j, t: (i, j, 0, 0)),
            pl.BlockSpec((1, 1, ATTN_V_ROWS, n_keys), lambda i, j, t: (i, j, 0, 0)),
        ],
        out_specs=pl.BlockSpec((1, tq, GROUP_LANES), lambda i, j, t: (i, t, j)),
        scratch_shapes=[
            pltpu.VMEM((V7X_LANES, width), BF16),
            pltpu.VMEM((1, width), F32),
            pltpu.VMEM((ATTN_V_ROWS, width), F32),
            pltpu.VMEM((tk, width), F32),
            pltpu.VMEM((tk, width), F32),
            pltpu.VMEM((min(ATTN_CHUNKS_PER_PV, n_chunks) * tk, width), BF16),
        ],
        compiler_params=_cparams(3),
        name=name,
    )(bound, *([q] * n_sub), k, vt)


def _score_bound(q_gain, k_gain):
    bound = (HEAD_DIM * (HEAD_DIM ** -0.5 * LOG2_E) * 1.02) * jnp.max(jnp.abs(q_gain)) * jnp.max(jnp.abs(k_gain))
    bound = bound.astype(BF16).astype(F32)
    return jnp.stack([-bound, (2.0 * bound <= EXP2_SAFE_RANGE).astype(F32)])


def _attention(q, k, vt, bound, ctx_len):
    nt = q.shape[1]
    n_lat = nt - ctx_len
    n_sub = max(s for s in (1, 2, 4) if n_lat % (s * ROW_TILE) == 0)
    o_ctx = _attention_call(bound, q, k, vt, row0=0, n_rows=ctx_len, n_keys=ctx_len, n_sub=1,
                            name="flash_attention_ctx")
    o_lat = _attention_call(bound, q, k, vt, row0=ctx_len, n_rows=n_lat, n_keys=nt, n_sub=n_sub,
                            name="flash_attention")
    return o_ctx, o_lat


def _out_ffn_kernel(*refs, final_norm, read_mix, n_mix, read_x, n_x):
    y = read_mix(refs[:n_mix])
    x = read_x(refs[n_mix:n_mix + n_x])
    mod_ref, g_ref, wo_ref, wg_ref, wu_ref, wd_ref, fg_ref, o_ref = refs[n_mix + n_x:]
    mod = mod_ref[0, 0]
    x1 = x + mod[2:3] * _dot(y, wo_ref[...])
    h2 = _norm_mod(x1, g_ref[...], mod[3:4], mod[4:5]).astype(BF16)
    a = _dot(h2, wg_ref[...])
    u = _dot(h2, wu_ref[...])
    hid = (a * jax.nn.sigmoid(a) * u).astype(BF16)
    x2 = x1 + mod[5:6] * _dot(hid, wd_ref[...])
    if final_norm:
        ms = jnp.mean(x2 * x2, axis=-1, keepdims=True)
        x2 = x2 * lax.rsqrt(ms + NORM_EPS) * fg_ref[...]
    o_ref[0] = x2


def _attention_mixer(o_ctx, o_lat):
    d = o_lat.shape[2]
    tm = ROW_TILE
    ctx_tiles = o_ctx.shape[1] // tm

    def read(refs):
        yc_ref, yl_ref = refs
        return jnp.where(pl.program_id(1) < ctx_tiles, yc_ref[0], yl_ref[0])

    specs = [pl.BlockSpec((1, tm, d), lambda i, t: (i, jnp.minimum(t, ctx_tiles - 1), 0)),
             pl.BlockSpec((1, tm, d), lambda i, t: (i, jnp.maximum(t - ctx_tiles, 0), 0))]
    return _Rows(read, specs, [o_ctx, o_lat], (o_lat.shape[0], o_ctx.shape[1] + o_lat.shape[1], d))


def _out_ffn(mix, rows, mods, g2, wo, wg, wu, wd, final_g, *, first_tile, ctx_tiles, final_norm):
    b, n, d = rows.shape
    assert mix.shape == rows.shape
    f = wg.shape[1]
    tm = ROW_TILE
    kern = functools.partial(_out_ffn_kernel, final_norm=final_norm, read_mix=mix.read,
                             n_mix=len(mix.specs), read_x=rows.read, n_x=len(rows.specs))

    def mod_sel(i, t):
        return (i, jnp.where(t + first_tile < ctx_tiles, 0, 1), 0, 0)

    return pl.pallas_call(
        kern,
        out_shape=jax.ShapeDtypeStruct((b, n, d), F32),
        grid=(b, n // tm),
        in_specs=mix.specs + rows.specs + [
            pl.BlockSpec((1, 1, N_MOD_ROWS, d), mod_sel),
            _const_spec((1, d)),
            _const_spec((d, d)),
            _const_spec((d, f)),
            _const_spec((d, f)),
            _const_spec((f, d)),
            _const_spec((1, d)),
        ],
        out_specs=pl.BlockSpec((1, tm, d), lambda i, t: (i, t, 0)),
        compiler_params=_cparams(2),
        name="out_ffn_final" if final_norm else "out_ffn",
    )(*mix.args, *rows.args, mods, g2.reshape(1, d), wo.astype(BF16), wg.astype(BF16), wu.astype(BF16),
      wd.astype(BF16), final_g.reshape(1, d))


def _rwkv_proj_kernel(x_ref, xp_ref, xn_ref, mod_ref, g_ref, mix_ref, wr_ref, wk_ref, wv_ref,
                      w1_ref, w2_ref, a1_ref, a2_ref, g1_ref, g2_ref, w0_ref, a0_ref, kk_ref, ka_ref,
                      seg_ref,
                      r_ref, v_ref, kkn_ref, gate_ref, lw0_ref, lw1_ref, kd0_ref, kd1_ref,
                      b0_ref, b1_ref, *, tm, seq_starts, seq_ends):
    t = pl.program_id(1)
    mod = mod_ref[0, 0]
    g = g_ref[...]
    h = _norm_mod(x_ref[0], g, mod[0:1], mod[1:2])
    h_prev = _norm_mod(xp_ref[0], g, mod[0:1], mod[1:2])[7:8]
    h_next = _norm_mod(xn_ref[0], g, mod[0:1], mod[1:2])[0:1]
    row = lax.broadcasted_iota(jnp.int32, h.shape, 0)
    pos = row + t * tm
    at_start = functools.reduce(jnp.logical_or, [pos == s for s in seq_starts])
    at_end = functools.reduce(jnp.logical_or, [pos == e for e in seq_ends])
    before = jnp.where(row == 0, h_prev, pltpu.roll(h, 1, 0))
    after = jnp.where(row == tm - 1, h_next, pltpu.roll(h, tm - 1, 0))
    before = jnp.where(at_start, 0.0, before)
    after = jnp.where(at_end, 0.0, after)
    xx = 0.5 * (before + after) - h
    mix = mix_ref[...]

    def lerp(j):
        return (h + xx * mix[j:j + 1]).astype(BF16)

    r_ref[0] = _dot(lerp(0), wr_ref[...]).astype(r_ref.dtype)
    k = _dot(lerp(2), wk_ref[...])
    v_ref[0] = _dot(lerp(3), wv_ref[...]).astype(v_ref.dtype)
    kkr = k * kk_ref[...]
    ss = _group_sum(kkr * kkr, seg_ref)
    kkn = kkr * lax.rsqrt(jnp.maximum(ss, 1e-24))
    kkn_ref[0] = kkn.astype(kkn_ref.dtype)
    xw, xa = lerp(1), lerp(4)
    ka = ka_ref[...]
    for d, (lw_ref, kd_ref, b_ref) in enumerate(((lw0_ref, kd0_ref, b0_ref), (lw1_ref, kd1_ref, b1_ref))):
        z = w0_ref[d:d + 1] + _dot(jnp.tanh(_dot(xw, w1_ref[d])).astype(BF16), w2_ref[d])
        lw_ref[0] = -DECAY_SCALE * jax.nn.sigmoid(z)
        a = jax.nn.sigmoid(a0_ref[d:d + 1] + _dot(_dot(xa, a1_ref[d]).astype(BF16), a2_ref[d]))
        kd_ref[0] = (k * (1.0 + (a - 1.0) * ka)).astype(kd_ref.dtype)
        b_ref[0] = (kkn * a).astype(b_ref.dtype)
    gate = _dot(jax.nn.sigmoid(_dot(lerp(5), g1_ref[...])).astype(BF16), g2_ref[...])
    gate_ref[0] = gate.astype(gate_ref.dtype)


def _rwkv_project(xs, mods, g1n, mix, w_rkv, w0, w1, w2, a0, a1, a2, gw1, gw2, k_k, k_a, seg,
                  ctx_len):
    b, nt, d = xs.shape
    tm = ROW_TILE
    halo = 8
    per = tm // halo
    n_halo = nt // halo
    kern = functools.partial(_rwkv_proj_kernel, tm=tm, seq_starts=(0, ctx_len),
                             seq_ends=(ctx_len - 1, nt - 1))
    tok = pl.BlockSpec((1, tm, d), lambda i, t: (i, t, 0))
    bf = lambda w: w.astype(BF16)
    outs = pl.pallas_call(
        kern,
        out_shape=tuple(jax.ShapeDtypeStruct((b, nt, d), dt)
                        for dt in (BF16, BF16, BF16, BF16, F32, F32, BF16, BF16, BF16, BF16)),
        grid=(b, nt // tm),
        in_specs=[
            tok,
            pl.BlockSpec((1, halo, d), lambda i, t: (i, jnp.maximum(t * per - 1, 0), 0)),
            pl.BlockSpec((1, halo, d), lambda i, t: (i, jnp.minimum((t + 1) * per, n_halo - 1), 0)),
            pl.BlockSpec((1, 1, N_MOD_ROWS, d), lambda i, t: (i, jnp.where(t * tm < ctx_len, 0, 1), 0, 0)),
            _const_spec((1, d)),
            _const_spec((N_MOD_ROWS, d)),
            _const_spec((d, d)), _const_spec((d, d)), _const_spec((d, d)),
            _const_spec(w1.shape), _const_spec(w2.shape), _const_spec(a1.shape), _const_spec(a2.shape),
            _const_spec(gw1.shape), _const_spec(gw2.shape),
            _const_spec((2, d)), _const_spec((2, d)), _const_spec((1, d)), _const_spec((1, d)),
            _const_spec((GROUP_LANES, GROUP_LANES)),
        ],
        out_specs=tuple(tok for _ in range(10)),
        compiler_params=_cparams(2),
        name="rwkv_project",
    )(xs, xs, xs, mods, g1n.reshape(1, d),
      jnp.concatenate([mix, jnp.zeros((N_MOD_ROWS - mix.shape[0], d), F32)], axis=0),
      bf(w_rkv[0]), bf(w_rkv[1]), bf(w_rkv[2]), bf(w1), bf(w2), bf(a1), bf(a2), bf(gw1), bf(gw2),
      w0, a0, k_k.reshape(1, d), k_a.reshape(1, d), seg)
    return outs


def _block_diag(x_bf, mask):
    return jnp.where(mask, jnp.concatenate([x_bf] * HEADS_PER_GROUP, axis=0), jnp.zeros((), BF16))


def _cumsum_rows(x, reverse):
    n = x.shape[0]
    row = lax.broadcasted_iota(jnp.int32, x.shape, 0)
    shift = 1
    while shift < n:
        if reverse:
            x = x + jnp.where(row < n - shift, pltpu.roll(x, n - shift, 0), 0.0)
        else:
            x = x + jnp.where(row >= shift, pltpu.roll(x, shift, 0), 0.0)
        shift *= 2
    return x


def _block_transpose(x):
    xt = x.T
    return jnp.concatenate([xt[j * RWKV_HEAD:(j + 1) * RWKV_HEAD] for j in range(HEADS_PER_GROUP)], axis=1)


def _wkv_masks(reverse):
    c = WKV_CHUNK
    rows = lax.broadcasted_iota(jnp.int32, (c, GROUP_LANES), 0)
    cols = lax.broadcasted_iota(jnp.int32, (c, GROUP_LANES), 1) % RWKV_HEAD
    if reverse:
        return cols > rows, cols >= rows
    return cols < rows, cols <= rows


def _bd_mask():
    bd_r = lax.broadcasted_iota(jnp.int32, (GROUP_LANES, GROUP_LANES), 0) // RWKV_HEAD
    bd_c = lax.broadcasted_iota(jnp.int32, (GROUP_LANES, GROUP_LANES), 1) // RWKV_HEAD
    return bd_r == bd_c


def _wkv_prepare(chains):
    c = WKV_CHUNK
    n = len(chains)
    rng = range(n)
    rows = lax.broadcasted_iota(jnp.int32, (c, GROUP_LANES), 0)
    cols = lax.broadcasted_iota(jnp.int32, (c, GROUP_LANES), 1) % RWKV_HEAD
    eye = (cols == rows).astype(F32)
    bmask = _bd_mask()
    masks = {rev: _wkv_masks(rev) for rev in sorted({ch[6] for ch in chains})}
    strict = [masks[ch[6]][0] for ch in chains]
    incl = [masks[ch[6]][1] for ch in chains]
    lw, kd, bb, kk, v, r = ([ch[i] for ch in chains] for i in range(6))
    bd = lambda x: _block_diag(x.astype(BF16), bmask)

    cs = [_cumsum_rows(lw[i], chains[i][6]) for i in rng]
    total = [cs[i][0:1] if chains[i][6] else cs[i][c - 1:c] for i in rng]
    a_s = [-kk[i] * jnp.exp(cs[i] - lw[i]) for i in rng]
    g_inv = [jnp.exp(-cs[i]) for i in rng]
    b_s = [bb[i] * g_inv[i] for i in rng]
    k_s = [kd[i] * g_inv[i] for i in rng]
    r_s = [r[i] * jnp.exp(cs[i]) for i in rng]
    g_rest = [jnp.exp(total[i] - cs[i]) for i in rng]
    b_e = [bb[i] * g_rest[i] for i in rng]
    k_e = [kd[i] * g_rest[i] for i in rng]
    g_end = [jnp.exp(total[i]) for i in rng]

    lhs = [jnp.concatenate([a_s[i], r_s[i]], axis=0).astype(BF16) for i in rng]
    pb = [_dot(lhs[i], bd(_block_transpose(b_s[i]))) for i in rng]
    pk = [_dot(lhs[i], bd(_block_transpose(k_s[i]))) for i in rng]
    l_ab = [jnp.where(strict[i], pb[i][:c], 0.0) for i in rng]
    m_rb = [jnp.where(incl[i], pb[i][c:], 0.0).astype(BF16) for i in rng]
    l_ak = [jnp.where(strict[i], pk[i][:c], 0.0) for i in rng]
    m_rk = [jnp.where(incl[i], pk[i][c:], 0.0) for i in rng]

    blk_r, blk_c = rows, cols

    def coupling(i, w):
        pair = ((blk_r // (2 * w)) == (blk_c // (2 * w))) & ((blk_r // w) != (blk_c // w))
        return jnp.where(pair, l_ab[i], 0.0)

    tmat = [eye + coupling(i, 1) for i in rng]
    w = 2
    while w < c:
        ed = [_dot(coupling(i, w).astype(BF16), bd(tmat[i])) for i in rng]
        tmat = [tmat[i] + _dot(tmat[i].astype(BF16), bd(ed[i])) for i in rng]
        w *= 2
    t_bf = [tmat[i].astype(BF16) for i in rng]

    b_et = [_block_transpose(b_e[i]).astype(BF16) for i in rng]
    k_et = [_block_transpose(k_e[i]).astype(BF16) for i in rng]
    zv = [_dot(jnp.concatenate([l_ak[i].astype(BF16), m_rk[i].astype(BF16), k_et[i]], axis=0), bd(v[i]))
          for i in rng]
    a_hat = [_dot(t_bf[i], bd(a_s[i])) for i in rng]
    u_til = [_dot(t_bf[i], bd(zv[i][:c])) for i in rng]
    mb = [jnp.concatenate([m_rb[i], b_et[i]], axis=0) for i in rng]
    ra = [_dot(mb[i], bd(a_hat[i])) for i in rng]
    yu = [_dot(mb[i], bd(u_til[i])) for i in rng]
    r_hat = [r_s[i] + ra[i][:c] for i in rng]
    y_til = [zv[i][c:2 * c] + yu[i][:c] for i in rng]
    g_t = [ra[i][c:] for i in rng]
    h_t = [yu[i][c:] + zv[i][2 * c:] for i in rng]
    decay_t = [_block_transpose(jnp.broadcast_to(g_end[i], (c, GROUP_LANES))) for i in rng]
    return [(jnp.concatenate([r_hat[i], g_t[i]], axis=0).astype(BF16), y_til[i], h_t[i], decay_t[i])
            for i in rng]


def _wkv_kernel(lwf, kdf, bf_, kkf, vf, rf, lwr, kdr, br, kkr, vr, rr, yf_ref, yr_ref, s_ref):
    @pl.when(pl.program_id(1) == 0)
    def _():
        s_ref[...] = jnp.zeros(s_ref.shape, F32)

    c = WKV_CHUNK
    n_sub = lwf.shape[1] // c
    n_groups = lwf.shape[2] // GROUP_LANES
    dirs = (((lwf, kdf, bf_, kkf, vf, rf), yf_ref), ((lwr, kdr, br, kkr, vr, rr), yr_ref))
    chains, keys = [], []
    for u in range(n_sub):
        rs = slice(u * c, (u + 1) * c)
        for j in range(n_groups):
            sl = slice(j * GROUP_LANES, (j + 1) * GROUP_LANES)
            for d, (refs, _) in enumerate(dirs):
                lw, kd, bb, kk, v, r = (ref[0, rs, sl] for ref in refs)
                chains.append((lw, kd.astype(F32), bb.astype(F32), kk.astype(F32), v, r.astype(F32),
                               d == 1))
                keys.append((u, j, d))
    prepared = dict(zip(keys, _wkv_prepare(chains)))

    bmask = _bd_mask()
    for j in range(n_groups):
        sl = slice(j * GROUP_LANES, (j + 1) * GROUP_LANES)
        for d, (_, y_ref) in enumerate(dirs):
            s_t = s_ref[d, :, sl]
            for u in (range(n_sub) if d == 0 else reversed(range(n_sub))):
                rg, y_til, h_t, decay_t = prepared[(u, j, d)]
                z = _dot(rg, _block_diag(s_t.astype(BF16), bmask))
                y_ref[0, u * c:(u + 1) * c, sl] = (y_til + z[:c]).astype(y_ref.dtype)
                s_t = s_t * decay_t + z[c:] + h_t
            s_ref[d, :, sl] = s_t


def _wkv_scan(lw0, lw1, kd0, kd1, b0, b1, kk, v, r, ctx_len):
    b, nt, d = v.shape
    c = WKV_CHUNKS_PER_STEP * WKV_CHUNK
    assert ctx_len % c == 0 and nt % c == 0
    n_steps = nt // c
    ctx_chunks = ctx_len // c

    def fwd(i, s):
        return (i, s, 0)

    def rev(i, s):
        return (i, jnp.where(s < ctx_chunks, ctx_chunks - 1 - s, n_steps - 1 + ctx_chunks - s), 0)

    blk_f = pl.BlockSpec((1, c, d), fwd)
    blk_r = pl.BlockSpec((1, c, d), rev)
    return pl.pallas_call(
        _wkv_kernel,
        out_shape=(jax.ShapeDtypeStruct((b, nt, d), BF16), jax.ShapeDtypeStruct((b, nt, d), BF16)),
        grid=(b, n_steps),
        in_specs=[blk_f] * 6 + [blk_r] * 6,
        out_specs=(blk_f, blk_r),
        scratch_shapes=[pltpu.VMEM((2, RWKV_HEAD, d), F32)],
        compiler_params=_cparams(2),
        name="wkv_scan",
    )(lw0, kd0, b0, kk, v, r, lw1, kd1, b1, kk, v, r)


def _read_rwkv_mix(refs):
    yf_ref, yr_ref, r_ref, v_ref, kd0_ref, kd1_ref, gate_ref, rk_ref, lg_ref, lb_ref, seg_ref = refs
    wkv = yf_ref[0].astype(F32) + yr_ref[0].astype(F32)
    inv_n = 1.0 / RWKV_HEAD
    mu = _group_sum(wkv, seg_ref, two_pass=True) * inv_n
    dev = wkv - mu
    var = _group_sum(dev * dev, seg_ref) * inv_n
    gn = dev * lax.rsqrt(var + GN_EPS) * lg_ref[...] + lb_ref[...]
    rk = r_ref[0].astype(F32) * rk_ref[...]
    kd_sum = kd0_ref[0].astype(F32) + kd1_ref[0].astype(F32)
    bonus = _group_sum(rk * kd_sum, seg_ref) * v_ref[0].astype(F32)
    return ((gn + bonus) * gate_ref[0].astype(F32)).astype(BF16)


def _rwkv_mixer(yf, yr, r, v, kd0, kd1, gate, r_k, ln_g, ln_b, seg, ctx_len):
    b, nt, d = v.shape
    tm = ROW_TILE
    off = ctx_len // tm
    tok = pl.BlockSpec((1, tm, d), lambda i, t: (i, t + off, 0))
    specs = [tok] * 7 + [_const_spec((1, d))] * 3 + [_const_spec((GROUP_LANES, GROUP_LANES))]
    args = [yf, yr, r, v, kd0, kd1, gate, r_k.reshape(1, d), ln_g.reshape(1, d), ln_b.reshape(1, d), seg]
    return _Rows(_read_rwkv_mix, specs, args, (b, nt - ctx_len, d))


def _rope_tables(ctx_len, seq_len):
    f32 = np.float32
    rows = seq_len // GRID_W
    row = np.repeat(np.arange(rows, dtype=f32), GRID_W)
    col = np.tile(np.arange(GRID_W, dtype=f32), rows)
    inv_freq = np.power(f32(ROPE_THETA), -np.arange(ROPE_PAIRS, dtype=f32) / f32(ROPE_PAIRS)).astype(f32)
    row_ang, col_ang = row[:, None] * inv_freq, col[:, None] * inv_freq
    ang = np.concatenate([row_ang, row_ang, col_ang, col_ang], axis=1).astype(np.float64)
    cos = np.concatenate([np.ones((ctx_len, HEAD_DIM)), np.cos(ang)], axis=0)
    sin = np.concatenate([np.zeros((ctx_len, HEAD_DIM)), np.sin(ang)], axis=0)
    sign = np.concatenate([-np.ones(ROPE_PAIRS), np.ones(ROPE_PAIRS)] * 2)
    reps = V7X_LANES // HEAD_DIM
    return (jnp.asarray(np.tile(cos, (1, reps)), F32), jnp.asarray(np.tile(sin * sign, (1, reps)), F32))


def kernel(x, c, ctx, c_ctx, mod_w, mod_b, norm1_g, norm2_g, ffn_wg, ffn_wu, ffn_wd, attn_wqkv,
           attn_q_gain, attn_k_gain, attn_wo, rwkv_mix, rwkv_wrkv, rwkv_w0, rwkv_w1, rwkv_w2, rwkv_a0,
           rwkv_a1, rwkv_a2, rwkv_g1, rwkv_g2, rwkv_k_k, rwkv_k_a, rwkv_r_k, rwkv_ln_g, rwkv_ln_b,
           rwkv_wo, final_g):
    b, seq_len, d = x.shape
    ctx_len = ctx.shape[1]
    depth = mod_w.shape[0]
    assert depth == 2 and d == N_HEADS * HEAD_DIM
    assert ctx_len % ROW_TILE == 0 and seq_len % ROW_TILE == 0 and seq_len % GRID_W == 0
    ctx_tiles = ctx_len // ROW_TILE

    n_rows = -(-(b + 1) // 8) * 8
    c_rows = jnp.concatenate([c, c_ctx[None], jnp.zeros((n_rows - b - 1, d), F32)], axis=0)
    m_all = _modulation(c_rows, mod_w, mod_b).reshape(depth, n_rows, 6, d)
    pad = jnp.zeros((depth, b, N_MOD_ROWS - 6, d), F32)
    lat = jnp.concatenate([m_all[:, :b], pad], axis=2)
    con = jnp.concatenate([jnp.broadcast_to(m_all[:, b:b + 1], (depth, b, 6, d)), pad], axis=2)
    mods = jnp.stack([con, lat], axis=2)

    cos2, sin2 = _rope_tables(ctx_len, seq_len)
    seg = jnp.asarray(np.kron(np.eye(HEADS_PER_GROUP), np.ones((HEAD_DIM, HEAD_DIM))), BF16)

    stream = _stream_of(ctx, x)

    q, k, vt = _qkv_project(stream, mods[0], norm1_g[0], attn_wqkv[0], attn_q_gain[0], attn_k_gain[0],
                            cos2, sin2, seg, ctx_tiles)
    o_ctx, o_lat = _attention(q, k, vt, _score_bound(attn_q_gain[0], attn_k_gain[0]), ctx_len)
    xs = _out_ffn(_attention_mixer(o_ctx, o_lat), stream, mods[0], norm2_g[0], attn_wo[0], ffn_wg[0],
                  ffn_wu[0], ffn_wd[0], final_g, first_tile=0, ctx_tiles=ctx_tiles, final_norm=False)

    r, v, kk, gate, lw0, lw1, kd0, kd1, b0, b1 = _rwkv_project(
        xs, mods[1], norm1_g[1], rwkv_mix[0], rwkv_wrkv[0], rwkv_w0[0], rwkv_w1[0], rwkv_w2[0],
        rwkv_a0[0], rwkv_a1[0], rwkv_a2[0], rwkv_g1[0], rwkv_g2[0], rwkv_k_k[0], rwkv_k_a[0], seg,
        ctx_len)
    yf, yr = _wkv_scan(lw0, lw1, kd0, kd1, b0, b1, kk, v, r, ctx_len)
    mix = _rwkv_mixer(yf, yr, r, v, kd0, kd1, gate, rwkv_r_k[0].reshape(-1), rwkv_ln_g[0],
                      rwkv_ln_b[0], seg, ctx_len)
    return _out_ffn(mix, _rows_from(xs, ctx_tiles), mods[1], norm2_g[1], rwkv_wo[0], ffn_wg[1], ffn_wu[1],
                    ffn_wd[1], final_g, first_tile=ctx_tiles, ctx_tiles=ctx_tiles, final_norm=True)
```

```python
import functools
from typing import Callable, NamedTuple

import jax
import jax.numpy as jnp
import numpy as np
from jax import lax
from jax.experimental import pallas as pl
from jax.experimental.pallas import tpu as pltpu

F32 = jnp.float32
BF16 = jnp.bfloat16

NORM_EPS = 1e-6
GN_EPS = 64e-5
GRID_W = 64
N_HEADS = 16
N_KV_HEADS = 4
KV_GROUP = N_HEADS // N_KV_HEADS
HEAD_DIM = 64
ROPE_THETA = 10000.0
ROPE_PAIRS = HEAD_DIM // 4
RWKV_HEAD = 64
DECAY_SCALE = float(np.exp(-0.5))
LOG2_E = float(np.log2(np.e))
EXP2_SAFE_RANGE = 120.0

V7X_LANES = 128
V7X_MXU_DIM = 256
V7X_VMEM_LIMIT_BYTES = 60000 * 1024

ROW_TILE = 256
GROUP_LANES = V7X_MXU_DIM
HEADS_PER_GROUP = GROUP_LANES // HEAD_DIM
BF16_SUBLANES = 16
ATTN_V_ROWS = HEAD_DIM + BF16_SUBLANES
ATTN_CHUNKS_PER_PV = 11
WKV_CHUNK = 64
WKV_CHUNKS_PER_STEP = 2
N_MOD_ROWS = 8


def _cparams(n_axes):
    return pltpu.CompilerParams(
        dimension_semantics=("arbitrary",) * n_axes,
        vmem_limit_bytes=V7X_VMEM_LIMIT_BYTES,
    )


def _const_spec(shape):
    nd = len(shape)
    return pl.BlockSpec(shape, lambda *_: (0,) * nd, pipeline_mode=pl.Buffered(1))


def _dot(a, b):
    return jnp.dot(a, b, preferred_element_type=F32)


def _dot_nt(a, b):
    return lax.dot_general(a, b, (((1,), (1,)), ((), ())), preferred_element_type=F32)


def _norm_mod(x, g, shift, scale):
    ms = jnp.mean(x * x, axis=-1, keepdims=True)
    return (x * lax.rsqrt(ms + NORM_EPS) * g) * (1.0 + scale) + shift


def _group_sum(x, seg_ref, two_pass=False):
    seg = seg_ref[...]
    outs = []
    for j in range(x.shape[1] // GROUP_LANES):
        xs = x[:, j * GROUP_LANES:(j + 1) * GROUP_LANES]
        hi = xs.astype(BF16)
        out = _dot(hi, seg)
        if two_pass:
            out = out + _dot((xs - hi.astype(F32)).astype(BF16), seg)
        outs.append(out)
    return outs[0] if len(outs) == 1 else jnp.concatenate(outs, axis=1)


def _mod_kernel(c_ref, w_ref, b_ref, o_ref):
    c = c_ref[...]
    s = c * jax.nn.sigmoid(c)
    o_ref[0] = _dot(s.astype(BF16), w_ref[0]) + b_ref[0]


def _modulation(c_rows, mod_w, mod_b):
    depth, d, n = mod_w.shape
    rows = c_rows.shape[0]
    tn = n // 4
    return pl.pallas_call(
        _mod_kernel,
        out_shape=jax.ShapeDtypeStruct((depth, rows, n), F32),
        grid=(depth, n // tn),
        in_specs=[
            pl.BlockSpec((rows, d), lambda i, j: (0, 0)),
            pl.BlockSpec((1, d, tn), lambda i, j: (i, 0, j)),
            pl.BlockSpec((1, 1, tn), lambda i, j: (i, 0, j)),
        ],
        out_specs=pl.BlockSpec((1, rows, tn), lambda i, j: (i, 0, j)),
        compiler_params=_cparams(2),
        name="modulation",
    )(c_rows, mod_w.astype(BF16), mod_b.reshape(depth, 1, n))


def _rope(x, cos, sin_signed):
    w = x.shape[1]
    lane = lax.broadcasted_iota(jnp.int32, x.shape, 1)
    first_half = (lane % (2 * ROPE_PAIRS)) < ROPE_PAIRS
    partner = jnp.where(first_half, pltpu.roll(x, w - ROPE_PAIRS, 1), pltpu.roll(x, ROPE_PAIRS, 1))
    return x * cos + partner * sin_signed


class _Rows(NamedTuple):
    read: Callable
    specs: list
    args: list
    shape: tuple


def _stream_of(ctx, x):
    b, c, d = ctx.shape
    tm = ROW_TILE
    ctx_tiles = c // tm

    def read(refs):
        c_ref, x_ref = refs
        return jnp.where(pl.program_id(1) < ctx_tiles, c_ref[0], x_ref[0])

    specs = [pl.BlockSpec((1, tm, d), lambda i, t: (i, jnp.minimum(t, ctx_tiles - 1), 0)),
             pl.BlockSpec((1, tm, d), lambda i, t: (i, jnp.maximum(t - ctx_tiles, 0), 0))]
    return _Rows(read, specs, [ctx, x], (b, c + x.shape[1], d))


def _rows_from(xs, tile_offset):
    b, n, d = xs.shape
    spec = pl.BlockSpec((1, ROW_TILE, d), lambda i, t: (i, t + tile_offset, 0))
    return _Rows(lambda refs: refs[0][0], [spec], [xs], (b, n - tile_offset * ROW_TILE, d))


def _qkv_kernel(*refs, nq, nk, read_x, n_x):
    x = read_x(refs[:n_x])
    (mod_ref, g_ref, w_ref, qg_ref, kg_ref, cos_ref, sin_ref, seg_ref, ek_ref, ev_ref,
     q_ref, k_ref, v_ref) = refs[n_x:]
    mod = mod_ref[0, 0]
    h = _norm_mod(x, g_ref[...], mod[0:1], mod[1:2]).astype(BF16)
    qkv = _dot(h, w_ref[...])
    q, k, v = qkv[:, :nq], qkv[:, nq:nq + nk], qkv[:, nq + nk:]
    cos2, sin2 = cos_ref[...], sin_ref[...]

    def head_norm_rope(z, gain):
        reps = z.shape[1] // V7X_LANES
        cos = jnp.concatenate([cos2] * reps, axis=1)
        sin = jnp.concatenate([sin2] * reps, axis=1)
        ss = _group_sum(z * z, seg_ref)
        zn = z * lax.rsqrt(ss * (1.0 / HEAD_DIM) + NORM_EPS) * gain
        return _rope(zn, cos, sin)

    qn = head_norm_rope(q, qg_ref[...]) * (HEAD_DIM ** -0.5 * LOG2_E)
    q_ref[0] = qn.astype(BF16)
    kn = head_norm_rope(k, kg_ref[...]).astype(BF16)
    k_t = _dot(kn, ek_ref[...]).astype(BF16)
    v_bf = v.astype(BF16)
    ones = jnp.ones((ATTN_V_ROWS - HEAD_DIM, v_bf.shape[0]), BF16)
    shift_lane = lax.broadcasted_iota(jnp.int32, (v_bf.shape[0], V7X_LANES), 1) == HEAD_DIM
    for j in range(N_KV_HEADS):
        k_ref[0, j] = jnp.where(shift_lane, jnp.ones((), BF16), k_t[:, j * V7X_LANES:(j + 1) * V7X_LANES])
        v_ref[0, j, 0:HEAD_DIM, :] = _dot_nt(ev_ref[j], v_bf).astype(BF16)
        v_ref[0, j, HEAD_DIM:, :] = ones


def _head_select(n_heads, rows):
    sel = np.zeros((n_heads, rows, n_heads * HEAD_DIM), np.float32)
    for j in range(n_heads):
        sel[j, np.arange(HEAD_DIM), j * HEAD_DIM + np.arange(HEAD_DIM)] = 1.0
    return sel


def _qkv_project(rows, mods, g1, wqkv, q_gain, k_gain, cos2, sin2, seg, ctx_tiles):
    b, nt, d = rows.shape
    nq, nk = N_HEADS * HEAD_DIM, N_KV_HEADS * HEAD_DIM
    tm = ROW_TILE
    ek = np.zeros((nk, N_KV_HEADS * V7X_LANES), np.float32)
    for j in range(N_KV_HEADS):
        ek[j * HEAD_DIM + np.arange(HEAD_DIM), j * V7X_LANES + np.arange(HEAD_DIM)] = 1.0
    ev = _head_select(N_KV_HEADS, HEAD_DIM)
    kern = functools.partial(_qkv_kernel, nq=nq, nk=nk, read_x=rows.read, n_x=len(rows.specs))
    return pl.pallas_call(
        kern,
        out_shape=(
            jax.ShapeDtypeStruct((b, nt, nq), BF16),
            jax.ShapeDtypeStruct((b, N_KV_HEADS, nt, V7X_LANES), BF16),
            jax.ShapeDtypeStruct((b, N_KV_HEADS, ATTN_V_ROWS, nt), BF16),
        ),
        grid=(b, nt // tm),
        in_specs=rows.specs + [
            pl.BlockSpec((1, 1, N_MOD_ROWS, d), lambda i, t: (i, jnp.where(t < ctx_tiles, 0, 1), 0, 0)),
            _const_spec((1, d)),
            _const_spec((d, nq + 2 * nk)),
            _const_spec((1, nq)),
            _const_spec((1, nk)),
            pl.BlockSpec((tm, V7X_LANES), lambda i, t: (t, 0)),
            pl.BlockSpec((tm, V7X_LANES), lambda i, t: (t, 0)),
            _const_spec((GROUP_LANES, GROUP_LANES)),
            _const_spec(ek.shape),
            _const_spec(ev.shape),
        ],
        out_specs=(
            pl.BlockSpec((1, tm, nq), lambda i, t: (i, t, 0)),
            pl.BlockSpec((1, N_KV_HEADS, tm, V7X_LANES), lambda i, t: (i, 0, t, 0)),
            pl.BlockSpec((1, N_KV_HEADS, ATTN_V_ROWS, tm), lambda i, t: (i, 0, 0, t)),
        ),
        compiler_params=_cparams(2),
        name="qkv_project",
    )(*rows.args, mods, g1.reshape(1, d), wqkv.astype(BF16),
      jnp.tile(q_gain, N_HEADS).reshape(1, nq), jnp.tile(k_gain, N_KV_HEADS).reshape(1, nk),
      cos2, sin2, seg, jnp.asarray(ek, BF16), jnp.asarray(ev, BF16))


def _attn_kernel(*refs, n_sub, tk, n_chunks, unroll):
    bound_ref = refs[0]
    q_refs = refs[1:1 + n_sub]
    k_ref, vt_ref, o_ref, qt_ref, m_ref, acc_ref, sa_ref, sb_ref, p_ref = refs[1 + n_sub:]
    ts = q_refs[0].shape[1]
    width = qt_ref.shape[1]
    for s, q_ref in enumerate(q_refs):
        q_t = q_ref[0].astype(F32).T
        for g in range(KV_GROUP):
            col = (s * KV_GROUP + g) * ts
            qt_ref[0:HEAD_DIM, col:col + ts] = q_t[g * HEAD_DIM:(g + 1) * HEAD_DIM].astype(BF16)
    qt_ref[HEAD_DIM:, :] = jnp.full((qt_ref.shape[0] - HEAD_DIM, width), bound_ref[0], F32).astype(BF16)
    acc_ref[...] = jnp.zeros(acc_ref.shape, F32)

    def chunk(c, n=1):
        return pl.ds(c * tk if isinstance(c, int) else pl.multiple_of(c * tk, n * tk), n * tk)

    def scores(c):
        return _dot(k_ref[0, 0, chunk(c), :], qt_ref[...])

    @pl.when(bound_ref[1] > 0.5)
    def _():
        def accumulate(c0, n):
            for u in range(n):
                p_ref[u * tk:(u + 1) * tk, :] = jnp.exp2(scores(c0 + u)).astype(BF16)
            acc_ref[...] += _dot(vt_ref[0, 0, :, chunk(c0, n)], p_ref[0:n * tk, :])

        per_trip = p_ref.shape[0] // tk

        def trip(j, carry):
            accumulate(per_trip * j, per_trip)
            return carry

        n_trips, rem = n_chunks // per_trip, n_chunks % per_trip
        if n_trips:
            lax.fori_loop(0, n_trips, trip, 0)
        if rem:
            accumulate(n_trips * per_trip, rem)

    @pl.when(bound_ref[1] <= 0.5)
    def _():
        _attn_online_softmax(scores, chunk, vt_ref, m_ref, acc_ref, sa_ref, sb_ref, p_ref,
                             n_chunks=n_chunks, unroll=unroll)

    acc = acc_ref[...]
    ot = acc[0:HEAD_DIM] / acc[HEAD_DIM:HEAD_DIM + 1]
    for s in range(n_sub):
        heads = [ot[:, (s * KV_GROUP + g) * ts:(s * KV_GROUP + g + 1) * ts] for g in range(KV_GROUP)]
        o_ref[0, s * ts:(s + 1) * ts, :] = jnp.concatenate(heads, axis=0).T.astype(o_ref.dtype)


def _attn_online_softmax(scores, chunk, vt_ref, m_ref, acc_ref, sa_ref, sb_ref, p_ref, *, n_chunks, unroll):
    m_ref[...] = jnp.full(m_ref.shape, -jnp.inf, F32)
    tk = sa_ref.shape[0]

    def update(c, s_ref):
        vtc = vt_ref[0, 0, :, chunk(c)]
        alphas = []
        for cb in range(s_ref.shape[1] // V7X_LANES):
            cols = slice(cb * V7X_LANES, (cb + 1) * V7X_LANES)
            st = s_ref[:, cols]
            m_prev = m_ref[:, cols]
            m_new = jnp.maximum(m_prev, jnp.max(st, axis=0, keepdims=True))
            alphas.append(jnp.exp2(m_prev - m_new))
            p_ref[0:tk, cols] = jnp.exp2(st - m_new).astype(BF16)
            m_ref[:, cols] = m_new
        alpha = jnp.concatenate(alphas, axis=1)
        acc_ref[...] = alpha * acc_ref[...] + _dot(vtc, p_ref[0:tk, :])

    slots = (sa_ref, sb_ref)
    sa_ref[...] = scores(0)

    def body(j, carry):
        c0 = unroll * j
        for u in range(unroll):
            slots[(u + 1) % 2][...] = scores(c0 + u + 1)
            update(c0 + u, slots[u % 2])
        return carry

    if n_chunks > 1:
        lax.fori_loop(0, (n_chunks - 1) // unroll, body, 0)
    update(n_chunks - 1, slots[0])


def _attention_call(bound, q, k, vt, *, row0, n_rows, n_keys, n_sub, name):
    b, _, nq = q.shape
    ts = ROW_TILE
    tk = ROW_TILE
    tq = n_sub * ts
    n_chunks = n_keys // tk
    assert n_rows % tq == 0 and row0 % ts == 0 and n_keys % tk == 0 and n_chunks % 2 == 1
    unroll = max(u for u in (2, 4) if (n_chunks - 1) % u == 0)
    kern = functools.partial(_attn_kernel, n_sub=n_sub, tk=tk, n_chunks=n_chunks, unroll=unroll)
    width = n_sub * KV_GROUP * ts

    def q_spec(s):
        return pl.BlockSpec((1, ts, GROUP_LANES), lambda i, j, t: (i, row0 // ts + n_sub * t + s, j))

    return pl.pallas_call(
        kern,
        out_shape=jax.ShapeDtypeStruct((b, n_rows, nq), BF16),
        grid=(b, N_KV_HEADS, n_rows // tq),
        in_specs=[pl.BlockSpec(memory_space=pltpu.SMEM)] + [q_spec(s) for s in range(n_sub)] + [
            pl.BlockSpec((1, 1, n_keys, V7X_LANES), lambda i, j, t: (i, j, 0, 0)),
            pl.BlockSpec((1, 1, ATTN_V_ROWS, n_keys), lambda i, j, t: (i, j, 0, 0)),
        ],
        out_specs=pl.BlockSpec((1, tq, GROUP_LANES), lambda i, j, t: (i, t, j)),
        scratch_shapes=[
            pltpu.VMEM((V7X_LANES, width), BF16),
            pltpu.VMEM((1, width), F32),
            pltpu.VMEM((ATTN_V_ROWS, width), F32),
            pltpu.VMEM((tk, width), F32),
            pltpu.VMEM((tk, width), F32),
            pltpu.VMEM((min(ATTN_CHUNKS_PER_PV, n_chunks) * tk, width), BF16),
        ],
        compiler_params=_cparams(3),
        name=name,
    )(bound, *([q] * n_sub), k, vt)


def _score_bound(q_gain, k_gain):
    bound = (HEAD_DIM * (HEAD_DIM ** -0.5 * LOG2_E) * 1.02) * jnp.max(jnp.abs(q_gain)) * jnp.max(jnp.abs(k_gain))
    bound = bound.astype(BF16).astype(F32)
    return jnp.stack([-bound, (2.0 * bound <= EXP2_SAFE_RANGE).astype(F32)])


def _attention(q, k, vt, bound, ctx_len):
    nt = q.shape[1]
    n_lat = nt - ctx_len
    n_sub = max(s for s in (1, 2, 4) if n_lat % (s * ROW_TILE) == 0)
    o_ctx = _attention_call(bound, q, k, vt, row0=0, n_rows=ctx_len, n_keys=ctx_len, n_sub=1,
                            name="flash_attention_ctx")
    o_lat = _attention_call(bound, q, k, vt, row0=ctx_len, n_rows=n_lat, n_keys=nt, n_sub=n_sub,
                            name="flash_attention")
    return o_ctx, o_lat


def _out_ffn_kernel(*refs, final_norm, read_mix, n_mix, read_x, n_x):
    y = read_mix(refs[:n_mix])
    x = read_x(refs[n_mix:n_mix + n_x])
    mod_ref, g_ref, wo_ref, wg_ref, wu_ref, wd_ref, fg_ref, o_ref = refs[n_mix + n_x:]
    mod = mod_ref[0, 0]
    x1 = x + mod[2:3] * _dot(y, wo_ref[...])
    h2 = _norm_mod(x1, g_ref[...], mod[3:4], mod[4:5]).astype(BF16)
    a = _dot(h2, wg_ref[...])
    u = _dot(h2, wu_ref[...])
    hid = (a * jax.nn.sigmoid(a) * u).astype(BF16)
    x2 = x1 + mod[5:6] * _dot(hid, wd_ref[...])
    if final_norm:
        ms = jnp.mean(x2 * x2, axis=-1, keepdims=True)
        x2 = x2 * lax.rsqrt(ms + NORM_EPS) * fg_ref[...]
    o_ref[0] = x2


def _attention_mixer(o_ctx, o_lat):
    d = o_lat.shape[2]
    tm = ROW_TILE
    ctx_tiles = o_ctx.shape[1] // tm

    def read(refs):
        yc_ref, yl_ref = refs
        return jnp.where(pl.program_id(1) < ctx_tiles, yc_ref[0], yl_ref[0])

    specs = [pl.BlockSpec((1, tm, d), lambda i, t: (i, jnp.minimum(t, ctx_tiles - 1), 0)),
             pl.BlockSpec((1, tm, d), lambda i, t: (i, jnp.maximum(t - ctx_tiles, 0), 0))]
    return _Rows(read, specs, [o_ctx, o_lat], (o_lat.shape[0], o_ctx.shape[1] + o_lat.shape[1], d))


def _out_ffn(mix, rows, mods, g2, wo, wg, wu, wd, final_g, *, first_tile, ctx_tiles, final_norm):
    b, n, d = rows.shape
    assert mix.shape == rows.shape
    f = wg.shape[1]
    tm = ROW_TILE
    kern = functools.partial(_out_ffn_kernel, final_norm=final_norm, read_mix=mix.read,
                             n_mix=len(mix.specs), read_x=rows.read, n_x=len(rows.specs))

    def mod_sel(i, t):
        return (i, jnp.where(t + first_tile < ctx_tiles, 0, 1), 0, 0)

    return pl.pallas_call(
        kern,
        out_shape=jax.ShapeDtypeStruct((b, n, d), F32),
        grid=(b, n // tm),
        in_specs=mix.specs + rows.specs + [
            pl.BlockSpec((1, 1, N_MOD_ROWS, d), mod_sel),
            _const_spec((1, d)),
            _const_spec((d, d)),
            _const_spec((d, f)),
            _const_spec((d, f)),
            _const_spec((f, d)),
            _const_spec((1, d)),
        ],
        out_specs=pl.BlockSpec((1, tm, d), lambda i, t: (i, t, 0)),
        compiler_params=_cparams(2),
        name="out_ffn_final" if final_norm else "out_ffn",
    )(*mix.args, *rows.args, mods, g2.reshape(1, d), wo.astype(BF16), wg.astype(BF16), wu.astype(BF16),
      wd.astype(BF16), final_g.reshape(1, d))


def _rwkv_proj_kernel(x_ref, xp_ref, xn_ref, mod_ref, g_ref, mix_ref, wr_ref, wk_ref, wv_ref,
                      w1_ref, w2_ref, a1_ref, a2_ref, g1_ref, g2_ref, w0_ref, a0_ref, kk_ref, ka_ref,
                      seg_ref,
                      r_ref, v_ref, kkn_ref, gate_ref, lw0_ref, lw1_ref, kd0_ref, kd1_ref,
                      b0_ref, b1_ref, *, tm, seq_starts, seq_ends):
    t = pl.program_id(1)
    mod = mod_ref[0, 0]
    g = g_ref[...]
    h = _norm_mod(x_ref[0], g, mod[0:1], mod[1:2])
    h_prev = _norm_mod(xp_ref[0], g, mod[0:1], mod[1:2])[7:8]
    h_next = _norm_mod(xn_ref[0], g, mod[0:1], mod[1:2])[0:1]
    row = lax.broadcasted_iota(jnp.int32, h.shape, 0)
    pos = row + t * tm
    at_start = functools.reduce(jnp.logical_or, [pos == s for s in seq_starts])
    at_end = functools.reduce(jnp.logical_or, [pos == e for e in seq_ends])
    before = jnp.where(row == 0, h_prev, pltpu.roll(h, 1, 0))
    after = jnp.where(row == tm - 1, h_next, pltpu.roll(h, tm - 1, 0))
    before = jnp.where(at_start, 0.0, before)
    after = jnp.where(at_end, 0.0, after)
    xx = 0.5 * (before + after) - h
    mix = mix_ref[...]

    def lerp(j):
        return (h + xx * mix[j:j + 1]).astype(BF16)

    r_ref[0] = _dot(lerp(0), wr_ref[...]).astype(r_ref.dtype)
    k = _dot(lerp(2), wk_ref[...])
    v_ref[0] = _dot(lerp(3), wv_ref[...]).astype(v_ref.dtype)
    kkr = k * kk_ref[...]
    ss = _group_sum(kkr * kkr, seg_ref)
    kkn = kkr * lax.rsqrt(jnp.maximum(ss, 1e-24))
    kkn_ref[0] = kkn.astype(kkn_ref.dtype)
    xw, xa = lerp(1), lerp(4)
    ka = ka_ref[...]
    for d, (lw_ref, kd_ref, b_ref) in enumerate(((lw0_ref, kd0_ref, b0_ref), (lw1_ref, kd1_ref, b1_ref))):
        z = w0_ref[d:d + 1] + _dot(jnp.tanh(_dot(xw, w1_ref[d])).astype(BF16), w2_ref[d])
        lw_ref[0] = -DECAY_SCALE * jax.nn.sigmoid(z)
        a = jax.nn.sigmoid(a0_ref[d:d + 1] + _dot(_dot(xa, a1_ref[d]).astype(BF16), a2_ref[d]))
        kd_ref[0] = (k * (1.0 + (a - 1.0) * ka)).astype(kd_ref.dtype)
        b_ref[0] = (kkn * a).astype(b_ref.dtype)
    gate = _dot(jax.nn.sigmoid(_dot(lerp(5), g1_ref[...])).astype(BF16), g2_ref[...])
    gate_ref[0] = gate.astype(gate_ref.dtype)


def _rwkv_project(xs, mods, g1n, mix, w_rkv, w0, w1, w2, a0, a1, a2, gw1, gw2, k_k, k_a, seg,
                  ctx_len):
    b, nt, d = xs.shape
    tm = ROW_TILE
    halo = 8
    per = tm // halo
    n_halo = nt // halo
    kern = functools.partial(_rwkv_proj_kernel, tm=tm, seq_starts=(0, ctx_len),
                             seq_ends=(ctx_len - 1, nt - 1))
    tok = pl.BlockSpec((1, tm, d), lambda i, t: (i, t, 0))
    bf = lambda w: w.astype(BF16)
    outs = pl.pallas_call(
        kern,
        out_shape=tuple(jax.ShapeDtypeStruct((b, nt, d), dt)
                        for dt in (BF16, BF16, BF16, BF16, F32, F32, BF16, BF16, BF16, BF16)),
        grid=(b, nt // tm),
        in_specs=[
            tok,
            pl.BlockSpec((1, halo, d), lambda i, t: (i, jnp.maximum(t * per - 1, 0), 0)),
            pl.BlockSpec((1, halo, d), lambda i, t: (i, jnp.minimum((t + 1) * per, n_halo - 1), 0)),
            pl.BlockSpec((1, 1, N_MOD_ROWS, d), lambda i, t: (i, jnp.where(t * tm < ctx_len, 0, 1), 0, 0)),
            _const_spec((1, d)),
            _const_spec((N_MOD_ROWS, d)),
            _const_spec((d, d)), _const_spec((d, d)), _const_spec((d, d)),
            _const_spec(w1.shape), _const_spec(w2.shape), _const_spec(a1.shape), _const_spec(a2.shape),
            _const_spec(gw1.shape), _const_spec(gw2.shape),
            _const_spec((2, d)), _const_spec((2, d)), _const_spec((1, d)), _const_spec((1, d)),
            _const_spec((GROUP_LANES, GROUP_LANES)),
        ],
        out_specs=tuple(tok for _ in range(10)),
        compiler_params=_cparams(2),
        name="rwkv_project",
    )(xs, xs, xs, mods, g1n.reshape(1, d),
      jnp.concatenate([mix, jnp.zeros((N_MOD_ROWS - mix.shape[0], d), F32)], axis=0),
      bf(w_rkv[0]), bf(w_rkv[1]), bf(w_rkv[2]), bf(w1), bf(w2), bf(a1), bf(a2), bf(gw1), bf(gw2),
      w0, a0, k_k.reshape(1, d), k_a.reshape(1, d), seg)
    return outs


def _block_diag(x_bf, mask):
    return jnp.where(mask, jnp.concatenate([x_bf] * HEADS_PER_GROUP, axis=0), jnp.zeros((), BF16))


def _cumsum_rows(x, reverse):
    n = x.shape[0]
    row = lax.broadcasted_iota(jnp.int32, x.shape, 0)
    shift = 1
    while shift < n:
        if reverse:
            x = x + jnp.where(row < n - shift, pltpu.roll(x, n - shift, 0), 0.0)
        else:
            x = x + jnp.where(row >= shift, pltpu.roll(x, shift, 0), 0.0)
        shift *= 2
    return x


def _block_transpose(x):
    xt = x.T
    return jnp.concatenate([xt[j * RWKV_HEAD:(j + 1) * RWKV_HEAD] for j in range(HEADS_PER_GROUP)], axis=1)


def _wkv_masks(reverse):
    c = WKV_CHUNK
    rows = lax.broadcasted_iota(jnp.int32, (c, GROUP_LANES), 0)
    cols = lax.broadcasted_iota(jnp.int32, (c, GROUP_LANES), 1) % RWKV_HEAD
    if reverse:
        return cols > rows, cols >= rows
    return cols < rows, cols <= rows


def _bd_mask():
    bd_r = lax.broadcasted_iota(jnp.int32, (GROUP_LANES, GROUP_LANES), 0) // RWKV_HEAD
    bd_c = lax.broadcasted_iota(jnp.int32, (GROUP_LANES, GROUP_LANES), 1) // RWKV_HEAD
    return bd_r == bd_c


def _wkv_prepare(chains):
    c = WKV_CHUNK
    n = len(chains)
    rng = range(n)
    rows = lax.broadcasted_iota(jnp.int32, (c, GROUP_LANES), 0)
    cols = lax.broadcasted_iota(jnp.int32, (c, GROUP_LANES), 1) % RWKV_HEAD
    eye = (cols == rows).astype(F32)
    bmask = _bd_mask()
    masks = {rev: _wkv_masks(rev) for rev in sorted({ch[6] for ch in chains})}
    strict = [masks[ch[6]][0] for ch in chains]
    incl = [masks[ch[6]][1] for ch in chains]
    lw, kd, bb, kk, v, r = ([ch[i] for ch in chains] for i in range(6))
    bd = lambda x: _block_diag(x.astype(BF16), bmask)

    cs = [_cumsum_rows(lw[i], chains[i][6]) for i in rng]
    total = [cs[i][0:1] if chains[i][6] else cs[i][c - 1:c] for i in rng]
    a_s = [-kk[i] * jnp.exp(cs[i] - lw[i]) for i in rng]
    g_inv = [jnp.exp(-cs[i]) for i in rng]
    b_s = [bb[i] * g_inv[i] for i in rng]
    k_s = [kd[i] * g_inv[i] for i in rng]
    r_s = [r[i] * jnp.exp(cs[i]) for i in rng]
    g_rest = [jnp.exp(total[i] - cs[i]) for i in rng]
    b_e = [bb[i] * g_rest[i] for i in rng]
    k_e = [kd[i] * g_rest[i] for i in rng]
    g_end = [jnp.exp(total[i]) for i in rng]

    lhs = [jnp.concatenate([a_s[i], r_s[i]], axis=0).astype(BF16) for i in rng]
    pb = [_dot(lhs[i], bd(_block_transpose(b_s[i]))) for i in rng]
    pk = [_dot(lhs[i], bd(_block_transpose(k_s[i]))) for i in rng]
    l_ab = [jnp.where(strict[i], pb[i][:c], 0.0) for i in rng]
    m_rb = [jnp.where(incl[i], pb[i][c:], 0.0).astype(BF16) for i in rng]
    l_ak = [jnp.where(strict[i], pk[i][:c], 0.0) for i in rng]
    m_rk = [jnp.where(incl[i], pk[i][c:], 0.0) for i in rng]

    blk_r, blk_c = rows, cols

    def coupling(i, w):
        pair = ((blk_r // (2 * w)) == (blk_c // (2 * w))) & ((blk_r // w) != (blk_c // w))
        return jnp.where(pair, l_ab[i], 0.0)

    tmat = [eye + coupling(i, 1) for i in rng]
    w = 2
    while w < c:
        if 2 * w < c:
            ee = [jnp.concatenate([coupling(i, w), coupling(i, 2 * w)], axis=0).astype(BF16) for i in rng]
            mf = [_dot(ee[i], bd(tmat[i])) for i in rng]
            df = [jnp.concatenate([tmat[i], mf[i][c:]], axis=0).astype(BF16) for i in rng]
            z = [_dot(df[i], bd(mf[i][:c])) for i in rng]
            tmat = [tmat[i] + z[i][:c] for i in rng]
            m2 = [mf[i][c:] + z[i][c:] for i in rng]
            tmat = [tmat[i] + _dot(tmat[i].astype(BF16), bd(m2[i])) for i in rng]
            w *= 4
        else:
            ed = [_dot(coupling(i, w).astype(BF16), bd(tmat[i])) for i in rng]
            tmat = [tmat[i] + _dot(tmat[i].astype(BF16), bd(ed[i])) for i in rng]
            w *= 2
    t_bf = [tmat[i].astype(BF16) for i in rng]

    b_et = [_block_transpose(b_e[i]).astype(BF16) for i in rng]
    k_et = [_block_transpose(k_e[i]).astype(BF16) for i in rng]
    zv = [_dot(jnp.concatenate([l_ak[i].astype(BF16), m_rk[i].astype(BF16), k_et[i]], axis=0), bd(v[i]))
          for i in rng]
    a_hat = [_dot(t_bf[i], bd(a_s[i])) for i in rng]
    u_til = [_dot(t_bf[i], bd(zv[i][:c])) for i in rng]
    mb = [jnp.concatenate([m_rb[i], b_et[i]], axis=0) for i in rng]
    ra = [_dot(mb[i], bd(a_hat[i])) for i in rng]
    yu = [_dot(mb[i], bd(u_til[i])) for i in rng]
    r_hat = [r_s[i] + ra[i][:c] for i in rng]
    y_til = [zv[i][c:2 * c] + yu[i][:c] for i in rng]
    g_t = [ra[i][c:] for i in rng]
    h_t = [yu[i][c:] + zv[i][2 * c:] for i in rng]
    decay_t = [_block_transpose(jnp.broadcast_to(g_end[i], (c, GROUP_LANES))) for i in rng]
    return [(jnp.concatenate([r_hat[i], g_t[i]], axis=0).astype(BF16), y_til[i], h_t[i], decay_t[i])
            for i in rng]


def _wkv_kernel(lwf, kdf, bf_, kkf, vf, rf, lwr, kdr, br, kkr, vr, rr, yf_ref, yr_ref, s_ref):
    @pl.when(pl.program_id(1) == 0)
    def _():
        s_ref[...] = jnp.zeros(s_ref.shape, F32)

    c = WKV_CHUNK
    n_sub = lwf.shape[1] // c
    n_groups = lwf.shape[2] // GROUP_LANES
    dirs = (((lwf, kdf, bf_, kkf, vf, rf), yf_ref), ((lwr, kdr, br, kkr, vr, rr), yr_ref))
    chains, keys = [], []
    for u in range(n_sub):
        rs = slice(u * c, (u + 1) * c)
        for j in range(n_groups):
            sl = slice(j * GROUP_LANES, (j + 1) * GROUP_LANES)
            for d, (refs, _) in enumerate(dirs):
                lw, kd, bb, kk, v, r = (ref[0, rs, sl] for ref in refs)
                chains.append((lw, kd.astype(F32), bb.astype(F32), kk.astype(F32), v, r.astype(F32),
                               d == 1))
                keys.append((u, j, d))
    prepared = dict(zip(keys, _wkv_prepare(chains)))

    bmask = _bd_mask()
    for j in range(n_groups):
        sl = slice(j * GROUP_LANES, (j + 1) * GROUP_LANES)
        for d, (_, y_ref) in enumerate(dirs):
            s_t = s_ref[d, :, sl]
            for u in (range(n_sub) if d == 0 else reversed(range(n_sub))):
                rg, y_til, h_t, decay_t = prepared[(u, j, d)]
                z = _dot(rg, _block_diag(s_t.astype(BF16), bmask))
                y_ref[0, u * c:(u + 1) * c, sl] = (y_til + z[:c]).astype(y_ref.dtype)
                s_t = s_t * decay_t + z[c:] + h_t
            s_ref[d, :, sl] = s_t


def _wkv_scan(lw0, lw1, kd0, kd1, b0, b1, kk, v, r, ctx_len):
    b, nt, d = v.shape
    c = WKV_CHUNKS_PER_STEP * WKV_CHUNK
    assert ctx_len % c == 0 and nt % c == 0
    n_steps = nt // c
    ctx_chunks = ctx_len // c

    def fwd(i, s):
        return (i, s, 0)

    def rev(i, s):
        return (i, jnp.where(s < ctx_chunks, ctx_chunks - 1 - s, n_steps - 1 + ctx_chunks - s), 0)

    blk_f = pl.BlockSpec((1, c, d), fwd)
    blk_r = pl.BlockSpec((1, c, d), rev)
    return pl.pallas_call(
        _wkv_kernel,
        out_shape=(jax.ShapeDtypeStruct((b, nt, d), BF16), jax.ShapeDtypeStruct((b, nt, d), BF16)),
        grid=(b, n_steps),
        in_specs=[blk_f] * 6 + [blk_r] * 6,
        out_specs=(blk_f, blk_r),
        scratch_shapes=[pltpu.VMEM((2, RWKV_HEAD, d), F32)],
        compiler_params=_cparams(2),
        name="wkv_scan",
    )(lw0, kd0, b0, kk, v, r, lw1, kd1, b1, kk, v, r)


def _read_rwkv_mix(refs):
    yf_ref, yr_ref, r_ref, v_ref, kd0_ref, kd1_ref, gate_ref, rk_ref, lg_ref, lb_ref, seg_ref = refs
    wkv = yf_ref[0].astype(F32) + yr_ref[0].astype(F32)
    inv_n = 1.0 / RWKV_HEAD
    mu = _group_sum(wkv, seg_ref, two_pass=True) * inv_n
    dev = wkv - mu
    var = _group_sum(dev * dev, seg_ref) * inv_n
    gn = dev * lax.rsqrt(var + GN_EPS) * lg_ref[...] + lb_ref[...]
    rk = r_ref[0].astype(F32) * rk_ref[...]
    kd_sum = kd0_ref[0].astype(F32) + kd1_ref[0].astype(F32)
    bonus = _group_sum(rk * kd_sum, seg_ref) * v_ref[0].astype(F32)
    return ((gn + bonus) * gate_ref[0].astype(F32)).astype(BF16)


def _rwkv_mixer(yf, yr, r, v, kd0, kd1, gate, r_k, ln_g, ln_b, seg, ctx_len):
    b, nt, d = v.shape
    tm = ROW_TILE
    off = ctx_len // tm
    tok = pl.BlockSpec((1, tm, d), lambda i, t: (i, t + off, 0))
    specs = [tok] * 7 + [_const_spec((1, d))] * 3 + [_const_spec((GROUP_LANES, GROUP_LANES))]
    args = [yf, yr, r, v, kd0, kd1, gate, r_k.reshape(1, d), ln_g.reshape(1, d), ln_b.reshape(1, d), seg]
    return _Rows(_read_rwkv_mix, specs, args, (b, nt - ctx_len, d))


def _rope_tables(ctx_len, seq_len):
    f32 = np.float32
    rows = seq_len // GRID_W
    row = np.repeat(np.arange(rows, dtype=f32), GRID_W)
    col = np.tile(np.arange(GRID_W, dtype=f32), rows)
    inv_freq = np.power(f32(ROPE_THETA), -np.arange(ROPE_PAIRS, dtype=f32) / f32(ROPE_PAIRS)).astype(f32)
    row_ang, col_ang = row[:, None] * inv_freq, col[:, None] * inv_freq
    ang = np.concatenate([row_ang, row_ang, col_ang, col_ang], axis=1).astype(np.float64)
    cos = np.concatenate([np.ones((ctx_len, HEAD_DIM)), np.cos(ang)], axis=0)
    sin = np.concatenate([np.zeros((ctx_len, HEAD_DIM)), np.sin(ang)], axis=0)
    sign = np.concatenate([-np.ones(ROPE_PAIRS), np.ones(ROPE_PAIRS)] * 2)
    reps = V7X_LANES // HEAD_DIM
    return (jnp.asarray(np.tile(cos, (1, reps)), F32), jnp.asarray(np.tile(sin * sign, (1, reps)), F32))


def kernel(x, c, ctx, c_ctx, mod_w, mod_b, norm1_g, norm2_g, ffn_wg, ffn_wu, ffn_wd, attn_wqkv,
           attn_q_gain, attn_k_gain, attn_wo, rwkv_mix, rwkv_wrkv, rwkv_w0, rwkv_w1, rwkv_w2, rwkv_a0,
           rwkv_a1, rwkv_a2, rwkv_g1, rwkv_g2, rwkv_k_k, rwkv_k_a, rwkv_r_k, rwkv_ln_g, rwkv_ln_b,
           rwkv_wo, final_g):
    b, seq_len, d = x.shape
    ctx_len = ctx.shape[1]
    depth = mod_w.shape[0]
    assert depth == 2 and d == N_HEADS * HEAD_DIM
    assert ctx_len % ROW_TILE == 0 and seq_len % ROW_TILE == 0 and seq_len % GRID_W == 0
    ctx_tiles = ctx_len // ROW_TILE

    n_rows = -(-(b + 1) // 8) * 8
    c_rows = jnp.concatenate([c, c_ctx[None], jnp.zeros((n_rows - b - 1, d), F32)], axis=0)
    m_all = _modulation(c_rows, mod_w, mod_b).reshape(depth, n_rows, 6, d)
    pad = jnp.zeros((depth, b, N_MOD_ROWS - 6, d), F32)
    lat = jnp.concatenate([m_all[:, :b], pad], axis=2)
    con = jnp.concatenate([jnp.broadcast_to(m_all[:, b:b + 1], (depth, b, 6, d)), pad], axis=2)
    mods = jnp.stack([con, lat], axis=2)

    cos2, sin2 = _rope_tables(ctx_len, seq_len)
    seg = jnp.asarray(np.kron(np.eye(HEADS_PER_GROUP), np.ones((HEAD_DIM, HEAD_DIM))), BF16)

    stream = _stream_of(ctx, x)

    q, k, vt = _qkv_project(stream, mods[0], norm1_g[0], attn_wqkv[0], attn_q_gain[0], attn_k_gain[0],
                            cos2, sin2, seg, ctx_tiles)
    o_ctx, o_lat = _attention(q, k, vt, _score_bound(attn_q_gain[0], attn_k_gain[0]), ctx_len)
    xs = _out_ffn(_attention_mixer(o_ctx, o_lat), stream, mods[0], norm2_g[0], attn_wo[0], ffn_wg[0],
                  ffn_wu[0], ffn_wd[0], final_g, first_tile=0, ctx_tiles=ctx_tiles, final_norm=False)

    r, v, kk, gate, lw0, lw1, kd0, kd1, b0, b1 = _rwkv_project(
        xs, mods[1], norm1_g[1], rwkv_mix[0], rwkv_wrkv[0], rwkv_w0[0], rwkv_w1[0], rwkv_w2[0],
        rwkv_a0[0], rwkv_a1[0], rwkv_a2[0], rwkv_g1[0], rwkv_g2[0], rwkv_k_k[0], rwkv_k_a[0], seg,
        ctx_len)
    yf, yr = _wkv_scan(lw0, lw1, kd0, kd1, b0, b1, kk, v, r, ctx_len)
    mix = _rwkv_mixer(yf, yr, r, v, kd0, kd1, gate, rwkv_r_k[0].reshape(-1), rwkv_ln_g[0],
                      rwkv_ln_b[0], seg, ctx_len)
    return _out_ffn(mix, _rows_from(xs, ctx_tiles), mods[1], norm2_g[1], rwkv_wo[0], ffn_wg[1], ffn_wu[1],
                    ffn_wd[1], final_g, first_tile=ctx_tiles, ctx_tiles=ctx_tiles, final_norm=True)
```

```python
import functools
from typing import Callable, NamedTuple

import jax
import jax.numpy as jnp
import numpy as np
from jax import lax
from jax.experimental import pallas as pl
from jax.experimental.pallas import tpu as pltpu

F32 = jnp.float32
BF16 = jnp.bfloat16

NORM_EPS = 1e-6
GN_EPS = 64e-5
GRID_W = 64
N_HEADS = 16
N_KV_HEADS = 4
KV_GROUP = N_HEADS // N_KV_HEADS
HEAD_DIM = 64
ROPE_THETA = 10000.0
ROPE_PAIRS = HEAD_DIM // 4
RWKV_HEAD = 64
DECAY_SCALE = float(np.exp(-0.5))
LOG2_E = float(np.log2(np.e))
EXP2_SAFE_RANGE = 120.0

V7X_LANES = 128
V7X_MXU_DIM = 256
V7X_VMEM_LIMIT_BYTES = 60000 * 1024

ROW_TILE = 256
GROUP_LANES = V7X_MXU_DIM
HEADS_PER_GROUP = GROUP_LANES // HEAD_DIM
BF16_SUBLANES = 16
ATTN_V_ROWS = HEAD_DIM + BF16_SUBLANES
ATTN_CHUNKS_PER_PV = 11
WKV_CHUNK = 64
WKV_CHUNKS_PER_STEP = 2
N_MOD_ROWS = 8


def _cparams(n_axes):
    return pltpu.CompilerParams(
        dimension_semantics=("arbitrary",) * n_axes,
        vmem_limit_bytes=V7X_VMEM_LIMIT_BYTES,
    )


def _const_spec(shape):
    nd = len(shape)
    return pl.BlockSpec(shape, lambda *_: (0,) * nd, pipeline_mode=pl.Buffered(1))


def _dot(a, b):
    return jnp.dot(a, b, preferred_element_type=F32)


def _dot_nt(a, b):
    return lax.dot_general(a, b, (((1,), (1,)), ((), ())), preferred_element_type=F32)


def _norm_mod(x, g, shift, scale):
    ms = jnp.mean(x * x, axis=-1, keepdims=True)
    return (x * lax.rsqrt(ms + NORM_EPS) * g) * (1.0 + scale) + shift


def _group_sum(x, seg_ref, two_pass=False):
    seg = seg_ref[...]
    outs = []
    for j in range(x.shape[1] // GROUP_LANES):
        xs = x[:, j * GROUP_LANES:(j + 1) * GROUP_LANES]
        hi = xs.astype(BF16)
        out = _dot(hi, seg)
        if two_pass:
            out = out + _dot((xs - hi.astype(F32)).astype(BF16), seg)
        outs.append(out)
    return outs[0] if len(outs) == 1 else jnp.concatenate(outs, axis=1)


def _mod_kernel(c_ref, w_ref, b_ref, o_ref):
    c = c_ref[...]
    s = c * jax.nn.sigmoid(c)
    o_ref[0] = _dot(s.astype(BF16), w_ref[0]) + b_ref[0]


def _modulation(c_rows, mod_w, mod_b):
    depth, d, n = mod_w.shape
    rows = c_rows.shape[0]
    tn = n // 4
    return pl.pallas_call(
        _mod_kernel,
        out_shape=jax.ShapeDtypeStruct((depth, rows, n), F32),
        grid=(depth, n // tn),
        in_specs=[
            pl.BlockSpec((rows, d), lambda i, j: (0, 0)),
            pl.BlockSpec((1, d, tn), lambda i, j: (i, 0, j)),
            pl.BlockSpec((1, 1, tn), lambda i, j: (i, 0, j)),
        ],
        out_specs=pl.BlockSpec((1, rows, tn), lambda i, j: (i, 0, j)),
        compiler_params=_cparams(2),
        name="modulation",
    )(c_rows, mod_w.astype(BF16), mod_b.reshape(depth, 1, n))


def _rope(x, cos, sin_signed):
    w = x.shape[1]
    lane = lax.broadcasted_iota(jnp.int32, x.shape, 1)
    first_half = (lane % (2 * ROPE_PAIRS)) < ROPE_PAIRS
    partner = jnp.where(first_half, pltpu.roll(x, w - ROPE_PAIRS, 1), pltpu.roll(x, ROPE_PAIRS, 1))
    return x * cos + partner * sin_signed


class _Rows(NamedTuple):
    read: Callable
    specs: list
    args: list
    shape: tuple


def _stream_of(ctx, x):
    b, c, d = ctx.shape
    tm = ROW_TILE
    ctx_tiles = c // tm

    def read(refs):
        c_ref, x_ref = refs
        return jnp.where(pl.program_id(1) < ctx_tiles, c_ref[0], x_ref[0])

    specs = [pl.BlockSpec((1, tm, d), lambda i, t: (i, jnp.minimum(t, ctx_tiles - 1), 0)),
             pl.BlockSpec((1, tm, d), lambda i, t: (i, jnp.maximum(t - ctx_tiles, 0), 0))]
    return _Rows(read, specs, [ctx, x], (b, c + x.shape[1], d))


def _rows_from(xs, tile_offset):
    b, n, d = xs.shape
    spec = pl.BlockSpec((1, ROW_TILE, d), lambda i, t: (i, t + tile_offset, 0))
    return _Rows(lambda refs: refs[0][0], [spec], [xs], (b, n - tile_offset * ROW_TILE, d))


def _qkv_kernel(*refs, nq, nk, read_x, n_x):
    x = read_x(refs[:n_x])
    (mod_ref, g_ref, w_ref, qg_ref, kg_ref, cos_ref, sin_ref, seg_ref, ek_ref, ev_ref,
     q_ref, k_ref, v_ref) = refs[n_x:]
    mod = mod_ref[0, 0]
    h = _norm_mod(x, g_ref[...], mod[0:1], mod[1:2]).astype(BF16)
    qkv = _dot(h, w_ref[...])
    q, k, v = qkv[:, :nq], qkv[:, nq:nq + nk], qkv[:, nq + nk:]
    cos2, sin2 = cos_ref[...], sin_ref[...]

    def head_norm_rope(z, gain):
        reps = z.shape[1] // V7X_LANES
        cos = jnp.concatenate([cos2] * reps, axis=1)
        sin = jnp.concatenate([sin2] * reps, axis=1)
        ss = _group_sum(z * z, seg_ref)
        zn = z * lax.rsqrt(ss * (1.0 / HEAD_DIM) + NORM_EPS) * gain
        return _rope(zn, cos, sin)

    qn = head_norm_rope(q, qg_ref[...]) * (HEAD_DIM ** -0.5 * LOG2_E)
    q_ref[0] = qn.astype(BF16)
    kn = head_norm_rope(k, kg_ref[...]).astype(BF16)
    k_t = _dot(kn, ek_ref[...]).astype(BF16)
    v_bf = v.astype(BF16)
    ones = jnp.ones((ATTN_V_ROWS - HEAD_DIM, v_bf.shape[0]), BF16)
    shift_lane = lax.broadcasted_iota(jnp.int32, (v_bf.shape[0], V7X_LANES), 1) == HEAD_DIM
    for j in range(N_KV_HEADS):
        k_ref[0, j] = jnp.where(shift_lane, jnp.ones((), BF16), k_t[:, j * V7X_LANES:(j + 1) * V7X_LANES])
        v_ref[0, j, 0:HEAD_DIM, :] = _dot_nt(ev_ref[j], v_bf).astype(BF16)
        v_ref[0, j, HEAD_DIM:, :] = ones


def _head_select(n_heads, rows):
    sel = np.zeros((n_heads, rows, n_heads * HEAD_DIM), np.float32)
    for j in range(n_heads):
        sel[j, np.arange(HEAD_DIM), j * HEAD_DIM + np.arange(HEAD_DIM)] = 1.0
    return sel


def _qkv_project(rows, mods, g1, wqkv, q_gain, k_gain, cos2, sin2, seg, ctx_tiles):
    b, nt, d = rows.shape
    nq, nk = N_HEADS * HEAD_DIM, N_KV_HEADS * HEAD_DIM
    tm = ROW_TILE
    ek = np.zeros((nk, N_KV_HEADS * V7X_LANES), np.float32)
    for j in range(N_KV_HEADS):
        ek[j * HEAD_DIM + np.arange(HEAD_DIM), j * V7X_LANES + np.arange(HEAD_DIM)] = 1.0
    ev = _head_select(N_KV_HEADS, HEAD_DIM)
    kern = functools.partial(_qkv_kernel, nq=nq, nk=nk, read_x=rows.read, n_x=len(rows.specs))
    return pl.pallas_call(
        kern,
        out_shape=(
            jax.ShapeDtypeStruct((b, nt, nq), BF16),
            jax.ShapeDtypeStruct((b, N_KV_HEADS, nt, V7X_LANES), BF16),
            jax.ShapeDtypeStruct((b, N_KV_HEADS, ATTN_V_ROWS, nt), BF16),
        ),
        grid=(b, nt // tm),
        in_specs=rows.specs + [
            pl.BlockSpec((1, 1, N_MOD_ROWS, d), lambda i, t: (i, jnp.where(t < ctx_tiles, 0, 1), 0, 0)),
            _const_spec((1, d)),
            _const_spec((d, nq + 2 * nk)),
            _const_spec((1, nq)),
            _const_spec((1, nk)),
            pl.BlockSpec((tm, V7X_LANES), lambda i, t: (t, 0)),
            pl.BlockSpec((tm, V7X_LANES), lambda i, t: (t, 0)),
            _const_spec((GROUP_LANES, GROUP_LANES)),
            _const_spec(ek.shape),
            _const_spec(ev.shape),
        ],
        out_specs=(
            pl.BlockSpec((1, tm, nq), lambda i, t: (i, t, 0)),
            pl.BlockSpec((1, N_KV_HEADS, tm, V7X_LANES), lambda i, t: (i, 0, t, 0)),
            pl.BlockSpec((1, N_KV_HEADS, ATTN_V_ROWS, tm), lambda i, t: (i, 0, 0, t)),
        ),
        compiler_params=_cparams(2),
        name="qkv_project",
    )(*rows.args, mods, g1.reshape(1, d), wqkv.astype(BF16),
      jnp.tile(q_gain, N_HEADS).reshape(1, nq), jnp.tile(k_gain, N_KV_HEADS).reshape(1, nk),
      cos2, sin2, seg, jnp.asarray(ek, BF16), jnp.asarray(ev, BF16))


def _attn_kernel(*refs, n_sub, tk, n_chunks, unroll):
    bound_ref = refs[0]
    q_refs = refs[1:1 + n_sub]
    k_ref, vt_ref, o_ref, qt_ref, m_ref, acc_ref, sa_ref, sb_ref, p_ref = refs[1 + n_sub:]
    ts = q_refs[0].shape[1]
    width = qt_ref.shape[1]
    for s, q_ref in enumerate(q_refs):
        q_t = q_ref[0].astype(F32).T
        for g in range(KV_GROUP):
            col = (s * KV_GROUP + g) * ts
            qt_ref[0:HEAD_DIM, col:col + ts] = q_t[g * HEAD_DIM:(g + 1) * HEAD_DIM].astype(BF16)
    qt_ref[HEAD_DIM:, :] = jnp.full((qt_ref.shape[0] - HEAD_DIM, width), bound_ref[0], F32).astype(BF16)
    acc_ref[...] = jnp.zeros(acc_ref.shape, F32)

    def chunk(c, n=1):
        return pl.ds(c * tk if isinstance(c, int) else pl.multiple_of(c * tk, n * tk), n * tk)

    def scores(c):
        return _dot(k_ref[0, 0, chunk(c), :], qt_ref[...])

    @pl.when(bound_ref[1] > 0.5)
    def _():
        def accumulate(c0, n):
            for u in range(n):
                p_ref[u * tk:(u + 1) * tk, :] = jnp.exp2(scores(c0 + u)).astype(BF16)
            acc_ref[...] += _dot(vt_ref[0, 0, :, chunk(c0, n)], p_ref[0:n * tk, :])

        per_trip = p_ref.shape[0] // tk

        def trip(j, carry):
            accumulate(per_trip * j, per_trip)
            return carry

        n_trips, rem = n_chunks // per_trip, n_chunks % per_trip
        if n_trips:
            lax.fori_loop(0, n_trips, trip, 0)
        if rem:
            accumulate(n_trips * per_trip, rem)

    @pl.when(bound_ref[1] <= 0.5)
    def _():
        _attn_online_softmax(scores, chunk, vt_ref, m_ref, acc_ref, sa_ref, sb_ref, p_ref,
                             n_chunks=n_chunks, unroll=unroll)

    acc = acc_ref[...]
    ot = acc[0:HEAD_DIM] / acc[HEAD_DIM:HEAD_DIM + 1]
    for s in range(n_sub):
        heads = [ot[:, (s * KV_GROUP + g) * ts:(s * KV_GROUP + g + 1) * ts] for g in range(KV_GROUP)]
        o_ref[0, s * ts:(s + 1) * ts, :] = jnp.concatenate(heads, axis=0).T.astype(o_ref.dtype)


def _attn_online_softmax(scores, chunk, vt_ref, m_ref, acc_ref, sa_ref, sb_ref, p_ref, *, n_chunks, unroll):
    m_ref[...] = jnp.full(m_ref.shape, -jnp.inf, F32)
    tk = sa_ref.shape[0]

    def update(c, s_ref):
        vtc = vt_ref[0, 0, :, chunk(c)]
        alphas = []
        for cb in range(s_ref.shape[1] // V7X_LANES):
            cols = slice(cb * V7X_LANES, (cb + 1) * V7X_LANES)
            st = s_ref[:, cols]
            m_prev = m_ref[:, cols]
            m_new = jnp.maximum(m_prev, jnp.max(st, axis=0, keepdims=True))
            alphas.append(jnp.exp2(m_prev - m_new))
            p_ref[0:tk, cols] = jnp.exp2(st - m_new).astype(BF16)
            m_ref[:, cols] = m_new
        alpha = jnp.concatenate(alphas, axis=1)
        acc_ref[...] = alpha * acc_ref[...] + _dot(vtc, p_ref[0:tk, :])

    slots = (sa_ref, sb_ref)
    sa_ref[...] = scores(0)

    def body(j, carry):
        c0 = unroll * j
        for u in range(unroll):
            slots[(u + 1) % 2][...] = scores(c0 + u + 1)
            update(c0 + u, slots[u % 2])
        return carry

    if n_chunks > 1:
        lax.fori_loop(0, (n_chunks - 1) // unroll, body, 0)
    update(n_chunks - 1, slots[0])


def _attention_call(bound, q, k, vt, *, row0, n_rows, n_keys, n_sub, name):
    b, _, nq = q.shape
    ts = ROW_TILE
    tk = ROW_TILE
    tq = n_sub * ts
    n_chunks = n_keys // tk
    assert n_rows % tq == 0 and row0 % ts == 0 and n_keys % tk == 0 and n_chunks % 2 == 1
    unroll = max(u for u in (2, 4) if (n_chunks - 1) % u == 0)
    kern = functools.partial(_attn_kernel, n_sub=n_sub, tk=tk, n_chunks=n_chunks, unroll=unroll)
    width = n_sub * KV_GROUP * ts

    def q_spec(s):
        return pl.BlockSpec((1, ts, GROUP_LANES), lambda i, j, t: (i, row0 // ts + n_sub * t + s, j))

    return pl.pallas_call(
        kern,
        out_shape=jax.ShapeDtypeStruct((b, n_rows, nq), BF16),
        grid=(b, N_KV_HEADS, n_rows // tq),
        in_specs=[pl.BlockSpec(memory_space=pltpu.SMEM)] + [q_spec(s) for s in range(n_sub)] + [
            pl.BlockSpec((1, 1, n_keys, V7X_LANES), lambda i, j, t: (i, j, 0, 0)),
            pl.BlockSpec((1, 1, ATTN_V_ROWS, n_keys), lambda i, j, t: (i, j, 0, 0)),
        ],
        out_specs=pl.BlockSpec((1, tq, GROUP_LANES), lambda i, j, t: (i, t, j)),
        scratch_shapes=[
            pltpu.VMEM((V7X_LANES, width), BF16),
            pltpu.VMEM((1, width), F32),
            pltpu.VMEM((ATTN_V_ROWS, width), F32),
            pltpu.VMEM((tk, width), F32),
            pltpu.VMEM((tk, width), F32),
            pltpu.VMEM((min(ATTN_CHUNKS_PER_PV, n_chunks) * tk, width), BF16),
        ],
        compiler_params=_cparams(3),
        name=name,
    )(bound, *([q] * n_sub), k, vt)


def _score_bound(q_gain, k_gain):
    bound = (HEAD_DIM * (HEAD_DIM ** -0.5 * LOG2_E) * 1.02) * jnp.max(jnp.abs(q_gain)) * jnp.max(jnp.abs(k_gain))
    bound = bound.astype(BF16).astype(F32)
    return jnp.stack([-bound, (2.0 * bound <= EXP2_SAFE_RANGE).astype(F32)])


def _attention(q, k, vt, bound, ctx_len):
    nt = q.shape[1]
    n_lat = nt - ctx_len
    n_sub = max(s for s in (1, 2, 4) if n_lat % (s * ROW_TILE) == 0)
    o_ctx = _attention_call(bound, q, k, vt, row0=0, n_rows=ctx_len, n_keys=ctx_len, n_sub=1,
                            name="flash_attention_ctx")
    o_lat = _attention_call(bound, q, k, vt, row0=ctx_len, n_rows=n_lat, n_keys=nt, n_sub=n_sub,
                            name="flash_attention")
    return o_ctx, o_lat


def _out_ffn_kernel(*refs, final_norm, read_mix, n_mix, read_x, n_x):
    y = read_mix(refs[:n_mix])
    x = read_x(refs[n_mix:n_mix + n_x])
    mod_ref, g_ref, wo_ref, wg_ref, wu_ref, wd_ref, fg_ref, o_ref = refs[n_mix + n_x:]
    mod = mod_ref[0, 0]
    x1 = x + mod[2:3] * _dot(y, wo_ref[...])
    h2 = _norm_mod(x1, g_ref[...], mod[3:4], mod[4:5]).astype(BF16)
    a = _dot(h2, wg_ref[...])
    u = _dot(h2, wu_ref[...])
    hid = (a * jax.nn.sigmoid(a) * u).astype(BF16)
    x2 = x1 + mod[5:6] * _dot(hid, wd_ref[...])
    if final_norm:
        ms = jnp.mean(x2 * x2, axis=-1, keepdims=True)
        x2 = x2 * lax.rsqrt(ms + NORM_EPS) * fg_ref[...]
    o_ref[0] = x2


def _attention_mixer(o_ctx, o_lat):
    d = o_lat.shape[2]
    tm = ROW_TILE
    ctx_tiles = o_ctx.shape[1] // tm

    def read(refs):
        yc_ref, yl_ref = refs
        return jnp.where(pl.program_id(1) < ctx_tiles, yc_ref[0], yl_ref[0])

    specs = [pl.BlockSpec((1, tm, d), lambda i, t: (i, jnp.minimum(t, ctx_tiles - 1), 0)),
             pl.BlockSpec((1, tm, d), lambda i, t: (i, jnp.maximum(t - ctx_tiles, 0), 0))]
    return _Rows(read, specs, [o_ctx, o_lat], (o_lat.shape[0], o_ctx.shape[1] + o_lat.shape[1], d))


def _out_ffn(mix, rows, mods, g2, wo, wg, wu, wd, final_g, *, first_tile, ctx_tiles, final_norm):
    b, n, d = rows.shape
    assert mix.shape == rows.shape
    f = wg.shape[1]
    tm = ROW_TILE
    kern = functools.partial(_out_ffn_kernel, final_norm=final_norm, read_mix=mix.read,
                             n_mix=len(mix.specs), read_x=rows.read, n_x=len(rows.specs))

    def mod_sel(i, t):
        return (i, jnp.where(t + first_tile < ctx_tiles, 0, 1), 0, 0)

    return pl.pallas_call(
        kern,
        out_shape=jax.ShapeDtypeStruct((b, n, d), F32),
        grid=(b, n // tm),
        in_specs=mix.specs + rows.specs + [
            pl.BlockSpec((1, 1, N_MOD_ROWS, d), mod_sel),
            _const_spec((1, d)),
            _const_spec((d, d)),
            _const_spec((d, f)),
            _const_spec((d, f)),
            _const_spec((f, d)),
            _const_spec((1, d)),
        ],
        out_specs=pl.BlockSpec((1, tm, d), lambda i, t: (i, t, 0)),
        compiler_params=_cparams(2),
        name="out_ffn_final" if final_norm else "out_ffn",
    )(*mix.args, *rows.args, mods, g2.reshape(1, d), wo.astype(BF16), wg.astype(BF16), wu.astype(BF16),
      wd.astype(BF16), final_g.reshape(1, d))


def _rwkv_proj_kernel(x_ref, xp_ref, xn_ref, mod_ref, g_ref, mix_ref, wr_ref, wk_ref, wv_ref,
                      w1_ref, w2_ref, a1_ref, a2_ref, g1_ref, g2_ref, w0_ref, a0_ref, kk_ref, ka_ref,
                      seg_ref,
                      r_ref, v_ref, kkn_ref, gate_ref, lw0_ref, lw1_ref, kd0_ref, kd1_ref,
                      b0_ref, b1_ref, *, tm, seq_starts, seq_ends):
    t = pl.program_id(1)
    mod = mod_ref[0, 0]
    g = g_ref[...]
    h = _norm_mod(x_ref[0], g, mod[0:1], mod[1:2])
    h_prev = _norm_mod(xp_ref[0], g, mod[0:1], mod[1:2])[7:8]
    h_next = _norm_mod(xn_ref[0], g, mod[0:1], mod[1:2])[0:1]
    row = lax.broadcasted_iota(jnp.int32, h.shape, 0)
    pos = row + t * tm
    at_start = functools.reduce(jnp.logical_or, [pos == s for s in seq_starts])
    at_end = functools.reduce(jnp.logical_or, [pos == e for e in seq_ends])
    before = jnp.where(row == 0, h_prev, pltpu.roll(h, 1, 0))
    after = jnp.where(row == tm - 1, h_next, pltpu.roll(h, tm - 1, 0))
    before = jnp.where(at_start, 0.0, before)
    after = jnp.where(at_end, 0.0, after)
    xx = 0.5 * (before + after) - h
    mix = mix_ref[...]

    def lerp(j):
        return (h + xx * mix[j:j + 1]).astype(BF16)

    r_ref[0] = _dot(lerp(0), wr_ref[...]).astype(r_ref.dtype)
    k = _dot(lerp(2), wk_ref[...])
    v_ref[0] = _dot(lerp(3), wv_ref[...]).astype(v_ref.dtype)
    kkr = k * kk_ref[...]
    ss = _group_sum(kkr * kkr, seg_ref)
    kkn = kkr * lax.rsqrt(jnp.maximum(ss, 1e-24))
    kkn_ref[0] = kkn.astype(kkn_ref.dtype)
    xw, xa = lerp(1), lerp(4)
    ka = ka_ref[...]
    d_model = h.shape[1]
    zw = _dot(jnp.tanh(_dot(xw, w1_ref[...])).astype(BF16), w2_ref[...])
    za = _dot(_dot(xa, a1_ref[...]).astype(BF16), a2_ref[...])
    for d, (lw_ref, kd_ref, b_ref) in enumerate(((lw0_ref, kd0_ref, b0_ref), (lw1_ref, kd1_ref, b1_ref))):
        z = w0_ref[d:d + 1] + zw[:, d * d_model:(d + 1) * d_model]
        lw_ref[0] = -DECAY_SCALE * jax.nn.sigmoid(z)
        a = jax.nn.sigmoid(a0_ref[d:d + 1] + za[:, d * d_model:(d + 1) * d_model])
        kd_ref[0] = (k * (1.0 + (a - 1.0) * ka)).astype(kd_ref.dtype)
        b_ref[0] = (kkn * a).astype(b_ref.dtype)
    gate = _dot(jax.nn.sigmoid(_dot(lerp(5), g1_ref[...])).astype(BF16), g2_ref[...])
    gate_ref[0] = gate.astype(gate_ref.dtype)


def _rwkv_project(xs, mods, g1n, mix, w_rkv, w0, w1, w2, a0, a1, a2, gw1, gw2, k_k, k_a, seg,
                  ctx_len):
    b, nt, d = xs.shape
    tm = ROW_TILE
    halo = 8
    per = tm // halo
    n_halo = nt // halo
    kern = functools.partial(_rwkv_proj_kernel, tm=tm, seq_starts=(0, ctx_len),
                             seq_ends=(ctx_len - 1, nt - 1))
    tok = pl.BlockSpec((1, tm, d), lambda i, t: (i, t, 0))
    bf = lambda w: w.astype(BF16)

    def side_by_side(w):
        return jnp.concatenate([w[0], w[1]], axis=1)

    def block_diagonal(w):
        zero = jnp.zeros_like(w[0])
        return jnp.concatenate([jnp.concatenate([w[0], zero], axis=1),
                                jnp.concatenate([zero, w[1]], axis=1)], axis=0)

    w1, a1 = side_by_side(w1), side_by_side(a1)
    w2, a2 = block_diagonal(w2), block_diagonal(a2)
    outs = pl.pallas_call(
        kern,
        out_shape=tuple(jax.ShapeDtypeStruct((b, nt, d), dt)
                        for dt in (BF16, BF16, BF16, BF16, F32, F32, BF16, BF16, BF16, BF16)),
        grid=(b, nt // tm),
        in_specs=[
            tok,
            pl.BlockSpec((1, halo, d), lambda i, t: (i, jnp.maximum(t * per - 1, 0), 0)),
            pl.BlockSpec((1, halo, d), lambda i, t: (i, jnp.minimum((t + 1) * per, n_halo - 1), 0)),
            pl.BlockSpec((1, 1, N_MOD_ROWS, d), lambda i, t: (i, jnp.where(t * tm < ctx_len, 0, 1), 0, 0)),
            _const_spec((1, d)),
            _const_spec((N_MOD_ROWS, d)),
            _const_spec((d, d)), _const_spec((d, d)), _const_spec((d, d)),
            _const_spec(w1.shape), _const_spec(w2.shape), _const_spec(a1.shape), _const_spec(a2.shape),
            _const_spec(gw1.shape), _const_spec(gw2.shape),
            _const_spec((2, d)), _const_spec((2, d)), _const_spec((1, d)), _const_spec((1, d)),
            _const_spec((GROUP_LANES, GROUP_LANES)),
        ],
        out_specs=tuple(tok for _ in range(10)),
        compiler_params=_cparams(2),
        name="rwkv_project",
    )(xs, xs, xs, mods, g1n.reshape(1, d),
      jnp.concatenate([mix, jnp.zeros((N_MOD_ROWS - mix.shape[0], d), F32)], axis=0),
      bf(w_rkv[0]), bf(w_rkv[1]), bf(w_rkv[2]), bf(w1), bf(w2), bf(a1), bf(a2), bf(gw1), bf(gw2),
      w0, a0, k_k.reshape(1, d), k_a.reshape(1, d), seg)
    return outs


def _block_diag(x_bf, mask):
    return jnp.where(mask, jnp.concatenate([x_bf] * HEADS_PER_GROUP, axis=0), jnp.zeros((), BF16))


def _cumsum_rows(x, reverse):
    n = x.shape[0]
    row = lax.broadcasted_iota(jnp.int32, x.shape, 0)
    shift = 1
    while shift < n:
        if reverse:
            x = x + jnp.where(row < n - shift, pltpu.roll(x, n - shift, 0), 0.0)
        else:
            x = x + jnp.where(row >= shift, pltpu.roll(x, shift, 0), 0.0)
        shift *= 2
    return x


def _block_transpose(x):
    xt = x.T
    return jnp.concatenate([xt[j * RWKV_HEAD:(j + 1) * RWKV_HEAD] for j in range(HEADS_PER_GROUP)], axis=1)


def _wkv_masks(reverse):
    c = WKV_CHUNK
    rows = lax.broadcasted_iota(jnp.int32, (c, GROUP_LANES), 0)
    cols = lax.broadcasted_iota(jnp.int32, (c, GROUP_LANES), 1) % RWKV_HEAD
    if reverse:
        return cols > rows, cols >= rows
    return cols < rows, cols <= rows


def _bd_mask():
    bd_r = lax.broadcasted_iota(jnp.int32, (GROUP_LANES, GROUP_LANES), 0) // RWKV_HEAD
    bd_c = lax.broadcasted_iota(jnp.int32, (GROUP_LANES, GROUP_LANES), 1) // RWKV_HEAD
    return bd_r == bd_c


def _wkv_prepare(chains):
    c = WKV_CHUNK
    n = len(chains)
    rng = range(n)
    rows = lax.broadcasted_iota(jnp.int32, (c, GROUP_LANES), 0)
    cols = lax.broadcasted_iota(jnp.int32, (c, GROUP_LANES), 1) % RWKV_HEAD
    eye = (cols == rows).astype(F32)
    bmask = _bd_mask()
    masks = {rev: _wkv_masks(rev) for rev in sorted({ch[6] for ch in chains})}
    strict = [masks[ch[6]][0] for ch in chains]
    incl = [masks[ch[6]][1] for ch in chains]
    lw, kd, bb, kk, v, r = ([ch[i] for ch in chains] for i in range(6))
    bd = lambda x: _block_diag(x.astype(BF16), bmask)

    cs = [_cumsum_rows(lw[i], chains[i][6]) for i in rng]
    total = [cs[i][0:1] if chains[i][6] else cs[i][c - 1:c] for i in rng]
    a_s = [-kk[i] * jnp.exp(cs[i] - lw[i]) for i in rng]
    g_inv = [jnp.exp(-cs[i]) for i in rng]
    b_s = [bb[i] * g_inv[i] for i in rng]
    k_s = [kd[i] * g_inv[i] for i in rng]
    r_s = [r[i] * jnp.exp(cs[i]) for i in rng]
    g_rest = [jnp.exp(total[i] - cs[i]) for i in rng]
    b_e = [bb[i] * g_rest[i] for i in rng]
    k_e = [kd[i] * g_rest[i] for i in rng]
    g_end = [jnp.exp(total[i]) for i in rng]

    lhs = [jnp.concatenate([a_s[i], r_s[i]], axis=0).astype(BF16) for i in rng]
    pb = [_dot(lhs[i], bd(_block_transpose(b_s[i]))) for i in rng]
    pk = [_dot(lhs[i], bd(_block_transpose(k_s[i]))) for i in rng]
    l_ab = [jnp.where(strict[i], pb[i][:c], 0.0) for i in rng]
    m_rb = [jnp.where(incl[i], pb[i][c:], 0.0).astype(BF16) for i in rng]
    l_ak = [jnp.where(strict[i], pk[i][:c], 0.0) for i in rng]
    m_rk = [jnp.where(incl[i], pk[i][c:], 0.0) for i in rng]

    blk_r, blk_c = rows, cols

    def coupling(i, w):
        pair = ((blk_r // (2 * w)) == (blk_c // (2 * w))) & ((blk_r // w) != (blk_c // w))
        return jnp.where(pair, l_ab[i], 0.0)

    tmat = [eye + coupling(i, 1) for i in rng]
    w = 2
    while w < c:
        if 2 * w < c:
            ee = [jnp.concatenate([coupling(i, w), coupling(i, 2 * w)], axis=0).astype(BF16) for i in rng]
            mf = [_dot(ee[i], bd(tmat[i])) for i in rng]
            df = [jnp.concatenate([tmat[i], mf[i][c:]], axis=0).astype(BF16) for i in rng]
            z = [_dot(df[i], bd(mf[i][:c])) for i in rng]
            tmat = [tmat[i] + z[i][:c] for i in rng]
            m2 = [mf[i][c:] + z[i][c:] for i in rng]
            tmat = [tmat[i] + _dot(tmat[i].astype(BF16), bd(m2[i])) for i in rng]
            w *= 4
        else:
            ed = [_dot(coupling(i, w).astype(BF16), bd(tmat[i])) for i in rng]
            tmat = [tmat[i] + _dot(tmat[i].astype(BF16), bd(ed[i])) for i in rng]
            w *= 2
    t_bf = [tmat[i].astype(BF16) for i in rng]

    b_et = [_block_transpose(b_e[i]).astype(BF16) for i in rng]
    k_et = [_block_transpose(k_e[i]).astype(BF16) for i in rng]
    zv = [_dot(jnp.concatenate([l_ak[i].astype(BF16), m_rk[i].astype(BF16), k_et[i]], axis=0), bd(v[i]))
          for i in rng]
    a_hat = [_dot(t_bf[i], bd(a_s[i])) for i in rng]
    u_til = [_dot(t_bf[i], bd(zv[i][:c])) for i in rng]
    mb = [jnp.concatenate([m_rb[i], b_et[i]], axis=0) for i in rng]
    ra = [_dot(mb[i], bd(a_hat[i])) for i in rng]
    yu = [_dot(mb[i], bd(u_til[i])) for i in rng]
    r_hat = [r_s[i] + ra[i][:c] for i in rng]
    y_til = [zv[i][c:2 * c] + yu[i][:c] for i in rng]
    g_t = [ra[i][c:] for i in rng]
    h_t = [yu[i][c:] + zv[i][2 * c:] for i in rng]
    decay_t = [_block_transpose(jnp.broadcast_to(g_end[i], (c, GROUP_LANES))) for i in rng]
    return [(jnp.concatenate([r_hat[i], g_t[i]], axis=0).astype(BF16), y_til[i], h_t[i], decay_t[i])
            for i in rng]


def _wkv_kernel(lwf, kdf, bf_, kkf, vf, rf, lwr, kdr, br, kkr, vr, rr, yf_ref, yr_ref, s_ref):
    @pl.when(pl.program_id(1) == 0)
    def _():
        s_ref[...] = jnp.zeros(s_ref.shape, F32)

    c = WKV_CHUNK
    n_sub = lwf.shape[1] // c
    n_groups = lwf.shape[2] // GROUP_LANES
    dirs = (((lwf, kdf, bf_, kkf, vf, rf), yf_ref), ((lwr, kdr, br, kkr, vr, rr), yr_ref))
    chains, keys = [], []
    for u in range(n_sub):
        rs = slice(u * c, (u + 1) * c)
        for j in range(n_groups):
            sl = slice(j * GROUP_LANES, (j + 1) * GROUP_LANES)
            for d, (refs, _) in enumerate(dirs):
                lw, kd, bb, kk, v, r = (ref[0, rs, sl] for ref in refs)
                chains.append((lw, kd.astype(F32), bb.astype(F32), kk.astype(F32), v, r.astype(F32),
                               d == 1))
                keys.append((u, j, d))
    prepared = dict(zip(keys, _wkv_prepare(chains)))

    bmask = _bd_mask()
    for j in range(n_groups):
        sl = slice(j * GROUP_LANES, (j + 1) * GROUP_LANES)
        for d, (_, y_ref) in enumerate(dirs):
            s_t = s_ref[d, :, sl]
            for u in (range(n_sub) if d == 0 else reversed(range(n_sub))):
                rg, y_til, h_t, decay_t = prepared[(u, j, d)]
                z = _dot(rg, _block_diag(s_t.astype(BF16), bmask))
                y_ref[0, u * c:(u + 1) * c, sl] = (y_til + z[:c]).astype(y_ref.dtype)
                s_t = s_t * decay_t + z[c:] + h_t
            s_ref[d, :, sl] = s_t


def _wkv_scan(lw0, lw1, kd0, kd1, b0, b1, kk, v, r, ctx_len):
    b, nt, d = v.shape
    c = WKV_CHUNKS_PER_STEP * WKV_CHUNK
    assert ctx_len % c == 0 and nt % c == 0
    n_steps = nt // c
    ctx_chunks = ctx_len // c

    def fwd(i, s):
        return (i, s, 0)

    def rev(i, s):
        return (i, jnp.where(s < ctx_chunks, ctx_chunks - 1 - s, n_steps - 1 + ctx_chunks - s), 0)

    blk_f = pl.BlockSpec((1, c, d), fwd)
    blk_r = pl.BlockSpec((1, c, d), rev)
    return pl.pallas_call(
        _wkv_kernel,
        out_shape=(jax.ShapeDtypeStruct((b, nt, d), BF16), jax.ShapeDtypeStruct((b, nt, d), BF16)),
        grid=(b, n_steps),
        in_specs=[blk_f] * 6 + [blk_r] * 6,
        out_specs=(blk_f, blk_r),
        scratch_shapes=[pltpu.VMEM((2, RWKV_HEAD, d), F32)],
        compiler_params=_cparams(2),
        name="wkv_scan",
    )(lw0, kd0, b0, kk, v, r, lw1, kd1, b1, kk, v, r)


def _read_rwkv_mix(refs):
    yf_ref, yr_ref, r_ref, v_ref, kd0_ref, kd1_ref, gate_ref, rk_ref, lg_ref, lb_ref, seg_ref = refs
    wkv = yf_ref[0].astype(F32) + yr_ref[0].astype(F32)
    inv_n = 1.0 / RWKV_HEAD
    mu = _group_sum(wkv, seg_ref, two_pass=True) * inv_n
    dev = wkv - mu
    var = _group_sum(dev * dev, seg_ref) * inv_n
    gn = dev * lax.rsqrt(var + GN_EPS) * lg_ref[...] + lb_ref[...]
    rk = r_ref[0].astype(F32) * rk_ref[...]
    kd_sum = kd0_ref[0].astype(F32) + kd1_ref[0].astype(F32)
    bonus = _group_sum(rk * kd_sum, seg_ref) * v_ref[0].astype(F32)
    return ((gn + bonus) * gate_ref[0].astype(F32)).astype(BF16)


def _rwkv_mixer(yf, yr, r, v, kd0, kd1, gate, r_k, ln_g, ln_b, seg, ctx_len):
    b, nt, d = v.shape
    tm = ROW_TILE
    off = ctx_len // tm
    tok = pl.BlockSpec((1, tm, d), lambda i, t: (i, t + off, 0))
    specs = [tok] * 7 + [_const_spec((1, d))] * 3 + [_const_spec((GROUP_LANES, GROUP_LANES))]
    args = [yf, yr, r, v, kd0, kd1, gate, r_k.reshape(1, d), ln_g.reshape(1, d), ln_b.reshape(1, d), seg]
    return _Rows(_read_rwkv_mix, specs, args, (b, nt - ctx_len, d))


def _rope_tables(ctx_len, seq_len):
    f32 = np.float32
    rows = seq_len // GRID_W
    row = np.repeat(np.arange(rows, dtype=f32), GRID_W)
    col = np.tile(np.arange(GRID_W, dtype=f32), rows)
    inv_freq = np.power(f32(ROPE_THETA), -np.arange(ROPE_PAIRS, dtype=f32) / f32(ROPE_PAIRS)).astype(f32)
    row_ang, col_ang = row[:, None] * inv_freq, col[:, None] * inv_freq
    ang = np.concatenate([row_ang, row_ang, col_ang, col_ang], axis=1).astype(np.float64)
    cos = np.concatenate([np.ones((ctx_len, HEAD_DIM)), np.cos(ang)], axis=0)
    sin = np.concatenate([np.zeros((ctx_len, HEAD_DIM)), np.sin(ang)], axis=0)
    sign = np.concatenate([-np.ones(ROPE_PAIRS), np.ones(ROPE_PAIRS)] * 2)
    reps = V7X_LANES // HEAD_DIM
    return (jnp.asarray(np.tile(cos, (1, reps)), F32), jnp.asarray(np.tile(sin * sign, (1, reps)), F32))


def kernel(x, c, ctx, c_ctx, mod_w, mod_b, norm1_g, norm2_g, ffn_wg, ffn_wu, ffn_wd, attn_wqkv,
           attn_q_gain, attn_k_gain, attn_wo, rwkv_mix, rwkv_wrkv, rwkv_w0, rwkv_w1, rwkv_w2, rwkv_a0,
           rwkv_a1, rwkv_a2, rwkv_g1, rwkv_g2, rwkv_k_k, rwkv_k_a, rwkv_r_k, rwkv_ln_g, rwkv_ln_b,
           rwkv_wo, final_g):
    b, seq_len, d = x.shape
    ctx_len = ctx.shape[1]
    depth = mod_w.shape[0]
    assert depth == 2 and d == N_HEADS * HEAD_DIM
    assert ctx_len % ROW_TILE == 0 and seq_len % ROW_TILE == 0 and seq_len % GRID_W == 0
    ctx_tiles = ctx_len // ROW_TILE

    n_rows = -(-(b + 1) // 8) * 8
    c_rows = jnp.concatenate([c, c_ctx[None], jnp.zeros((n_rows - b - 1, d), F32)], axis=0)
    m_all = _modulation(c_rows, mod_w, mod_b).reshape(depth, n_rows, 6, d)
    pad = jnp.zeros((depth, b, N_MOD_ROWS - 6, d), F32)
    lat = jnp.concatenate([m_all[:, :b], pad], axis=2)
    con = jnp.concatenate([jnp.broadcast_to(m_all[:, b:b + 1], (depth, b, 6, d)), pad], axis=2)
    mods = jnp.stack([con, lat], axis=2)

    cos2, sin2 = _rope_tables(ctx_len, seq_len)
    seg = jnp.asarray(np.kron(np.eye(HEADS_PER_GROUP), np.ones((HEAD_DIM, HEAD_DIM))), BF16)

    stream = _stream_of(ctx, x)

    q, k, vt = _qkv_project(stream, mods[0], norm1_g[0], attn_wqkv[0], attn_q_gain[0], attn_k_gain[0],
                            cos2, sin2, seg, ctx_tiles)
    o_ctx, o_lat = _attention(q, k, vt, _score_bound(attn_q_gain[0], attn_k_gain[0]), ctx_len)
    xs = _out_ffn(_attention_mixer(o_ctx, o_lat), stream, mods[0], norm2_g[0], attn_wo[0], ffn_wg[0],
                  ffn_wu[0], ffn_wd[0], final_g, first_tile=0, ctx_tiles=ctx_tiles, final_norm=False)

    r, v, kk, gate, lw0, lw1, kd0, kd1, b0, b1 = _rwkv_project(
        xs, mods[1], norm1_g[1], rwkv_mix[0], rwkv_wrkv[0], rwkv_w0[0], rwkv_w1[0], rwkv_w2[0],
        rwkv_a0[0], rwkv_a1[0], rwkv_a2[0], rwkv_g1[0], rwkv_g2[0], rwkv_k_k[0], rwkv_k_a[0], seg,
        ctx_len)
    yf, yr = _wkv_scan(lw0, lw1, kd0, kd1, b0, b1, kk, v, r, ctx_len)
    mix = _rwkv_mixer(yf, yr, r, v, kd0, kd1, gate, rwkv_r_k[0].reshape(-1), rwkv_ln_g[0],
                      rwkv_ln_b[0], seg, ctx_len)
    return _out_ffn(mix, _rows_from(xs, ctx_tiles), mods[1], norm2_g[1], rwkv_wo[0], ffn_wg[1], ffn_wu[1],
                    ffn_wd[1], final_g, first_tile=ctx_tiles, ctx_tiles=ctx_tiles, final_norm=True)
```
